```python
import math
import jax
import jax.numpy as jnp
from jax import lax
import numpy as np


D_MODEL = 2048
BATCH = 1
SEQ = 8192
DEPTH = 4

N_MIXERS = 4
N_HEADS = 16
HEAD_DIM = D_MODEL // N_HEADS
ROPE_THETA = 500000.0
ROPE_FRACTION = 4
Q_BLOCK = 128
RMS_EPS = 1e-6
FFN_HIDDEN = ((8 * D_MODEL + 3 * 256 - 1) // (3 * 256)) * 256
MLA_Q_LORA = D_MODEL // 4
MLA_KV_LORA = D_MODEL // 4
MLA_NOPE = HEAD_DIM
MLA_ROPE = HEAD_DIM // 2
MLA_V = HEAD_DIM
IDX_HEADS = 16
IDX_DIM = 64
IDX_TOPK = 256
DIFF_HEADS = N_HEADS
DIFF_DIM = HEAD_DIM // 2
POS_OFFSET_MAX = 1024

kernel_name = 'hybrid_mla_dsa_diff_fox_trunk'


def _rms_norm(x, g):
    xf = x.astype(jnp.float32)
    y = xf * lax.rsqrt(jnp.mean(xf * xf, axis=-1, keepdims=True) + RMS_EPS)
    return (y * g.astype(jnp.float32)).astype(x.dtype)


def _rope_cos_sin(positions, rot_dim):
    inv_freq = ROPE_THETA ** (-jnp.arange(0, rot_dim, 2, dtype=jnp.float32) / rot_dim)
    ang = positions.astype(jnp.float32)[..., None] * inv_freq
    return jnp.cos(ang)[:, None], jnp.sin(ang)[:, None]


def _apply_rope(x, cos, sin):
    half = cos.shape[-1]
    x1 = x[..., :half].astype(jnp.float32)
    x2 = x[..., half:2 * half].astype(jnp.float32)
    rotated = jnp.concatenate([x1 * cos - x2 * sin, x2 * cos + x1 * sin], axis=-1).astype(x.dtype)
    return jnp.concatenate([rotated, x[..., 2 * half:]], axis=-1)


def _split_heads(t, n_heads):
    b, s, _ = t.shape
    return t.reshape(b, s, n_heads, -1).transpose(0, 2, 1, 3)


def _merge_heads(t):
    b, h, s, d = t.shape
    return t.transpose(0, 2, 1, 3).reshape(b, s, h * d)


def _causal_mask(start, n_q, n_k):
    q_pos = start + jnp.arange(n_q)
    return q_pos[:, None] >= jnp.arange(n_k)[None, :]


def _sweep_query_blocks(block_fn, q_side):
    s = q_side[0].shape[2]
    nb = s // Q_BLOCK

    def split(a):
        a = a.reshape(a.shape[:2] + (nb, Q_BLOCK) + a.shape[3:])
        return jnp.moveaxis(a, 2, 0)

    starts = jnp.arange(nb, dtype=jnp.int32) * Q_BLOCK
    out = lax.map(lambda xs: block_fn(xs[0], *xs[1]), (starts, tuple(split(a) for a in q_side)))
    out = jnp.moveaxis(out, 0, 2)
    return out.reshape(out.shape[:2] + (nb * Q_BLOCK,) + out.shape[4:])


def _dense_causal_attention(q, k, v, scale):
    s = k.shape[2]

    def block(start, qb):
        logits = jnp.einsum('bhqd,bhkd->bhqk', qb, k).astype(jnp.float32) * scale
        logits = jnp.where(_causal_mask(start, Q_BLOCK, s), logits, -jnp.inf)
        p = jax.nn.softmax(logits, axis=-1)
        return jnp.einsum('bhqk,bhkd->bhqd', p.astype(v.dtype), v)

    return _sweep_query_blocks(block, (q,))


def _mla_mixer(h, rope_mla, w_in, q_a_g, kv_a_g, w_q_b, w_kv_b, q_g, k_g, w_out):
    b, s, _ = h.shape
    c_q, c_kv, k_pe = jnp.split(h @ w_in, [MLA_Q_LORA, MLA_Q_LORA + MLA_KV_LORA], axis=-1)
    q = _split_heads(_rms_norm(c_q, q_a_g) @ w_q_b, N_HEADS)
    kv = _split_heads(_rms_norm(c_kv, kv_a_g) @ w_kv_b, N_HEADS)
    k_nope, v = jnp.split(kv, [MLA_NOPE], axis=-1)
    k_pe = jnp.broadcast_to(k_pe[:, None], (b, N_HEADS, s, MLA_ROPE))
    k = jnp.concatenate([k_pe, k_nope], axis=-1)
    cos, sin = rope_mla
    q = _apply_rope(_rms_norm(q, q_g), cos, sin)
    k = _apply_rope(_rms_norm(k, k_g), cos, sin)
    o = _dense_causal_attention(q, k, v, (MLA_ROPE + MLA_NOPE) ** -0.5)
    return _merge_heads(o) @ w_out


def _dsa_mixer(h, rope_head, rope_idx, w_in, q_g, k_g, idx_k_g, w_out):
    b, s, _ = h.shape
    hd = N_HEADS * HEAD_DIM
    ih = IDX_HEADS * IDX_DIM
    q, k, v, iq, ik, iw = jnp.split(h @ w_in, [hd, 2 * hd, 3 * hd, 3 * hd + ih, 3 * hd + ih + IDX_DIM], axis=-1)
    cos, sin = rope_head
    q = _apply_rope(_rms_norm(_split_heads(q, N_HEADS), q_g), cos, sin)
    k = _apply_rope(_rms_norm(_split_heads(k, N_HEADS), k_g), cos, sin)
    v = _split_heads(v, N_HEADS)
    icos, isin = rope_idx
    iq = _apply_rope(_split_heads(iq, IDX_HEADS), icos, isin)
    ik = _apply_rope(_rms_norm(ik, idx_k_g)[:, None], icos, isin)[:, 0]
    iw = (iw.astype(jnp.float32) * IDX_HEADS ** -0.5).transpose(0, 2, 1)
    n_sel = min(IDX_TOPK, s // 4)
    scale = HEAD_DIM ** -0.5
    iscale = IDX_DIM ** -0.5
    take = jax.vmap(lambda t, idx: t[:, idx])

    def block(start, qb, iqb, iwb):
        q_pos = start + jnp.arange(Q_BLOCK)
        rel = jax.nn.relu(jnp.einsum('bhqd,bkd->bhqk', iqb, ik).astype(jnp.float32) * iscale)
        score = jnp.einsum('bhq,bhqk->bqk', iwb, rel)
        score = jnp.where(_causal_mask(start, Q_BLOCK, s)[None], score, -jnp.inf)
        _, idx = lax.top_k(score, n_sel)
        kg = take(k, idx)
        vg = take(v, idx)
        logits = jnp.einsum('bhqd,bhqnd->bhqn', qb, kg).astype(jnp.float32) * scale
        valid = idx <= q_pos[None, :, None]
        p = jax.nn.softmax(jnp.where(valid[:, None], logits, -jnp.inf), axis=-1)
        return jnp.einsum('bhqn,bhqnd->bhqd', p.astype(vg.dtype), vg)

    o = _sweep_query_blocks(block, (q, iq, iw))
    return _merge_heads(o) @ w_out


def _diff_mixer(h, rope_diff, layer_idx, w_in, q_g, k_g, lq1, lk1, lq2, lk2, subln_g, w_out):
    b, s, _ = h.shape
    w = DIFF_HEADS * 2 * DIFF_DIM
    q, k, v = jnp.split(h @ w_in, [w, 2 * w], axis=-1)
    cos, sin = rope_diff
    q = _apply_rope(_rms_norm(_split_heads(q, 2 * DIFF_HEADS), q_g), cos, sin)
    k = _apply_rope(_rms_norm(_split_heads(k, 2 * DIFF_HEADS), k_g), cos, sin)
    v = _split_heads(v, DIFF_HEADS)
    lam_init = 0.8 - 0.6 * math.exp(-0.3 * layer_idx)
    f32 = jnp.float32
    lam = (jnp.exp(jnp.sum(lq1.astype(f32) * lk1.astype(f32)))
           - jnp.exp(jnp.sum(lq2.astype(f32) * lk2.astype(f32))) + lam_init)
    scale = DIFF_DIM ** -0.5

    def block(start, qb):
        logits = jnp.einsum('bhqd,bhkd->bhqk', qb, k).astype(f32) * scale
        logits = jnp.where(_causal_mask(start, Q_BLOCK, s), logits, -jnp.inf)
        p = jax.nn.softmax(logits, axis=-1).reshape(b, DIFF_HEADS, 2, Q_BLOCK, s)
        a = p[:, :, 0] - lam * p[:, :, 1]
        return jnp.einsum('bhqk,bhkd->bhqd', a.astype(v.dtype), v)

    o = _sweep_query_blocks(block, (q,))
    o = _rms_norm(o, subln_g) * (1.0 - lam_init)
    return _merge_heads(o) @ w_out


def _fox_mixer(h, w_in, b_f, q_g, k_g, w_out):
    b, s, _ = h.shape
    hd = N_HEADS * HEAD_DIM
    q, k, v, f, g = jnp.split(h @ w_in, [hd, 2 * hd, 3 * hd, 3 * hd + N_HEADS], axis=-1)
    q = _rms_norm(_split_heads(q, N_HEADS), q_g)
    k = _rms_norm(_split_heads(k, N_HEADS), k_g)
    v = _split_heads(v, N_HEADS)
    log_f = jax.nn.log_sigmoid(f.astype(jnp.float32) + b_f.astype(jnp.float32))
    cum = lax.cumsum(log_f, axis=1).transpose(0, 2, 1)
    scale = HEAD_DIM ** -0.5

    def block(start, qb, cq):
        logits = (jnp.einsum('bhqd,bhkd->bhqk', qb, k).astype(jnp.float32) * scale
                  + cq[..., None] - cum[:, :, None, :])
        logits = jnp.where(_causal_mask(start, Q_BLOCK, s), logits, -jnp.inf)
        p = jax.nn.softmax(logits, axis=-1)
        return jnp.einsum('bhqk,bhkd->bhqd', p.astype(v.dtype), v)

    o = _sweep_query_blocks(block, (q, cum))
    o = _merge_heads(o) * jax.nn.sigmoid(g)
    return o @ w_out


def _swiglu(h, w_gate_up, w_down):
    gate, up = jnp.split(h @ w_gate_up, 2, axis=-1)
    return (jax.nn.silu(gate) * up) @ w_down


def setup_inputs(seed: int = 0) -> dict:
    key = jax.random.key(seed)
    ks = iter(jax.random.split(key, 48))

    def normal(shape, scale):
        return jax.random.normal(next(ks), shape, jnp.float32) * scale

    def gain(shape):
        return 1.0 + normal(shape, 0.02)

    n_a, n_b, n_c, n_d = [len(range(m, DEPTH, N_MIXERS)) for m in range(N_MIXERS)]
    d = D_MODEL
    hd = N_HEADS * HEAD_DIM
    x = normal((BATCH, SEQ, d), 1.0)
    c = normal((BATCH, d), 1.0)
    positions = (jax.random.randint(next(ks), (BATCH, 1), 0, POS_OFFSET_MAX, dtype=jnp.int32)
                 + jnp.arange(SEQ, dtype=jnp.int32)[None, :])
    mla_in = MLA_Q_LORA + MLA_KV_LORA + MLA_ROPE
    dsa_in = 3 * hd + IDX_HEADS * IDX_DIM + IDX_DIM + IDX_HEADS
    diff_w = DIFF_HEADS * 2 * DIFF_DIM
    fox_in = 3 * hd + N_HEADS + hd
    return {
        'x': x,
        'c': c,
        'positions': positions,
        'ln_mix_g': gain((DEPTH, d)),
        'ln_ffn_g': gain((DEPTH, d)),
        'ada_w': normal((DEPTH, d, 6 * d), 0.5 * d ** -0.5),
        'ada_b': normal((DEPTH, 6 * d), 0.02),
        'ffn_w_gate_up': normal((DEPTH, d, 2 * FFN_HIDDEN), d ** -0.5),
        'ffn_w_down': normal((DEPTH, FFN_HIDDEN, d), FFN_HIDDEN ** -0.5),
        'mla_w_in': normal((n_a, d, mla_in), d ** -0.5),
        'mla_q_a_g': gain((n_a, MLA_Q_LORA)),
        'mla_kv_a_g': gain((n_a, MLA_KV_LORA)),
        'mla_w_q_b': normal((n_a, MLA_Q_LORA, N_HEADS * (MLA_ROPE + MLA_NOPE)), MLA_Q_LORA ** -0.5),
        'mla_w_kv_b': normal((n_a, MLA_KV_LORA, N_HEADS * (MLA_NOPE + MLA_V)), MLA_KV_LORA ** -0.5),
        'mla_q_g': gain((n_a, MLA_ROPE + MLA_NOPE)),
        'mla_k_g': gain((n_a, MLA_ROPE + MLA_NOPE)),
        'mla_w_out': normal((n_a, N_HEADS * MLA_V, d), (N_HEADS * MLA_V) ** -0.5),
        'dsa_w_in': normal((n_b, d, dsa_in), d ** -0.5),
        'dsa_q_g': gain((n_b, HEAD_DIM)),
        'dsa_k_g': gain((n_b, HEAD_DIM)),
        'dsa_idx_k_g': gain((n_b, IDX_DIM)),
        'dsa_w_out': normal((n_b, hd, d), hd ** -0.5),
        'diff_w_in': normal((n_c, d, 3 * diff_w), d ** -0.5),
        'diff_q_g': gain((n_c, DIFF_DIM)),
        'diff_k_g': gain((n_c, DIFF_DIM)),
        'diff_lambda_q1': normal((n_c, DIFF_DIM), 0.1),
        'diff_lambda_k1': normal((n_c, DIFF_DIM), 0.1),
        'diff_lambda_q2': normal((n_c, DIFF_DIM), 0.1),
        'diff_lambda_k2': normal((n_c, DIFF_DIM), 0.1),
        'diff_subln_g': gain((n_c, 2 * DIFF_DIM)),
        'diff_w_out': normal((n_c, diff_w, d), diff_w ** -0.5),
        'fox_w_in': normal((n_d, d, fox_in), d ** -0.5),
        'fox_b_f': jax.random.uniform(next(ks), (n_d, N_HEADS), jnp.float32, 1.0, 6.0),
        'fox_q_g': gain((n_d, HEAD_DIM)),
        'fox_k_g': gain((n_d, HEAD_DIM)),
        'fox_w_out': normal((n_d, hd, d), hd ** -0.5),
    }


def reference(x, c, positions, ln_mix_g, ln_ffn_g, ada_w, ada_b, ffn_w_gate_up, ffn_w_down,
              mla_w_in, mla_q_a_g, mla_kv_a_g, mla_w_q_b, mla_w_kv_b, mla_q_g, mla_k_g, mla_w_out,
              dsa_w_in, dsa_q_g, dsa_k_g, dsa_idx_k_g, dsa_w_out,
              diff_w_in, diff_q_g, diff_k_g, diff_lambda_q1, diff_lambda_k1, diff_lambda_q2,
              diff_lambda_k2, diff_subln_g, diff_w_out,
              fox_w_in, fox_b_f, fox_q_g, fox_k_g, fox_w_out):
    rope_head = _rope_cos_sin(positions, HEAD_DIM // ROPE_FRACTION)
    rope_idx = _rope_cos_sin(positions, IDX_DIM // ROPE_FRACTION)
    rope_diff = _rope_cos_sin(positions, DIFF_DIM // ROPE_FRACTION)
    rope_mla = _rope_cos_sin(positions, MLA_ROPE)
    cond = jax.nn.silu(c)
    for i in range(DEPTH):
        mod = (cond @ ada_w[i] + ada_b[i])[:, None, :]
        sh1, sc1, g1, sh2, sc2, g2 = jnp.split(mod, 6, axis=-1)
        h = _rms_norm(x, ln_mix_g[i]) * (1.0 + sc1) + sh1
        kind, j = i % N_MIXERS, i // N_MIXERS
        if kind == 0:
            y = _mla_mixer(h, rope_mla, mla_w_in[j], mla_q_a_g[j], mla_kv_a_g[j], mla_w_q_b[j],
                           mla_w_kv_b[j], mla_q_g[j], mla_k_g[j], mla_w_out[j])
        elif kind == 1:
            y = _dsa_mixer(h, rope_head, rope_idx, dsa_w_in[j], dsa_q_g[j], dsa_k_g[j],
                           dsa_idx_k_g[j], dsa_w_out[j])
        elif kind == 2:
            y = _diff_mixer(h, rope_diff, i, diff_w_in[j], diff_q_g[j], diff_k_g[j],
                            diff_lambda_q1[j], diff_lambda_k1[j], diff_lambda_q2[j],
                            diff_lambda_k2[j], diff_subln_g[j], diff_w_out[j])
        else:
            y = _fox_mixer(h, fox_w_in[j], fox_b_f[j], fox_q_g[j], fox_k_g[j], fox_w_out[j])
        x = x + g1 * y
        h = _rms_norm(x, ln_ffn_g[i]) * (1.0 + sc2) + sh2
        x = x + g2 * _swiglu(h, ffn_w_gate_up[i], ffn_w_down[i])
    return x
```

```python
import functools
import math

import jax
import jax.numpy as jnp
from jax import lax
from jax.experimental import pallas as pl
from jax.experimental.pallas import tpu as pltpu

F32 = jnp.float32
BF16 = jnp.bfloat16

D_MODEL = 2048
N_HEADS = 16
HEAD_DIM = 128
ROPE_THETA = 500000.0
RMS_EPS = 1e-6
FFN_HIDDEN = 5632
MLA_LORA = 512
MLA_ROPE = 64
MLA_HEAD = 192
IDX_HEADS = 16
IDX_DIM = 64
IDX_TOPK = 256
DIFF_DIM = 64

LANES = 128
LOG2E = 1.4426950408889634
NEG = -1e30
VMEM_LIMIT_BYTES = 56 * 1024 * 1024
INT_MIN = -2 ** 31
KEY_NEG_INF = (0xFF800000 ^ 0x7FFFFFFF) - 2 ** 32


def _cparams(*sem):
    return pltpu.CompilerParams(dimension_semantics=sem, vmem_limit_bytes=VMEM_LIMIT_BYTES)


def _dot_nt(a, b):
    return lax.dot_general(a, b, (((1,), (1,)), ((), ())), preferred_element_type=F32)


def _sigmoid(z):
    return 1.0 / (1.0 + jnp.exp(-z))


def _mm_body(*refs, n_w, n_aux, n_out, epi):
    a_ref = refs[0]
    w_refs = refs[1:1 + n_w]
    aux_refs = refs[1 + n_w:1 + n_w + n_aux]
    out_refs = refs[1 + n_w + n_aux:1 + n_w + n_aux + n_out]
    wb_refs = refs[1 + n_w + n_aux + n_out:]

    @pl.when(pl.program_id(1) == 0)
    def _cast_weights():
        for w_ref, wb_ref in zip(w_refs, wb_refs):
            wb_ref[...] = w_ref[...].astype(BF16)

    a = a_ref[...]
    accs = [jnp.dot(a, wb_ref[...], preferred_element_type=F32) for wb_ref in wb_refs]
    epi(accs, aux_refs, out_refs)


def _mm(a, ws, aux, outs, epi, *, tm, tn, n_tiles, a_blk=0):
    s = a.shape[0]
    k = ws[0][0].shape[1]
    tm = min(tm, s)
    in_specs = [pl.BlockSpec((tm, k), lambda n, m: (m, a_blk))]
    operands = [a]
    for w, layer, off in ws:
        assert w.shape[1] == k
        in_specs.append(pl.BlockSpec((None, k, tn), lambda n, m, layer=layer, off=off: (layer, 0, n + off)))
        operands.append(w)
    for arr, block, imap in aux:
        in_specs.append(pl.BlockSpec(block, imap))
        operands.append(arr)
    out_shape = [jax.ShapeDtypeStruct(shape, dtype) for shape, dtype, _, _ in outs]
    out_specs = [pl.BlockSpec(block, imap) for _, _, block, imap in outs]
    body = functools.partial(_mm_body, n_w=len(ws), n_aux=len(aux), n_out=len(outs), epi=epi)
    res = pl.pallas_call(
        body,
        grid=(n_tiles, s // tm),
        in_specs=in_specs,
        out_specs=out_specs,
        out_shape=out_shape,
        scratch_shapes=[pltpu.VMEM((k, tn), BF16) for _ in ws],
        compiler_params=_cparams("arbitrary", "arbitrary"),
    )(*operands)
    return res


def _rope_group(y, tabs, half):
    c, s1, s2 = tabs
    return y * c + pltpu.roll(y, half, 1) * s1 + pltpu.roll(y, LANES - half, 1) * s2


def _group_inv_rms(y, gw, real):
    tn = y.shape[1]
    sq = y * y
    n_groups = tn // LANES
    if gw == DIFF_DIM:
        lane = lax.broadcasted_iota(jnp.int32, (1, LANES), 1)
        lo = lane < DIFF_DIM
        out = []
        for g in range(n_groups):
            sg = sq[:, g * LANES:(g + 1) * LANES]
            s_lo = jnp.sum(jnp.where(lo, sg, 0.0), axis=1, keepdims=True)
            s_hi = jnp.sum(jnp.where(lo, 0.0, sg), axis=1, keepdims=True)
            out.append(lax.rsqrt(jnp.where(lo, s_lo, s_hi) * (1.0 / real) + RMS_EPS))
        return out
    sums = [jnp.sum(sq[:, g * LANES:(g + 1) * LANES], axis=1, keepdims=True) for g in range(n_groups)]
    per = gw // LANES
    out = []
    for h in range(n_groups // per):
        tot = sums[h * per]
        for t in range(1, per):
            tot = tot + sums[h * per + t]
        r = lax.rsqrt(tot * (1.0 / real) + RMS_EPS)
        out.extend([r] * per)
    return out


def _epi_heads(accs, aux_refs, out_refs, *, gw, real, half, rope_groups, scale, use_gain, use_rope):
    y = accs[0]
    tn = y.shape[1]
    idx = 0
    gain = None
    if use_gain:
        gain = aux_refs[idx][...]
        idx += 1
        inv = _group_inv_rms(y, gw, real)
    if use_rope:
        tabs = tuple(aux_refs[idx + t][...] for t in range(3))
    for g in range(tn // LANES):
        yg = y[:, g * LANES:(g + 1) * LANES]
        if use_gain:
            yg = yg * inv[g] * gain[:, g * LANES:(g + 1) * LANES]
        if use_rope and rope_groups(g):
            yg = _rope_group(yg, tabs, half)
        if scale != 1.0:
            yg = yg * scale
        out_refs[0][:, g * LANES:(g + 1) * LANES] = yg.astype(out_refs[0].dtype)


def _epi_plain(accs, aux_refs, out_refs):
    out_refs[0][...] = accs[0].astype(out_refs[0].dtype)


def _epi_sigmoid(accs, aux_refs, out_refs):
    out_refs[0][...] = _sigmoid(accs[0]).astype(out_refs[0].dtype)


def _epi_residual(accs, aux_refs, out_refs):
    x_ref, g_ref = aux_refs
    out_refs[0][...] = x_ref[...] + g_ref[...] * accs[0]


def _epi_swiglu(accs, aux_refs, out_refs):
    gate, up = accs
    out_refs[0][...] = (gate * _sigmoid(gate) * up).astype(out_refs[0].dtype)


def _epi_mla_kv(accs, aux_refs, out_refs):
    kpe_ref, g_nope_ref, g_pe_ref, c_ref, s1_ref, s2_ref = aux_refs
    k_ref, v_ref = out_refs
    y = accs[0]
    kpe = kpe_ref[...]
    ss_pe = jnp.sum(kpe * kpe, axis=1, keepdims=True)
    tabs = (c_ref[...], s1_ref[...], s2_ref[...])
    g_nope = g_nope_ref[...]
    g_pe = g_pe_ref[...]
    for h in range(y.shape[1] // (2 * LANES)):
        kn = y[:, 2 * h * LANES:(2 * h + 1) * LANES]
        v = y[:, (2 * h + 1) * LANES:(2 * h + 2) * LANES]
        r = lax.rsqrt((jnp.sum(kn * kn, axis=1, keepdims=True) + ss_pe) * (1.0 / MLA_HEAD) + RMS_EPS)
        k_ref[:, 2 * h * LANES:(2 * h + 1) * LANES] = (kn * r * g_nope).astype(k_ref.dtype)
        pe = _rope_group(kpe * r * g_pe, tabs, MLA_ROPE // 2)
        k_ref[:, (2 * h + 1) * LANES:(2 * h + 2) * LANES] = pe.astype(k_ref.dtype)
        v_ref[:, h * LANES:(h + 1) * LANES] = v.astype(v_ref.dtype)


def _row_tab_aux(tabs, tm):
    return [(t, (tm, LANES), lambda n, m: (m, 0)) for t in tabs]


def _proj(a, w, layer, col_off, width, *, tn, out_dtype=BF16, gain=None, tabs=None, gw=LANES, real=LANES,
          half=0, rope_groups=lambda g: True, scale=1.0, epi=None, tm=512, a_blk=0):
    s = a.shape[0]
    tm = min(tm, s)
    assert col_off % tn == 0 and width % tn == 0
    aux = []
    if gain is not None:
        aux.append((gain.reshape(1, width).astype(F32), (1, tn), lambda n, m: (0, n)))
    if tabs is not None:
        aux.extend(_row_tab_aux(tabs, tm))
    if epi is None:
        epi = functools.partial(_epi_heads, gw=gw, real=real, half=half, rope_groups=rope_groups, scale=scale,
                                use_gain=gain is not None, use_rope=tabs is not None)
    outs = [((s, width), out_dtype, (tm, tn), lambda n, m: (m, n))]
    return _mm(a, [(w, layer, col_off // tn)], aux, outs, epi, tm=tm, tn=tn, n_tiles=width // tn, a_blk=a_blk)[0]


def _residual_mm(a, w, layer, x, gate, *, tn, tm=512):
    s, d = x.shape
    tm = min(tm, s)
    aux = [(x, (tm, tn), lambda n, m: (m, n)), (gate.reshape(1, d), (1, tn), lambda n, m: (0, n))]
    outs = [((s, d), F32, (tm, tn), lambda n, m: (m, n))]
    return _mm(a, [(w, layer, 0)], aux, outs, _epi_residual, tm=tm, tn=tn, n_tiles=d // tn)[0]


def _ada_body(c_ref, w_ref, b_ref, o_ref):
    c = c_ref[...]
    cond = c * _sigmoid(c)
    o_ref[...] = jnp.dot(cond.astype(BF16), w_ref[...].astype(BF16), preferred_element_type=F32) + b_ref[...]


def _ada_mod(c, ada_w, ada_b):
    depth, d, n = ada_w.shape
    tn = 1536
    c8 = jnp.broadcast_to(c, (8, d))
    out = pl.pallas_call(
        _ada_body,
        grid=(depth, n // tn),
        in_specs=[pl.BlockSpec((8, d), lambda l, j: (0, 0)),
                  pl.BlockSpec((None, d, tn), lambda l, j: (l, 0, j)),
                  pl.BlockSpec((None, 1, tn), lambda l, j: (l, 0, j))],
        out_specs=pl.BlockSpec((None, 8, tn), lambda l, j: (l, 0, j)),
        out_shape=jax.ShapeDtypeStruct((depth, 8, n), F32),
        compiler_params=_cparams("arbitrary", "arbitrary"),
    )(c8, ada_w, ada_b.reshape(depth, 1, n))
    return out[:, 0, :]


def _normmod_body(x_ref, g_ref, sc_ref, sh_ref, o_ref):
    x = x_ref[...]
    y = x * lax.rsqrt(jnp.mean(x * x, axis=1, keepdims=True) + RMS_EPS) * g_ref[...]
    o_ref[...] = (y * (1.0 + sc_ref[...]) + sh_ref[...]).astype(o_ref.dtype)


def _normmod(x, g, sc, sh, tm=256):
    s, d = x.shape
    tm = min(tm, s)
    vec = pl.BlockSpec((1, d), lambda m: (0, 0))
    return pl.pallas_call(
        _normmod_body,
        grid=(s // tm,),
        in_specs=[pl.BlockSpec((tm, d), lambda m: (m, 0)), vec, vec, vec],
        out_specs=pl.BlockSpec((tm, d), lambda m: (m, 0)),
        out_shape=jax.ShapeDtypeStruct((s, d), BF16),
        compiler_params=_cparams("arbitrary"),
    )(x, g.reshape(1, d), sc.reshape(1, d), sh.reshape(1, d))


def _online_update(s, v, m_ref, l_ref, acc_ref):
    m_prev = m_ref[...]
    m_new = jnp.maximum(m_prev, jnp.max(s, axis=1, keepdims=True))
    alpha = jnp.exp2(m_prev - m_new)
    p = jnp.exp2(s - m_new)
    l_ref[...] = alpha * l_ref[...] + jnp.sum(p, axis=1, keepdims=True)
    acc_ref[...] = alpha * acc_ref[...] + jnp.dot(p.astype(BF16), v, preferred_element_type=F32)
    m_ref[...] = m_new


def _init_state(m_ref, l_ref, acc_ref):
    m_ref[...] = jnp.full(m_ref.shape, NEG, F32)
    l_ref[...] = jnp.zeros(l_ref.shape, F32)
    acc_ref[...] = jnp.zeros(acc_ref.shape, F32)


def _causal_keep(tq, tk, col_shift):
    row = lax.broadcasted_iota(jnp.int32, (tq, 1), 0)
    col = lax.broadcasted_iota(jnp.int32, (1, tk), 1)
    return row >= col + col_shift


def _dense_attn_body(*refs, mode, tq, tk, lam_init):
    if mode == "mla":
        q_ref, k_ref, v_ref, o_ref = refs[:4]
        scratch = refs[4:]
    elif mode == "fox":
        q_ref, k_ref, v_ref, cum_ref, cumt_ref, gate_ref, o_ref = refs[:7]
        scratch = refs[7:]
    else:
        q_ref, k_ref, v_ref, lam_ref, subg_ref, o_ref = refs[:6]
        scratch = refs[6:]
    h = pl.program_id(0)
    i = pl.program_id(1)
    n_maps = 2 if mode == "diff" else 1
    states = [scratch[3 * t:3 * t + 3] for t in range(n_maps)]
    for st in states:
        _init_state(*st)

    q = q_ref[...]
    if mode == "diff":
        lane = lax.broadcasted_iota(jnp.int32, (1, LANES), 1)
        zero = jnp.zeros_like(q)
        qs = [jnp.where(lane < DIFF_DIM, q, zero), jnp.where(lane >= DIFF_DIM, q, zero)]
    else:
        qs = [q]
    if mode == "fox":
        lane = lax.broadcasted_iota(jnp.int32, (1, LANES), 1)
        cq = jnp.sum(jnp.where(lane == h, cum_ref[...], 0.0), axis=1, keepdims=True) * LOG2E

    def step(start, keep):
        k = k_ref[pl.ds(start, tk), :]
        v = v_ref[pl.ds(start, tk), :]
        for qm, st in zip(qs, states):
            s = _dot_nt(qm, k)
            if mode == "fox":
                s = s + (cq - cumt_ref[:, pl.ds(start, tk)] * LOG2E)
            if keep is not None:
                s = jnp.where(keep, s, NEG)
            _online_update(s, v, *st)

    def full_body(j, carry):
        step(pl.multiple_of(j * tk, tk), None)
        return carry

    per = tq // tk
    lax.fori_loop(0, i * per, full_body, 0)
    for c in range(per):
        step(pl.multiple_of(i * tq + c * tk, tk), _causal_keep(tq, tk, c * tk))

    if mode == "diff":
        lp = lam_ref[...]
        lam = (jnp.exp(jnp.sum(lp[0:1] * lp[1:2], axis=1, keepdims=True))
               - jnp.exp(jnp.sum(lp[2:3] * lp[3:4], axis=1, keepdims=True)) + lam_init)
        (_, l0, a0), (_, l1, a1) = states
        o = a0[...] / l0[...] - lam * (a1[...] / l1[...])
        o = o * lax.rsqrt(jnp.mean(o * o, axis=1, keepdims=True) + RMS_EPS) * subg_ref[...]
        o = o * (1.0 - lam_init)
    else:
        _, l_ref, acc_ref = states[0]
        o = acc_ref[...] / l_ref[...]
        if mode == "fox":
            o = o * gate_ref[...]
    o_ref[...] = o.astype(o_ref.dtype)


def _dense_attention(q, k, v, *, mode, dq, extra=(), lam_init=0.0, tq=512, tk=512):
    s = q.shape[0]
    tq = min(tq, s)
    tk = min(tk, tq)
    in_specs = [pl.BlockSpec((tq, dq), lambda h, i: (i, h)),
                pl.BlockSpec((s, dq), lambda h, i: (0, h)),
                pl.BlockSpec((s, HEAD_DIM), lambda h, i: (0, h))]
    operands = [q, k, v]
    if mode == "fox":
        cum, cumt, gate = extra
        in_specs += [pl.BlockSpec((tq, LANES), lambda h, i: (i, 0)),
                     pl.BlockSpec((None, 1, s), lambda h, i: (h, 0, 0)),
                     pl.BlockSpec((tq, HEAD_DIM), lambda h, i: (i, h))]
        operands += [cum, cumt, gate]
    elif mode == "diff":
        lam_params, subln_g = extra
        in_specs += [pl.BlockSpec((4, DIFF_DIM), lambda h, i: (0, 0)),
                     pl.BlockSpec((1, HEAD_DIM), lambda h, i: (0, 0))]
        operands += [lam_params, subln_g]
    n_maps = 2 if mode == "diff" else 1
    scratch = []
    for _ in range(n_maps):
        scratch += [pltpu.VMEM((tq, 1), F32), pltpu.VMEM((tq, 1), F32), pltpu.VMEM((tq, HEAD_DIM), F32)]
    body = functools.partial(_dense_attn_body, mode=mode, tq=tq, tk=tk, lam_init=lam_init)
    return pl.pallas_call(
        body,
        grid=(N_HEADS, s // tq),
        in_specs=in_specs,
        out_specs=pl.BlockSpec((tq, HEAD_DIM), lambda h, i: (i, h)),
        out_shape=jax.ShapeDtypeStruct((s, N_HEADS * HEAD_DIM), BF16),
        scratch_shapes=scratch,
        compiler_params=_cparams("arbitrary", "arbitrary"),
    )(*operands)


def _dsa_attn_body(q_ref, k_ref, v_ref, bias_ref, o_ref, m_ref, l_ref, acc_ref, *, tq, tk):
    i = pl.program_id(0)
    _init_state(m_ref, l_ref, acc_ref)
    q = q_ref[...]

    def body(j, carry):
        start = pl.multiple_of(j * tk, tk)
        s = _dot_nt(q, k_ref[pl.ds(start, tk), :]) + bias_ref[:, pl.ds(start, tk)].astype(F32)
        _online_update(s, v_ref[pl.ds(start, tk), :], m_ref, l_ref, acc_ref)
        return carry

    lax.fori_loop(0, (i + 1) * (tq // tk), body, 0)
    o_ref[...] = (acc_ref[...] / l_ref[...]).astype(o_ref.dtype)


def _dsa_attention(q, k, v, bias, *, tq=512, tk=512):
    s = q.shape[0]
    tq = min(tq, s)
    tk = min(tk, tq)
    body = functools.partial(_dsa_attn_body, tq=tq, tk=tk)
    return pl.pallas_call(
        body,
        grid=(s // tq, N_HEADS),
        in_specs=[pl.BlockSpec((tq, HEAD_DIM), lambda i, h: (i, h)),
                  pl.BlockSpec((s, HEAD_DIM), lambda i, h: (0, h)),
                  pl.BlockSpec((s, HEAD_DIM), lambda i, h: (0, h)),
                  pl.BlockSpec((tq, s), lambda i, h: (i, 0))],
        out_specs=pl.BlockSpec((tq, HEAD_DIM), lambda i, h: (i, h)),
        out_shape=jax.ShapeDtypeStruct((s, N_HEADS * HEAD_DIM), BF16),
        scratch_shapes=[pltpu.VMEM((tq, 1), F32), pltpu.VMEM((tq, 1), F32), pltpu.VMEM((tq, HEAD_DIM), F32)],
        compiler_params=_cparams("arbitrary", "arbitrary"),
    )(q, k, v, bias)


def _sortable_key(score):
    bits = lax.bitcast_convert_type(score, jnp.int32)
    return bits ^ ((bits >> 31) & 0x7FFFFFFF)


def _indexer_body(iq_ref, ik_ref, iw_ref, bias_ref, keys_ref, qz_ref, wb_ref, *, tq, n_sel):
    i = pl.program_id(0)
    s_len = ik_ref.shape[0]
    n_chunks = s_len // tq
    lane = lax.broadcasted_iota(jnp.int32, (1, LANES), 1)
    iw = iw_ref[...]
    for p in range(IDX_HEADS // 2):
        pair = iq_ref[:, p * LANES:(p + 1) * LANES]
        zero = jnp.zeros_like(pair)
        qz_ref[2 * p] = jnp.where(lane < IDX_DIM, pair, zero)
        qz_ref[2 * p + 1] = jnp.where(lane >= IDX_DIM, pair, zero)
    for hh in range(IDX_HEADS):
        wb_ref[hh] = jnp.broadcast_to(iw[:, hh:hh + 1], (tq, LANES))

    def score_chunk(start):
        ik = ik_ref[pl.ds(start, tq), :]
        sc = jnp.zeros((tq, tq), F32)
        for hh in range(IDX_HEADS):
            rel = jnp.maximum(_dot_nt(qz_ref[hh], ik), 0.0)
            sc = sc + jnp.tile(wb_ref[hh], (1, tq // LANES)) * rel
        return sc

    def full_body(j, carry):
        start = pl.multiple_of(j * tq, tq)
        keys_ref[:, pl.ds(start, tq)] = _sortable_key(score_chunk(start))
        return carry

    lax.fori_loop(0, i, full_body, 0)
    diag = pl.multiple_of(i * tq, tq)
    sc = jnp.where(_causal_keep(tq, tq, 0), score_chunk(diag), -jnp.inf)
    keys_ref[:, pl.ds(diag, tq)] = _sortable_key(sc)

    def bit_body(b, cand):
        trial = cand | lax.shift_left(jnp.int32(1), 31 - b)
        trial_signed = trial ^ INT_MIN

        def count_body(j, cnt):
            kk = keys_ref[:, pl.ds(pl.multiple_of(j * tq, tq), tq)]
            ge = (kk >= trial_signed).astype(jnp.int32)
            for g in range(tq // LANES):
                cnt = cnt + ge[:, g * LANES:(g + 1) * LANES]
            return cnt

        cnt = lax.fori_loop(0, i + 1, count_body, jnp.zeros((tq, LANES), jnp.int32))
        total = jnp.sum(cnt, axis=1, keepdims=True)
        return jnp.where(total >= n_sel, trial, cand)

    cand = lax.fori_loop(0, 32, bit_body, jnp.zeros((tq, 1), jnp.int32))
    thr = jnp.maximum(cand ^ INT_MIN, KEY_NEG_INF + 1)

    def write_body(j, carry):
        start = pl.multiple_of(j * tq, tq)
        kk = keys_ref[:, pl.ds(start, tq)]
        bias_ref[:, pl.ds(start, tq)] = jnp.where(kk >= thr, 0.0, NEG).astype(bias_ref.dtype)
        return carry

    lax.fori_loop(0, i + 1, write_body, 0)

    def fill_body(j, carry):
        bias_ref[:, pl.ds(pl.multiple_of(j * tq, tq), tq)] = jnp.full((tq, tq), NEG, bias_ref.dtype)
        return carry

    lax.fori_loop(i + 1, n_chunks, fill_body, 0)


def _dsa_select_bias(iq, ik2, iw, n_sel, tq=256):
    s = iq.shape[0]
    tq = min(tq, s)
    body = functools.partial(_indexer_body, tq=tq, n_sel=n_sel)
    return pl.pallas_call(
        body,
        grid=(s // tq,),
        in_specs=[pl.BlockSpec((tq, IDX_HEADS * IDX_DIM), lambda i: (i, 0)),
                  pl.BlockSpec((s, LANES), lambda i: (0, 0)),
                  pl.BlockSpec((tq, LANES), lambda i: (i, 0))],
        out_specs=pl.BlockSpec((tq, s), lambda i: (i, 0)),
        out_shape=jax.ShapeDtypeStruct((s, s), BF16),
        scratch_shapes=[pltpu.VMEM((tq, s), jnp.int32),
                        pltpu.VMEM((IDX_HEADS, tq, LANES), BF16),
                        pltpu.VMEM((IDX_HEADS, tq, LANES), F32)],
        compiler_params=_cparams("arbitrary"),
    )(iq, ik2, iw)


def _fox_cum_body(f_ref, b_ref, o_ref, carry_ref, *, tc):
    @pl.when(pl.program_id(0) == 0)
    def _():
        carry_ref[...] = jnp.zeros(carry_ref.shape, F32)

    z = f_ref[...] + b_ref[...]
    lf = jnp.minimum(z, 0.0) - jnp.log(1.0 + jnp.exp(-jnp.abs(z)))
    row = lax.broadcasted_iota(jnp.int32, (tc, tc), 0)
    col = lax.broadcasted_iota(jnp.int32, (tc, tc), 1)
    tri = (row >= col).astype(BF16)
    hi = lf.astype(BF16)
    r1 = lf - hi.astype(F32)
    mid = r1.astype(BF16)
    lo = (r1 - mid.astype(F32)).astype(BF16)
    cs = (jnp.dot(tri, hi, preferred_element_type=F32) + jnp.dot(tri, mid, preferred_element_type=F32)
          + jnp.dot(tri, lo, preferred_element_type=F32)) + carry_ref[...]
    o_ref[...] = cs
    carry_ref[...] = cs[tc - 1:tc, :]


def _fox_cumsum(f_raw, b_pad, tc=512):
    s = f_raw.shape[0]
    tc = min(tc, s)
    return pl.pallas_call(
        functools.partial(_fox_cum_body, tc=tc),
        grid=(s // tc,),
        in_specs=[pl.BlockSpec((tc, LANES), lambda m: (m, 0)), pl.BlockSpec((1, LANES), lambda m: (0, 0))],
        out_specs=pl.BlockSpec((tc, LANES), lambda m: (m, 0)),
        out_shape=jax.ShapeDtypeStruct((s, LANES), F32),
        scratch_shapes=[pltpu.VMEM((1, LANES), F32)],
        compiler_params=_cparams("arbitrary"),
    )(f_raw, b_pad)


def _rope_tables(positions, rot_dim, period):
    half = rot_dim // 2
    inv_freq = ROPE_THETA ** (-jnp.arange(0, rot_dim, 2, dtype=F32) / rot_dim)
    ang = positions.astype(F32)[:, None] * inv_freq
    cos, sin = jnp.cos(ang), jnp.sin(ang)
    lp = jnp.arange(LANES) % period
    in_x1 = lp < half
    in_x2 = (lp >= half) & (lp < 2 * half)
    idx = jnp.where(in_x1, lp, jnp.clip(lp - half, 0, half - 1))
    cos_l, sin_l = cos[:, idx], sin[:, idx]
    c = jnp.where(in_x1 | in_x2, cos_l, 1.0)
    s1 = jnp.where(in_x2, sin_l, 0.0)
    s2 = jnp.where(in_x1, -sin_l, 0.0)
    return c, s1, s2


def _pad_cols(w, width):
    return jnp.pad(w, ((0, 0), (0, width - w.shape[1])))


def _mla_mixer(h, tabs, w_in, q_a_g, kv_a_g, w_q_b, w_kv_b, q_g, k_g):
    s = h.shape[0]
    nope = HEAD_DIM
    scale = MLA_HEAD ** -0.5 * LOG2E
    gains = jnp.concatenate([q_a_g, kv_a_g])
    c_norm = _proj(h, w_in, 0, 0, 2 * MLA_LORA, tn=MLA_LORA, gain=gains, gw=MLA_LORA, real=MLA_LORA)
    kpe = _proj(h, _pad_cols(w_in[0][:, 2 * MLA_LORA:], LANES)[None], 0, 0, LANES, tn=LANES, out_dtype=F32,
                epi=_epi_plain)
    wq = w_q_b[0].reshape(MLA_LORA, N_HEADS, MLA_HEAD)
    wq = jnp.concatenate([wq[:, :, MLA_ROPE:], wq[:, :, :MLA_ROPE],
                          jnp.zeros((MLA_LORA, N_HEADS, 2 * LANES - MLA_HEAD), F32)], axis=2)
    wq = wq.reshape(1, MLA_LORA, N_HEADS * 2 * LANES)
    qg = jnp.concatenate([q_g[MLA_ROPE:], q_g[:MLA_ROPE], jnp.zeros((2 * LANES - MLA_HEAD,), F32)])
    q = _proj(c_norm, wq, 0, 0, N_HEADS * 2 * LANES, tn=512, gain=jnp.tile(qg, N_HEADS), tabs=tabs,
              gw=2 * LANES, real=MLA_HEAD, half=MLA_ROPE // 2, rope_groups=lambda g: g % 2 == 1, scale=scale)
    tm, tn = min(512, s), 512
    aux = [(kpe, (tm, LANES), lambda n, m: (m, 0)),
           (k_g[MLA_ROPE:].reshape(1, nope), (1, nope), lambda n, m: (0, 0)),
           (jnp.pad(k_g[:MLA_ROPE], (0, LANES - MLA_ROPE)).reshape(1, LANES), (1, LANES), lambda n, m: (0, 0))]
    aux += _row_tab_aux(tabs, tm)
    outs = [((s, N_HEADS * 2 * LANES), BF16, (tm, tn), lambda n, m: (m, n)),
            ((s, N_HEADS * HEAD_DIM), BF16, (tm, tn // 2), lambda n, m: (m, n))]
    k, v = _mm(c_norm, [(w_kv_b, 0, 0)], aux, outs, _epi_mla_kv, tm=tm, tn=tn,
               n_tiles=N_HEADS * 2 * LANES // tn, a_blk=1)
    return _dense_attention(q, k, v, mode="mla", dq=2 * LANES)


def _dsa_mixer(h, tabs_head, tabs_idx, w_in, q_g, k_g, idx_k_g):
    s = h.shape[0]
    hd = N_HEADS * HEAD_DIM
    ih = IDX_HEADS * IDX_DIM
    half = HEAD_DIM // 8
    q = _proj(h, w_in, 0, 0, hd, tn=512, gain=jnp.tile(q_g, N_HEADS), tabs=tabs_head, half=half,
              scale=HEAD_DIM ** -0.5 * LOG2E)
    k = _proj(h, w_in, 0, hd, hd, tn=512, gain=jnp.tile(k_g, N_HEADS), tabs=tabs_head, half=half)
    v = _proj(h, w_in, 0, 2 * hd, hd, tn=512, epi=_epi_plain)
    iq = _proj(h, w_in, 0, 3 * hd, ih, tn=512, tabs=tabs_idx, half=IDX_DIM // 8)
    w_ik = w_in[0][:, 3 * hd + ih:3 * hd + ih + IDX_DIM]
    ik2 = _proj(h, jnp.concatenate([w_ik, w_ik], axis=1)[None], 0, 0, LANES, tn=LANES,
                gain=jnp.tile(idx_k_g, 2), tabs=tabs_idx, half=IDX_DIM // 8)
    w_iw = _pad_cols(w_in[0][:, 3 * hd + ih + IDX_DIM:], LANES)[None]
    iw = _proj(h, w_iw, 0, 0, LANES, tn=LANES, out_dtype=F32,
               epi=functools.partial(_epi_heads, gw=LANES, real=LANES, half=0, rope_groups=None,
                                     scale=IDX_HEADS ** -0.5 * IDX_DIM ** -0.5, use_gain=False, use_rope=False))
    bias = _dsa_select_bias(iq, ik2, iw, min(IDX_TOPK, s // 4))
    return _dsa_attention(q, k, v, bias)


def _diff_mixer(h, tabs, layer_idx, w_in, q_g, k_g, lam_params, subln_g):
    w = N_HEADS * 2 * DIFF_DIM
    half = DIFF_DIM // 8
    q = _proj(h, w_in, 0, 0, w, tn=512, gain=jnp.tile(q_g, 2 * N_HEADS), tabs=tabs, gw=DIFF_DIM, real=DIFF_DIM,
              half=half, scale=DIFF_DIM ** -0.5 * LOG2E)
    k = _proj(h, w_in, 0, w, w, tn=512, gain=jnp.tile(k_g, 2 * N_HEADS), tabs=tabs, gw=DIFF_DIM, real=DIFF_DIM,
              half=half)
    v = _proj(h, w_in, 0, 2 * w, w, tn=512, epi=_epi_plain)
    lam_init = 0.8 - 0.6 * math.exp(-0.3 * layer_idx)
    return _dense_attention(q, k, v, mode="diff", dq=HEAD_DIM, extra=(lam_params, subln_g.reshape(1, HEAD_DIM)),
                            lam_init=lam_init)


def _fox_mixer(h, w_in, b_f, q_g, k_g):
    s = h.shape[0]
    hd = N_HEADS * HEAD_DIM
    q = _proj(h, w_in, 0, 0, hd, tn=512, gain=jnp.tile(q_g, N_HEADS), scale=HEAD_DIM ** -0.5 * LOG2E)
    k = _proj(h, w_in, 0, hd, hd, tn=512, gain=jnp.tile(k_g, N_HEADS))
    v = _proj(h, w_in, 0, 2 * hd, hd, tn=512, epi=_epi_plain)
    w_f = _pad_cols(w_in[0][:, 3 * hd:3 * hd + N_HEADS], LANES)[None]
    f_raw = _proj(h, w_f, 0, 0, LANES, tn=LANES, out_dtype=F32, epi=_epi_plain)
    gate = _proj(h, w_in[:, :, 3 * hd + N_HEADS:], 0, 0, hd, tn=512, out_dtype=F32, epi=_epi_sigmoid)
    cum = _fox_cumsum(f_raw, jnp.pad(b_f, (0, LANES - N_HEADS)).reshape(1, LANES))
    cumt = cum[:, :N_HEADS].T.reshape(N_HEADS, 1, s)
    return _dense_attention(q, k, v, mode="fox", dq=HEAD_DIM, extra=(cum, cumt, gate))


def _swiglu_ffn(h, x, gate, w_gate_up, w_down, layer):
    s = h.shape[0]
    tm, tn = min(512, s), 512
    outs = [((s, FFN_HIDDEN), BF16, (tm, tn), lambda n, m: (m, n))]
    act = _mm(h, [(w_gate_up, layer, 0), (w_gate_up, layer, FFN_HIDDEN // tn)], [], outs, _epi_swiglu,
              tm=tm, tn=tn, n_tiles=FFN_HIDDEN // tn)[0]
    return _residual_mm(act, w_down, layer, x, gate, tn=256)


def kernel(x, c, positions, ln_mix_g, ln_ffn_g, ada_w, ada_b, ffn_w_gate_up, ffn_w_down, mla_w_in, mla_q_a_g, mla_kv_a_g, mla_w_q_b, mla_w_kv_b, mla_q_g, mla_k_g, mla_w_out, dsa_w_in, dsa_q_g, dsa_k_g, dsa_idx_k_g, dsa_w_out, diff_w_in, diff_q_g, diff_k_g, diff_lambda_q1, diff_lambda_k1, diff_lambda_q2, diff_lambda_k2, diff_subln_g, diff_w_out, fox_w_in, fox_b_f, fox_q_g, fox_k_g, fox_w_out):
    batch, s, d = x.shape
    assert batch == 1 and d == D_MODEL
    depth = ada_w.shape[0]
    pos = positions[0]
    tabs_head = _rope_tables(pos, HEAD_DIM // 4, LANES)
    tabs_small = _rope_tables(pos, IDX_DIM // 4, IDX_DIM)
    tabs_mla = _rope_tables(pos, MLA_ROPE, LANES)
    mod = _ada_mod(c, ada_w, ada_b)
    xs = x[0]
    for i in range(depth):
        sh1, sc1, g1, sh2, sc2, g2 = [mod[i, t * d:(t + 1) * d] for t in range(6)]
        h = _normmod(xs, ln_mix_g[i], sc1, sh1)
        kind, j = i % 4, i // 4
        if kind == 0:
            o = _mla_mixer(h, tabs_mla, mla_w_in[j:j + 1], mla_q_a_g[j], mla_kv_a_g[j], mla_w_q_b[j:j + 1],
                           mla_w_kv_b[j:j + 1], mla_q_g[j], mla_k_g[j])
            w_out = mla_w_out
        elif kind == 1:
            o = _dsa_mixer(h, tabs_head, tabs_small, dsa_w_in[j:j + 1], dsa_q_g[j], dsa_k_g[j], dsa_idx_k_g[j])
            w_out = dsa_w_out
        elif kind == 2:
            lam_params = jnp.stack([diff_lambda_q1[j], diff_lambda_k1[j], diff_lambda_q2[j], diff_lambda_k2[j]])
            o = _diff_mixer(h, tabs_small, i, diff_w_in[j:j + 1], diff_q_g[j], diff_k_g[j], lam_params,
                            diff_subln_g[j])
            w_out = diff_w_out
        else:
            o = _fox_mixer(h, fox_w_in[j:j + 1], fox_b_f[j], fox_q_g[j], fox_k_g[j])
            w_out = fox_w_out
        xs = _residual_mm(o, w_out, j, xs, g1, tn=512)
        h = _normmod(xs, ln_ffn_g[i], sc2, sh2)
        xs = _swiglu_ffn(h, xs, g2, ffn_w_gate_up, ffn_w_down, i)
    return xs[None]
```

```python
import functools
import math

import jax
import jax.numpy as jnp
from jax import lax
from jax.experimental import pallas as pl
from jax.experimental.pallas import tpu as pltpu

F32 = jnp.float32
BF16 = jnp.bfloat16

D_MODEL = 2048
N_HEADS = 16
HEAD_DIM = 128
ROPE_THETA = 500000.0
RMS_EPS = 1e-6
FFN_HIDDEN = 5632
MLA_LORA = 512
MLA_ROPE = 64
MLA_HEAD = 192
IDX_HEADS = 16
IDX_DIM = 64
IDX_TOPK = 256
DIFF_DIM = 64

LANES = 128
LOG2E = 1.4426950408889634
NEG = -1e30
VMEM_LIMIT_BYTES = 56 * 1024 * 1024
INT_MIN = -2 ** 31
KEY_NEG_INF = (0xFF800000 ^ 0x7FFFFFFF) - 2 ** 32


def _cparams(*sem):
    return pltpu.CompilerParams(dimension_semantics=sem, vmem_limit_bytes=VMEM_LIMIT_BYTES)


def _dot_nt(a, b):
    return lax.dot_general(a, b, (((1,), (1,)), ((), ())), preferred_element_type=F32)


def _sigmoid(z):
    return 1.0 / (1.0 + jnp.exp(-z))


def _mm_body(*refs, n_w, n_aux, n_out, epi):
    a_ref = refs[0]
    w_refs = refs[1:1 + n_w]
    aux_refs = refs[1 + n_w:1 + n_w + n_aux]
    out_refs = refs[1 + n_w + n_aux:1 + n_w + n_aux + n_out]
    wb_refs = refs[1 + n_w + n_aux + n_out:]

    @pl.when(pl.program_id(1) == 0)
    def _cast_weights():
        for w_ref, wb_ref in zip(w_refs, wb_refs):
            wb_ref[...] = w_ref[...].astype(BF16)

    a = a_ref[...]
    accs = [jnp.dot(a, wb_ref[...], preferred_element_type=F32) for wb_ref in wb_refs]
    epi(accs, aux_refs, out_refs)


def _mm(a, ws, aux, outs, epi, *, tm, tn, n_tiles, a_blk=0, name="mm"):
    s = a.shape[0]
    k = ws[0][0].shape[1]
    tm = min(tm, s)
    in_specs = [pl.BlockSpec((tm, k), lambda n, m: (m, a_blk))]
    operands = [a]
    for w, layer, off in ws:
        assert w.shape[1] == k
        in_specs.append(pl.BlockSpec((None, k, tn), lambda n, m, layer=layer, off=off: (layer, 0, n + off)))
        operands.append(w)
    for arr, block, imap in aux:
        in_specs.append(pl.BlockSpec(block, imap))
        operands.append(arr)
    out_shape = [jax.ShapeDtypeStruct(shape, dtype) for shape, dtype, _, _ in outs]
    out_specs = [pl.BlockSpec(block, imap) for _, _, block, imap in outs]
    body = functools.partial(_mm_body, n_w=len(ws), n_aux=len(aux), n_out=len(outs), epi=epi)
    res = pl.pallas_call(
        body,
        grid=(n_tiles, s // tm),
        in_specs=in_specs,
        out_specs=out_specs,
        out_shape=out_shape,
        scratch_shapes=[pltpu.VMEM((k, tn), BF16) for _ in ws],
        compiler_params=_cparams("arbitrary", "arbitrary"),
        name=name,
    )(*operands)
    return res


def _rope_group(y, tabs, half):
    c, s1, s2 = tabs
    return y * c + pltpu.roll(y, half, 1) * s1 + pltpu.roll(y, LANES - half, 1) * s2


def _group_inv_rms(y, gw, real):
    tn = y.shape[1]
    sq = y * y
    n_groups = tn // LANES
    if gw == DIFF_DIM:
        lane = lax.broadcasted_iota(jnp.int32, (1, LANES), 1)
        lo = lane < DIFF_DIM
        out = []
        for g in range(n_groups):
            sg = sq[:, g * LANES:(g + 1) * LANES]
            s_lo = jnp.sum(jnp.where(lo, sg, 0.0), axis=1, keepdims=True)
            s_hi = jnp.sum(jnp.where(lo, 0.0, sg), axis=1, keepdims=True)
            out.append(lax.rsqrt(jnp.where(lo, s_lo, s_hi) * (1.0 / real) + RMS_EPS))
        return out
    sums = [jnp.sum(sq[:, g * LANES:(g + 1) * LANES], axis=1, keepdims=True) for g in range(n_groups)]
    per = gw // LANES
    out = []
    for h in range(n_groups // per):
        tot = sums[h * per]
        for t in range(1, per):
            tot = tot + sums[h * per + t]
        r = lax.rsqrt(tot * (1.0 / real) + RMS_EPS)
        out.extend([r] * per)
    return out


def _epi_heads(accs, aux_refs, out_refs, *, gw, real, half, rope_groups, scale, use_gain, use_rope):
    y = accs[0]
    tn = y.shape[1]
    idx = 0
    gain = None
    if use_gain:
        gain = aux_refs[idx][...]
        idx += 1
        inv = _group_inv_rms(y, gw, real)
    if use_rope:
        tabs = tuple(aux_refs[idx + t][...] for t in range(3))
    for g in range(tn // LANES):
        yg = y[:, g * LANES:(g + 1) * LANES]
        if use_gain:
            yg = yg * inv[g] * gain[:, g * LANES:(g + 1) * LANES]
        if use_rope and rope_groups(g):
            yg = _rope_group(yg, tabs, half)
        if scale != 1.0:
            yg = yg * scale
        out_refs[0][:, g * LANES:(g + 1) * LANES] = yg.astype(out_refs[0].dtype)


def _epi_plain(accs, aux_refs, out_refs):
    out_refs[0][...] = accs[0].astype(out_refs[0].dtype)


def _epi_sigmoid(accs, aux_refs, out_refs):
    out_refs[0][...] = _sigmoid(accs[0]).astype(out_refs[0].dtype)


def _epi_residual(accs, aux_refs, out_refs):
    x_ref, g_ref = aux_refs
    out_refs[0][...] = x_ref[...] + g_ref[...] * accs[0]


def _epi_swiglu(accs, aux_refs, out_refs):
    gate, up = accs
    out_refs[0][...] = (gate * _sigmoid(gate) * up).astype(out_refs[0].dtype)


def _epi_mla_kv(accs, aux_refs, out_refs):
    kpe_ref, g_nope_ref, g_pe_ref, c_ref, s1_ref, s2_ref = aux_refs
    k_ref, v_ref = out_refs
    y = accs[0]
    kpe = kpe_ref[...]
    ss_pe = jnp.sum(kpe * kpe, axis=1, keepdims=True)
    tabs = (c_ref[...], s1_ref[...], s2_ref[...])
    g_nope = g_nope_ref[...]
    g_pe = g_pe_ref[...]
    for h in range(y.shape[1] // (2 * LANES)):
        kn = y[:, 2 * h * LANES:(2 * h + 1) * LANES]
        v = y[:, (2 * h + 1) * LANES:(2 * h + 2) * LANES]
        r = lax.rsqrt((jnp.sum(kn * kn, axis=1, keepdims=True) + ss_pe) * (1.0 / MLA_HEAD) + RMS_EPS)
        k_ref[:, 2 * h * LANES:(2 * h + 1) * LANES] = (kn * r * g_nope).astype(k_ref.dtype)
        pe = _rope_group(kpe * r * g_pe, tabs, MLA_ROPE // 2)
        k_ref[:, (2 * h + 1) * LANES:(2 * h + 2) * LANES] = pe.astype(k_ref.dtype)
        v_ref[:, h * LANES:(h + 1) * LANES] = v.astype(v_ref.dtype)


def _row_tab_aux(tabs, tm):
    return [(t, (tm, LANES), lambda n, m: (m, 0)) for t in tabs]


def _proj(a, w, layer, col_off, width, *, tn, out_dtype=BF16, gain=None, tabs=None, gw=LANES, real=LANES,
          half=0, rope_groups=lambda g: True, scale=1.0, epi=None, tm=512, a_blk=0):
    s = a.shape[0]
    tm = min(tm, s)
    assert col_off % tn == 0 and width % tn == 0
    aux = []
    if gain is not None:
        aux.append((gain.reshape(1, width).astype(F32), (1, tn), lambda n, m: (0, n)))
    if tabs is not None:
        aux.extend(_row_tab_aux(tabs, tm))
    if epi is None:
        epi = functools.partial(_epi_heads, gw=gw, real=real, half=half, rope_groups=rope_groups, scale=scale,
                                use_gain=gain is not None, use_rope=tabs is not None)
    outs = [((s, width), out_dtype, (tm, tn), lambda n, m: (m, n))]
    return _mm(a, [(w, layer, col_off // tn)], aux, outs, epi, tm=tm, tn=tn, n_tiles=width // tn, a_blk=a_blk,
               name="proj")[0]


def _residual_mm(a, w, layer, x, gate, *, tn, tm=512):
    s, d = x.shape
    tm = min(tm, s)
    aux = [(x, (tm, tn), lambda n, m: (m, n)), (gate.reshape(1, d), (1, tn), lambda n, m: (0, n))]
    outs = [((s, d), F32, (tm, tn), lambda n, m: (m, n))]
    return _mm(a, [(w, layer, 0)], aux, outs, _epi_residual, tm=tm, tn=tn, n_tiles=d // tn, name="residual_mm")[0]


def _ada_body(c_ref, w_ref, b_ref, o_ref):
    c = c_ref[...]
    cond = c * _sigmoid(c)
    o_ref[...] = jnp.dot(cond.astype(BF16), w_ref[...].astype(BF16), preferred_element_type=F32) + b_ref[...]


def _ada_mod(c, ada_w, ada_b):
    depth, d, n = ada_w.shape
    tn = 1536
    c8 = jnp.broadcast_to(c, (8, d))
    out = pl.pallas_call(
        _ada_body,
        grid=(depth, n // tn),
        in_specs=[pl.BlockSpec((8, d), lambda l, j: (0, 0)),
                  pl.BlockSpec((None, d, tn), lambda l, j: (l, 0, j)),
                  pl.BlockSpec((None, 1, tn), lambda l, j: (l, 0, j))],
        out_specs=pl.BlockSpec((None, 8, tn), lambda l, j: (l, 0, j)),
        out_shape=jax.ShapeDtypeStruct((depth, 8, n), F32),
        compiler_params=_cparams("arbitrary", "arbitrary"),
        name="ada_mod",
    )(c8, ada_w, ada_b.reshape(depth, 1, n))
    return out[:, 0, :]


def _normmod_body(x_ref, g_ref, sc_ref, sh_ref, o_ref):
    x = x_ref[...]
    y = x * lax.rsqrt(jnp.mean(x * x, axis=1, keepdims=True) + RMS_EPS) * g_ref[...]
    o_ref[...] = (y * (1.0 + sc_ref[...]) + sh_ref[...]).astype(o_ref.dtype)


def _normmod(x, g, sc, sh, tm=256):
    s, d = x.shape
    tm = min(tm, s)
    vec = pl.BlockSpec((1, d), lambda m: (0, 0))
    return pl.pallas_call(
        _normmod_body,
        grid=(s // tm,),
        in_specs=[pl.BlockSpec((tm, d), lambda m: (m, 0)), vec, vec, vec],
        out_specs=pl.BlockSpec((tm, d), lambda m: (m, 0)),
        out_shape=jax.ShapeDtypeStruct((s, d), BF16),
        compiler_params=_cparams("arbitrary"),
        name="normmod",
    )(x, g.reshape(1, d), sc.reshape(1, d), sh.reshape(1, d))


def _softmax_scratch(n_maps, tq, tk):
    return [pltpu.VMEM((n_maps, tq, 2 * HEAD_DIM), F32), pltpu.VMEM((n_maps, tq, LANES), F32),
            pltpu.VMEM((n_maps, 2, tq, tk), F32), pltpu.VMEM((n_maps, 2, 2, tq, LANES), F32)]


def _softmax_pipeline(qs, k_ref, v_ref, scratch, *, tq, tk, first_chunk, first_keep, n_rest, bias_fn):
    acc_ref, m_ref, s_ref, st_ref = scratch
    ones = jnp.ones((tk, LANES), BF16)
    n_grp = tk // LANES
    maps = range(len(qs))

    def chunk_start(c):
        return pl.multiple_of(c * tk, tk)

    def stage1(c, slot, keep):
        start = chunk_start(c)
        k = k_ref[pl.ds(start, tk), :]
        for t in maps:
            s = _dot_nt(qs[t], k)
            if bias_fn is not None:
                s = s + bias_fn(start)
            if keep is not None:
                s = jnp.where(keep, s, NEG)
            m_run = m_ref[t]
            m_new = jnp.maximum(m_run, jnp.max(s, axis=1, keepdims=True))
            s_ref[t, slot] = s
            st_ref[t, slot, 0] = m_new
            st_ref[t, slot, 1] = jnp.exp2(m_run - m_new)
            m_ref[t] = m_new

    def stage2(c, slot):
        v1 = jnp.concatenate([v_ref[pl.ds(chunk_start(c), tk), :], ones], axis=1)
        for t in maps:
            m_new = st_ref[t, slot, 0]
            alpha = st_ref[t, slot, 1]
            p = jnp.concatenate([jnp.exp2(s_ref[t, slot, :, g * LANES:(g + 1) * LANES] - m_new)
                                 for g in range(n_grp)], axis=1)
            pv = jnp.dot(p.astype(BF16), v1, preferred_element_type=F32)
            acc_ref[t] = jnp.concatenate([alpha, alpha], axis=1) * acc_ref[t] + pv

    acc_ref[...] = jnp.zeros(acc_ref.shape, F32)
    m_ref[...] = jnp.full(m_ref.shape, NEG, F32)
    stage1(first_chunk, 0, first_keep)

    def body(t2, carry):
        c0 = 2 * t2
        stage1(c0, 1, None)
        stage2(jnp.where(t2 == 0, first_chunk, c0 - 1), 0)
        stage1(c0 + 1, 0, None)
        stage2(c0, 1)
        return carry

    n_pairs = n_rest // 2
    lax.fori_loop(0, n_pairs, body, 0)
    pending = jnp.where(n_pairs == 0, first_chunk, 2 * n_pairs - 1)

    @pl.when(n_rest % 2 == 1)
    def _odd_tail():
        stage1(n_rest - 1, 1, None)
        stage2(pending, 0)
        stage2(n_rest - 1, 1)

    @pl.when(n_rest % 2 == 0)
    def _even_tail():
        stage2(pending, 0)


def _causal_keep(tq, tk, col_shift):
    row = lax.broadcasted_iota(jnp.int32, (tq, 1), 0)
    col = lax.broadcasted_iota(jnp.int32, (1, tk), 1)
    return row >= col + col_shift


def _dense_attn_body(*refs, mode, tq, tk, lam_init):
    if mode == "mla":
        q_ref, k_ref, v_ref, o_ref = refs[:4]
        scratch = refs[4:]
    elif mode == "fox":
        q_ref, k_ref, v_ref, cum_ref, cumt_ref, gate_ref, o_ref = refs[:7]
        scratch = refs[7:]
    else:
        q_ref, k_ref, v_ref, lam_ref, subg_ref, o_ref = refs[:6]
        scratch = refs[6:]
    h = pl.program_id(0)
    i = pl.program_id(1)
    acc_ref = scratch[0]

    q = q_ref[...]
    if mode == "diff":
        lane = lax.broadcasted_iota(jnp.int32, (1, LANES), 1)
        zero = jnp.zeros_like(q)
        qs = [jnp.where(lane < DIFF_DIM, q, zero), jnp.where(lane >= DIFF_DIM, q, zero)]
    else:
        qs = [q]
    if mode == "fox":
        lane = lax.broadcasted_iota(jnp.int32, (1, LANES), 1)
        cq = jnp.sum(jnp.where(lane == h, cum_ref[...], 0.0), axis=1, keepdims=True) * LOG2E

        def bias_fn(start):
            return cq - cumt_ref[:, pl.ds(start, tk)] * LOG2E
    else:
        bias_fn = None

    _softmax_pipeline(qs, k_ref, v_ref, scratch, tq=tq, tk=tk, first_chunk=i, first_keep=_causal_keep(tq, tk, 0),
                      n_rest=i, bias_fn=bias_fn)

    def normalised(t):
        acc = acc_ref[t]
        return acc[:, :HEAD_DIM] / acc[:, HEAD_DIM:]

    if mode == "diff":
        lp = lam_ref[...]
        lam = (jnp.exp(jnp.sum(lp[0:1] * lp[1:2], axis=1, keepdims=True))
               - jnp.exp(jnp.sum(lp[2:3] * lp[3:4], axis=1, keepdims=True)) + lam_init)
        o = normalised(0) - lam * normalised(1)
        o = o * lax.rsqrt(jnp.mean(o * o, axis=1, keepdims=True) + RMS_EPS) * subg_ref[...]
        o = o * (1.0 - lam_init)
    else:
        o = normalised(0)
        if mode == "fox":
            o = o * gate_ref[...]
    o_ref[...] = o.astype(o_ref.dtype)


def _dense_attention(q, k, v, *, mode, dq, extra=(), lam_init=0.0, tq=512):
    s = q.shape[0]
    tq = min(tq, s)
    tk = tq
    in_specs = [pl.BlockSpec((tq, dq), lambda h, i: (i, h)),
                pl.BlockSpec((s, dq), lambda h, i: (0, h)),
                pl.BlockSpec((s, HEAD_DIM), lambda h, i: (0, h))]
    operands = [q, k, v]
    if mode == "fox":
        cum, cumt, gate = extra
        in_specs += [pl.BlockSpec((tq, LANES), lambda h, i: (i, 0)),
                     pl.BlockSpec((None, 1, s), lambda h, i: (h, 0, 0)),
                     pl.BlockSpec((tq, HEAD_DIM), lambda h, i: (i, h))]
        operands += [cum, cumt, gate]
    elif mode == "diff":
        lam_params, subln_g = extra
        in_specs += [pl.BlockSpec((4, DIFF_DIM), lambda h, i: (0, 0)),
                     pl.BlockSpec((1, HEAD_DIM), lambda h, i: (0, 0))]
        operands += [lam_params, subln_g]
    n_maps = 2 if mode == "diff" else 1
    body = functools.partial(_dense_attn_body, mode=mode, tq=tq, tk=tk, lam_init=lam_init)
    return pl.pallas_call(
        body,
        grid=(N_HEADS, s // tq),
        in_specs=in_specs,
        out_specs=pl.BlockSpec((tq, HEAD_DIM), lambda h, i: (i, h)),
        out_shape=jax.ShapeDtypeStruct((s, N_HEADS * HEAD_DIM), BF16),
        scratch_shapes=_softmax_scratch(n_maps, tq, tk),
        compiler_params=_cparams("arbitrary", "arbitrary"),
        name="attn_" + mode,
    )(*operands)


def _dsa_attn_body(q_ref, k_ref, v_ref, bias_ref, o_ref, *scratch, tq, tk):
    i = pl.program_id(0)

    def bias_fn(start):
        return bias_ref[:, pl.ds(start, tk)].astype(F32)

    _softmax_pipeline([q_ref[...]], k_ref, v_ref, scratch, tq=tq, tk=tk, first_chunk=i, first_keep=None,
                      n_rest=i, bias_fn=bias_fn)
    acc = scratch[0][0]
    o_ref[...] = (acc[:, :HEAD_DIM] / acc[:, HEAD_DIM:]).astype(o_ref.dtype)


def _dsa_attention(q, k, v, bias, *, tq=512):
    s = q.shape[0]
    tq = min(tq, s)
    tk = tq
    body = functools.partial(_dsa_attn_body, tq=tq, tk=tk)
    return pl.pallas_call(
        body,
        grid=(s // tq, N_HEADS),
        in_specs=[pl.BlockSpec((tq, HEAD_DIM), lambda i, h: (i, h)),
                  pl.BlockSpec((s, HEAD_DIM), lambda i, h: (0, h)),
                  pl.BlockSpec((s, HEAD_DIM), lambda i, h: (0, h)),
                  pl.BlockSpec((tq, s), lambda i, h: (i, 0))],
        out_specs=pl.BlockSpec((tq, HEAD_DIM), lambda i, h: (i, h)),
        out_shape=jax.ShapeDtypeStruct((s, N_HEADS * HEAD_DIM), BF16),
        scratch_shapes=_softmax_scratch(1, tq, tk),
        compiler_params=_cparams("arbitrary", "arbitrary"),
        name="attn_dsa",
    )(q, k, v, bias)


def _sortable_key(score):
    bits = lax.bitcast_convert_type(score, jnp.int32)
    return bits ^ ((bits >> 31) & 0x7FFFFFFF)


def _indexer_body(iq_ref, ik_ref, iw_ref, bias_ref, keys_ref, qz_ref, wb_ref, *, tq, n_sel):
    i = pl.program_id(0)
    s_len = ik_ref.shape[0]
    n_chunks = s_len // tq
    lane = lax.broadcasted_iota(jnp.int32, (1, LANES), 1)
    iw = iw_ref[...]
    for p in range(IDX_HEADS // 2):
        pair = iq_ref[:, p * LANES:(p + 1) * LANES]
        zero = jnp.zeros_like(pair)
        qz_ref[2 * p] = jnp.where(lane < IDX_DIM, pair, zero)
        qz_ref[2 * p + 1] = jnp.where(lane >= IDX_DIM, pair, zero)
    for hh in range(IDX_HEADS):
        wb_ref[hh] = jnp.broadcast_to(iw[:, hh:hh + 1], (tq, LANES))

    def score_chunk(start):
        ik = ik_ref[pl.ds(start, tq), :]
        sc = jnp.zeros((tq, tq), F32)
        for hh in range(IDX_HEADS):
            rel = jnp.maximum(_dot_nt(qz_ref[hh], ik), 0.0)
            sc = sc + jnp.tile(wb_ref[hh], (1, tq // LANES)) * rel
        return sc

    def full_body(j, carry):
        start = pl.multiple_of(j * tq, tq)
        keys_ref[:, pl.ds(start, tq)] = _sortable_key(score_chunk(start))
        return carry

    lax.fori_loop(0, i, full_body, 0)
    diag = pl.multiple_of(i * tq, tq)
    sc = jnp.where(_causal_keep(tq, tq, 0), score_chunk(diag), -jnp.inf)
    keys_ref[:, pl.ds(diag, tq)] = _sortable_key(sc)

    def bit_body(b, cand):
        trial = cand | lax.shift_left(jnp.int32(1), 31 - b)
        trial_signed = trial ^ INT_MIN

        def count_body(j, cnt):
            kk = keys_ref[:, pl.ds(pl.multiple_of(j * tq, tq), tq)]
            ge = (kk >= trial_signed).astype(jnp.int32)
            for g in range(tq // LANES):
                cnt = cnt + ge[:, g * LANES:(g + 1) * LANES]
            return cnt

        cnt = lax.fori_loop(0, i + 1, count_body, jnp.zeros((tq, LANES), jnp.int32))
        total = jnp.sum(cnt, axis=1, keepdims=True)
        return jnp.where(total >= n_sel, trial, cand)

    cand = lax.fori_loop(0, 32, bit_body, jnp.zeros((tq, 1), jnp.int32))
    thr = jnp.maximum(cand ^ INT_MIN, KEY_NEG_INF + 1)

    def write_body(j, carry):
        start = pl.multiple_of(j * tq, tq)
        kk = keys_ref[:, pl.ds(start, tq)]
        bias_ref[:, pl.ds(start, tq)] = jnp.where(kk >= thr, 0.0, NEG).astype(bias_ref.dtype)
        return carry

    lax.fori_loop(0, i + 1, write_body, 0)

    def fill_body(j, carry):
        bias_ref[:, pl.ds(pl.multiple_of(j * tq, tq), tq)] = jnp.full((tq, tq), NEG, bias_ref.dtype)
        return carry

    lax.fori_loop(i + 1, n_chunks, fill_body, 0)


def _dsa_select_bias(iq, ik2, iw, n_sel, tq=256):
    s = iq.shape[0]
    tq = min(tq, s)
    body = functools.partial(_indexer_body, tq=tq, n_sel=n_sel)
    return pl.pallas_call(
        body,
        grid=(s // tq,),
        in_specs=[pl.BlockSpec((tq, IDX_HEADS * IDX_DIM), lambda i: (i, 0)),
                  pl.BlockSpec((s, LANES), lambda i: (0, 0)),
                  pl.BlockSpec((tq, LANES), lambda i: (i, 0))],
        out_specs=pl.BlockSpec((tq, s), lambda i: (i, 0)),
        out_shape=jax.ShapeDtypeStruct((s, s), BF16),
        scratch_shapes=[pltpu.VMEM((tq, s), jnp.int32),
                        pltpu.VMEM((IDX_HEADS, tq, LANES), BF16),
                        pltpu.VMEM((IDX_HEADS, tq, LANES), F32)],
        compiler_params=_cparams("arbitrary"),
        name="dsa_select",
    )(iq, ik2, iw)


def _fox_cum_body(f_ref, b_ref, o_ref, carry_ref, *, tc):
    @pl.when(pl.program_id(0) == 0)
    def _():
        carry_ref[...] = jnp.zeros(carry_ref.shape, F32)

    z = f_ref[...] + b_ref[...]
    lf = jnp.minimum(z, 0.0) - jnp.log(1.0 + jnp.exp(-jnp.abs(z)))
    row = lax.broadcasted_iota(jnp.int32, (tc, tc), 0)
    col = lax.broadcasted_iota(jnp.int32, (tc, tc), 1)
    tri = (row >= col).astype(BF16)
    hi = lf.astype(BF16)
    r1 = lf - hi.astype(F32)
    mid = r1.astype(BF16)
    lo = (r1 - mid.astype(F32)).astype(BF16)
    cs = (jnp.dot(tri, hi, preferred_element_type=F32) + jnp.dot(tri, mid, preferred_element_type=F32)
          + jnp.dot(tri, lo, preferred_element_type=F32)) + carry_ref[...]
    o_ref[...] = cs
    carry_ref[...] = cs[tc - 1:tc, :]


def _fox_cumsum(f_raw, b_pad, tc=512):
    s = f_raw.shape[0]
    tc = min(tc, s)
    return pl.pallas_call(
        functools.partial(_fox_cum_body, tc=tc),
        grid=(s // tc,),
        in_specs=[pl.BlockSpec((tc, LANES), lambda m: (m, 0)), pl.BlockSpec((1, LANES), lambda m: (0, 0))],
        out_specs=pl.BlockSpec((tc, LANES), lambda m: (m, 0)),
        out_shape=jax.ShapeDtypeStruct((s, LANES), F32),
        scratch_shapes=[pltpu.VMEM((1, LANES), F32)],
        compiler_params=_cparams("arbitrary"),
        name="fox_cumsum",
    )(f_raw, b_pad)


def _rope_tables(positions, rot_dim, period):
    half = rot_dim // 2
    inv_freq = ROPE_THETA ** (-jnp.arange(0, rot_dim, 2, dtype=F32) / rot_dim)
    ang = positions.astype(F32)[:, None] * inv_freq
    cos, sin = jnp.cos(ang), jnp.sin(ang)
    lp = jnp.arange(LANES) % period
    in_x1 = lp < half
    in_x2 = (lp >= half) & (lp < 2 * half)
    idx = jnp.where(in_x1, lp, jnp.clip(lp - half, 0, half - 1))
    cos_l, sin_l = cos[:, idx], sin[:, idx]
    c = jnp.where(in_x1 | in_x2, cos_l, 1.0)
    s1 = jnp.where(in_x2, sin_l, 0.0)
    s2 = jnp.where(in_x1, -sin_l, 0.0)
    return c, s1, s2


def _pad_cols(w, width):
    return jnp.pad(w, ((0, 0), (0, width - w.shape[1])))


def _mla_mixer(h, tabs, w_in, q_a_g, kv_a_g, w_q_b, w_kv_b, q_g, k_g):
    s = h.shape[0]
    nope = HEAD_DIM
    scale = MLA_HEAD ** -0.5 * LOG2E
    gains = jnp.concatenate([q_a_g, kv_a_g])
    c_norm = _proj(h, w_in, 0, 0, 2 * MLA_LORA, tn=MLA_LORA, gain=gains, gw=MLA_LORA, real=MLA_LORA)
    kpe = _proj(h, _pad_cols(w_in[0][:, 2 * MLA_LORA:], LANES)[None], 0, 0, LANES, tn=LANES, out_dtype=F32,
                epi=_epi_plain)
    wq = w_q_b[0].reshape(MLA_LORA, N_HEADS, MLA_HEAD)
    wq = jnp.concatenate([wq[:, :, MLA_ROPE:], wq[:, :, :MLA_ROPE],
                          jnp.zeros((MLA_LORA, N_HEADS, 2 * LANES - MLA_HEAD), F32)], axis=2)
    wq = wq.reshape(1, MLA_LORA, N_HEADS * 2 * LANES)
    qg = jnp.concatenate([q_g[MLA_ROPE:], q_g[:MLA_ROPE], jnp.zeros((2 * LANES - MLA_HEAD,), F32)])
    q = _proj(c_norm, wq, 0, 0, N_HEADS * 2 * LANES, tn=512, gain=jnp.tile(qg, N_HEADS), tabs=tabs,
              gw=2 * LANES, real=MLA_HEAD, half=MLA_ROPE // 2, rope_groups=lambda g: g % 2 == 1, scale=scale)
    tm, tn = min(512, s), 512
    aux = [(kpe, (tm, LANES), lambda n, m: (m, 0)),
           (k_g[MLA_ROPE:].reshape(1, nope), (1, nope), lambda n, m: (0, 0)),
           (jnp.pad(k_g[:MLA_ROPE], (0, LANES - MLA_ROPE)).reshape(1, LANES), (1, LANES), lambda n, m: (0, 0))]
    aux += _row_tab_aux(tabs, tm)
    outs = [((s, N_HEADS * 2 * LANES), BF16, (tm, tn), lambda n, m: (m, n)),
            ((s, N_HEADS * HEAD_DIM), BF16, (tm, tn // 2), lambda n, m: (m, n))]
    k, v = _mm(c_norm, [(w_kv_b, 0, 0)], aux, outs, _epi_mla_kv, tm=tm, tn=tn,
               n_tiles=N_HEADS * 2 * LANES // tn, a_blk=1, name="mla_kv")
    return _dense_attention(q, k, v, mode="mla", dq=2 * LANES)


def _dsa_mixer(h, tabs_head, tabs_idx, w_in, q_g, k_g, idx_k_g):
    s = h.shape[0]
    hd = N_HEADS * HEAD_DIM
    ih = IDX_HEADS * IDX_DIM
    half = HEAD_DIM // 8
    q = _proj(h, w_in, 0, 0, hd, tn=512, gain=jnp.tile(q_g, N_HEADS), tabs=tabs_head, half=half,
              scale=HEAD_DIM ** -0.5 * LOG2E)
    k = _proj(h, w_in, 0, hd, hd, tn=512, gain=jnp.tile(k_g, N_HEADS), tabs=tabs_head, half=half)
    v = _proj(h, w_in, 0, 2 * hd, hd, tn=512, epi=_epi_plain)
    iq = _proj(h, w_in, 0, 3 * hd, ih, tn=512, tabs=tabs_idx, half=IDX_DIM // 8)
    w_ik = w_in[0][:, 3 * hd + ih:3 * hd + ih + IDX_DIM]
    ik2 = _proj(h, jnp.concatenate([w_ik, w_ik], axis=1)[None], 0, 0, LANES, tn=LANES,
                gain=jnp.tile(idx_k_g, 2), tabs=tabs_idx, half=IDX_DIM // 8)
    w_iw = _pad_cols(w_in[0][:, 3 * hd + ih + IDX_DIM:], LANES)[None]
    iw = _proj(h, w_iw, 0, 0, LANES, tn=LANES, out_dtype=F32,
               epi=functools.partial(_epi_heads, gw=LANES, real=LANES, half=0, rope_groups=None,
                                     scale=IDX_HEADS ** -0.5 * IDX_DIM ** -0.5, use_gain=False, use_rope=False))
    bias = _dsa_select_bias(iq, ik2, iw, min(IDX_TOPK, s // 4))
    return _dsa_attention(q, k, v, bias)


def _diff_mixer(h, tabs, layer_idx, w_in, q_g, k_g, lam_params, subln_g):
    w = N_HEADS * 2 * DIFF_DIM
    half = DIFF_DIM // 8
    q = _proj(h, w_in, 0, 0, w, tn=512, gain=jnp.tile(q_g, 2 * N_HEADS), tabs=tabs, gw=DIFF_DIM, real=DIFF_DIM,
              half=half, scale=DIFF_DIM ** -0.5 * LOG2E)
    k = _proj(h, w_in, 0, w, w, tn=512, gain=jnp.tile(k_g, 2 * N_HEADS), tabs=tabs, gw=DIFF_DIM, real=DIFF_DIM,
              half=half)
    v = _proj(h, w_in, 0, 2 * w, w, tn=512, epi=_epi_plain)
    lam_init = 0.8 - 0.6 * math.exp(-0.3 * layer_idx)
    return _dense_attention(q, k, v, mode="diff", dq=HEAD_DIM, extra=(lam_params, subln_g.reshape(1, HEAD_DIM)),
                            lam_init=lam_init)


def _fox_mixer(h, w_in, b_f, q_g, k_g):
    s = h.shape[0]
    hd = N_HEADS * HEAD_DIM
    q = _proj(h, w_in, 0, 0, hd, tn=512, gain=jnp.tile(q_g, N_HEADS), scale=HEAD_DIM ** -0.5 * LOG2E)
    k = _proj(h, w_in, 0, hd, hd, tn=512, gain=jnp.tile(k_g, N_HEADS))
    v = _proj(h, w_in, 0, 2 * hd, hd, tn=512, epi=_epi_plain)
    w_f = _pad_cols(w_in[0][:, 3 * hd:3 * hd + N_HEADS], LANES)[None]
    f_raw = _proj(h, w_f, 0, 0, LANES, tn=LANES, out_dtype=F32, epi=_epi_plain)
    gate = _proj(h, w_in[:, :, 3 * hd + N_HEADS:], 0, 0, hd, tn=512, out_dtype=F32, epi=_epi_sigmoid)
    cum = _fox_cumsum(f_raw, jnp.pad(b_f, (0, LANES - N_HEADS)).reshape(1, LANES))
    cumt = cum[:, :N_HEADS].T.reshape(N_HEADS, 1, s)
    return _dense_attention(q, k, v, mode="fox", dq=HEAD_DIM, extra=(cum, cumt, gate))


def _swiglu_ffn(h, x, gate, w_gate_up, w_down, layer):
    s = h.shape[0]
    tm, tn = min(512, s), 512
    outs = [((s, FFN_HIDDEN), BF16, (tm, tn), lambda n, m: (m, n))]
    act = _mm(h, [(w_gate_up, layer, 0), (w_gate_up, layer, FFN_HIDDEN // tn)], [], outs, _epi_swiglu,
              tm=tm, tn=tn, n_tiles=FFN_HIDDEN // tn, name="ffn_gate_up")[0]
    return _residual_mm(act, w_down, layer, x, gate, tn=256)


def kernel(x, c, positions, ln_mix_g, ln_ffn_g, ada_w, ada_b, ffn_w_gate_up, ffn_w_down, mla_w_in, mla_q_a_g, mla_kv_a_g, mla_w_q_b, mla_w_kv_b, mla_q_g, mla_k_g, mla_w_out, dsa_w_in, dsa_q_g, dsa_k_g, dsa_idx_k_g, dsa_w_out, diff_w_in, diff_q_g, diff_k_g, diff_lambda_q1, diff_lambda_k1, diff_lambda_q2, diff_lambda_k2, diff_subln_g, diff_w_out, fox_w_in, fox_b_f, fox_q_g, fox_k_g, fox_w_out):
    batch, s, d = x.shape
    assert batch == 1 and d == D_MODEL
    depth = ada_w.shape[0]
    pos = positions[0]
    tabs_head = _rope_tables(pos, HEAD_DIM // 4, LANES)
    tabs_small = _rope_tables(pos, IDX_DIM // 4, IDX_DIM)
    tabs_mla = _rope_tables(pos, MLA_ROPE, LANES)
    mod = _ada_mod(c, ada_w, ada_b)
    xs = x[0]
    for i in range(depth):
        sh1, sc1, g1, sh2, sc2, g2 = [mod[i, t * d:(t + 1) * d] for t in range(6)]
        h = _normmod(xs, ln_mix_g[i], sc1, sh1)
        kind, j = i % 4, i // 4
        if kind == 0:
            o = _mla_mixer(h, tabs_mla, mla_w_in[j:j + 1], mla_q_a_g[j], mla_kv_a_g[j], mla_w_q_b[j:j + 1],
                           mla_w_kv_b[j:j + 1], mla_q_g[j], mla_k_g[j])
            w_out = mla_w_out
        elif kind == 1:
            o = _dsa_mixer(h, tabs_head, tabs_small, dsa_w_in[j:j + 1], dsa_q_g[j], dsa_k_g[j], dsa_idx_k_g[j])
            w_out = dsa_w_out
        elif kind == 2:
            lam_params = jnp.stack([diff_lambda_q1[j], diff_lambda_k1[j], diff_lambda_q2[j], diff_lambda_k2[j]])
            o = _diff_mixer(h, tabs_small, i, diff_w_in[j:j + 1], diff_q_g[j], diff_k_g[j], lam_params,
                            diff_subln_g[j])
            w_out = diff_w_out
        else:
            o = _fox_mixer(h, fox_w_in[j:j + 1], fox_b_f[j], fox_q_g[j], fox_k_g[j])
            w_out = fox_w_out
        xs = _residual_mm(o, w_out, j, xs, g1, tn=512)
        h = _normmod(xs, ln_ffn_g[i], sc2, sh2)
        xs = _swiglu_ffn(h, xs, g2, ffn_w_gate_up, ffn_w_down, i)
    return xs[None]
```

```python
import functools
import math

import jax
import jax.numpy as jnp
from jax import lax
from jax.experimental import pallas as pl
from jax.experimental.pallas import tpu as pltpu

F32 = jnp.float32
BF16 = jnp.bfloat16

D_MODEL = 2048
N_HEADS = 16
HEAD_DIM = 128
ROPE_THETA = 500000.0
RMS_EPS = 1e-6
FFN_HIDDEN = 5632
MLA_LORA = 512
MLA_ROPE = 64
MLA_HEAD = 192
IDX_HEADS = 16
IDX_DIM = 64
IDX_TOPK = 256
DIFF_DIM = 64

LANES = 128
LOG2E = 1.4426950408889634
NEG = -1e30
VMEM_LIMIT_BYTES = 56 * 1024 * 1024
TM = 1024
TM_DEEP = 512
SUB_ROWS = 256
TN = 512
INT_MIN = -2 ** 31
KEY_NEG_INF = (0xFF800000 ^ 0x7FFFFFFF) - 2 ** 32


def _cparams(*sem):
    return pltpu.CompilerParams(dimension_semantics=sem, vmem_limit_bytes=VMEM_LIMIT_BYTES)


def _dot_nt(a, b):
    return lax.dot_general(a, b, (((1,), (1,)), ((), ())), preferred_element_type=F32)


def _sigmoid(z):
    return 1.0 / (1.0 + jnp.exp(-z))


def _mm_body(*refs, n_w, n_aux, n_out, epi, sub):
    a_ref = refs[0]
    w_refs = refs[1:1 + n_w]
    aux_refs = refs[1 + n_w:1 + n_w + n_aux]
    out_refs = refs[1 + n_w + n_aux:1 + n_w + n_aux + n_out]
    wb_refs = refs[1 + n_w + n_aux + n_out:]
    tm = a_ref.shape[0]

    @pl.when(pl.program_id(1) == 0)
    def _cast_weights():
        for w_ref, wb_ref in zip(w_refs, wb_refs):
            wb_ref[...] = w_ref[...].astype(BF16)

    for r in range(tm // sub):
        rows = pl.ds(r * sub, sub)
        accs = [jnp.dot(a_ref[rows, :], wb_ref[...], preferred_element_type=F32) for wb_ref in wb_refs]
        epi(accs, [ref.at[rows] if ref.shape[0] == tm else ref for ref in aux_refs],
            [ref.at[rows] for ref in out_refs])


def _mm(a, ws, aux, outs, epi, *, tm, tn, n_tiles, a_blk=0, name="mm", sub=SUB_ROWS):
    s = a.shape[0]
    k = ws[0][0].shape[1]
    tm = min(tm, s)
    in_specs = [pl.BlockSpec((tm, k), lambda n, m: (m, a_blk))]
    operands = [a]
    for w, layer, off in ws:
        assert w.shape[1] == k
        in_specs.append(pl.BlockSpec((None, k, tn), lambda n, m, layer=layer, off=off: (layer, 0, n + off)))
        operands.append(w)
    for arr, block, imap in aux:
        in_specs.append(pl.BlockSpec(block, imap))
        operands.append(arr)
    out_shape = [jax.ShapeDtypeStruct(shape, dtype) for shape, dtype, _, _ in outs]
    out_specs = [pl.BlockSpec(block, imap) for _, _, block, imap in outs]
    body = functools.partial(_mm_body, n_w=len(ws), n_aux=len(aux), n_out=len(outs), epi=epi, sub=min(sub, tm))
    res = pl.pallas_call(
        body,
        grid=(n_tiles, s // tm),
        in_specs=in_specs,
        out_specs=out_specs,
        out_shape=out_shape,
        scratch_shapes=[pltpu.VMEM((k, tn), BF16) for _ in ws],
        compiler_params=_cparams("arbitrary", "arbitrary"),
        name=name,
    )(*operands)
    return res


def _rope_group(y, tabs, half):
    c, s1, s2 = tabs
    return y * c + pltpu.roll(y, half, 1) * s1 + pltpu.roll(y, LANES - half, 1) * s2


def _group_inv_rms(y, gw, real):
    tn = y.shape[1]
    sq = y * y
    n_groups = tn // LANES
    if gw == DIFF_DIM:
        lane = lax.broadcasted_iota(jnp.int32, (1, LANES), 1)
        lo = lane < DIFF_DIM
        out = []
        for g in range(n_groups):
            sg = sq[:, g * LANES:(g + 1) * LANES]
            s_lo = jnp.sum(jnp.where(lo, sg, 0.0), axis=1, keepdims=True)
            s_hi = jnp.sum(jnp.where(lo, 0.0, sg), axis=1, keepdims=True)
            out.append(lax.rsqrt(jnp.where(lo, s_lo, s_hi) * (1.0 / real) + RMS_EPS))
        return out
    sums = [jnp.sum(sq[:, g * LANES:(g + 1) * LANES], axis=1, keepdims=True) for g in range(n_groups)]
    per = gw // LANES
    out = []
    for h in range(n_groups // per):
        tot = sums[h * per]
        for t in range(1, per):
            tot = tot + sums[h * per + t]
        r = lax.rsqrt(tot * (1.0 / real) + RMS_EPS)
        out.extend([r] * per)
    return out


def _epi_heads(accs, aux_refs, out_refs, *, gw, real, half, rope_groups, scale, use_gain, use_rope):
    y = accs[0]
    tn = y.shape[1]
    idx = 0
    gain = None
    if use_gain:
        gain = aux_refs[idx][...]
        idx += 1
        inv = _group_inv_rms(y, gw, real)
    if use_rope:
        tabs = tuple(aux_refs[idx + t][...] for t in range(3))
    for g in range(tn // LANES):
        yg = y[:, g * LANES:(g + 1) * LANES]
        if use_gain:
            yg = yg * inv[g] * gain[:, g * LANES:(g + 1) * LANES]
        if use_rope and rope_groups(g):
            yg = _rope_group(yg, tabs, half)
        if scale != 1.0:
            yg = yg * scale
        out_refs[0][:, g * LANES:(g + 1) * LANES] = yg.astype(out_refs[0].dtype)


def _epi_plain(accs, aux_refs, out_refs):
    out_refs[0][...] = accs[0].astype(out_refs[0].dtype)


def _epi_sigmoid(accs, aux_refs, out_refs):
    out_refs[0][...] = _sigmoid(accs[0]).astype(out_refs[0].dtype)


def _epi_residual(accs, aux_refs, out_refs):
    x_ref, g_ref = aux_refs
    out_refs[0][...] = x_ref[...] + g_ref[...] * accs[0]


def _epi_swiglu(accs, aux_refs, out_refs):
    gate, up = accs
    out_refs[0][...] = (gate * _sigmoid(gate) * up).astype(out_refs[0].dtype)


def _epi_mla_kv(accs, aux_refs, out_refs):
    kpe_ref, g_nope_ref, g_pe_ref, c_ref, s1_ref, s2_ref = aux_refs
    k_ref, v_ref = out_refs
    y = accs[0]
    kpe = kpe_ref[...]
    ss_pe = jnp.sum(kpe * kpe, axis=1, keepdims=True)
    tabs = (c_ref[...], s1_ref[...], s2_ref[...])
    g_nope = g_nope_ref[...]
    g_pe = g_pe_ref[...]
    for h in range(y.shape[1] // (2 * LANES)):
        kn = y[:, 2 * h * LANES:(2 * h + 1) * LANES]
        v = y[:, (2 * h + 1) * LANES:(2 * h + 2) * LANES]
        r = lax.rsqrt((jnp.sum(kn * kn, axis=1, keepdims=True) + ss_pe) * (1.0 / MLA_HEAD) + RMS_EPS)
        k_ref[:, 2 * h * LANES:(2 * h + 1) * LANES] = (kn * r * g_nope).astype(k_ref.dtype)
        pe = _rope_group(kpe * r * g_pe, tabs, MLA_ROPE // 2)
        k_ref[:, (2 * h + 1) * LANES:(2 * h + 2) * LANES] = pe.astype(k_ref.dtype)
        v_ref[:, h * LANES:(h + 1) * LANES] = v.astype(v_ref.dtype)


def _row_tab_aux(tabs, tm):
    return [(t, (tm, LANES), lambda n, m: (m, 0)) for t in tabs]


def _proj(a, w, layer, col_off, width, *, tn, out_dtype=BF16, gain=None, tabs=None, gw=LANES, real=LANES,
          half=0, rope_groups=lambda g: True, scale=1.0, epi=None, tm=TM, a_blk=0):
    s = a.shape[0]
    tm = min(tm, s)
    assert col_off % tn == 0 and width % tn == 0
    aux = []
    if gain is not None:
        aux.append((gain.reshape(1, width).astype(F32), (1, tn), lambda n, m: (0, n)))
    if tabs is not None:
        aux.extend(_row_tab_aux(tabs, tm))
    if epi is None:
        epi = functools.partial(_epi_heads, gw=gw, real=real, half=half, rope_groups=rope_groups, scale=scale,
                                use_gain=gain is not None, use_rope=tabs is not None)
    outs = [((s, width), out_dtype, (tm, tn), lambda n, m: (m, n))]
    return _mm(a, [(w, layer, col_off // tn)], aux, outs, epi, tm=tm, tn=tn, n_tiles=width // tn, a_blk=a_blk,
               name="proj")[0]


def _residual_mm(a, w, layer, x, gate, *, tn=TN, tm=TM):
    s, d = x.shape
    tm = min(tm, s)
    aux = [(x, (tm, tn), lambda n, m: (m, n)), (gate.reshape(1, d), (1, tn), lambda n, m: (0, n))]
    outs = [((s, d), F32, (tm, tn), lambda n, m: (m, n))]
    return _mm(a, [(w, layer, 0)], aux, outs, _epi_residual, tm=tm, tn=tn, n_tiles=d // tn, name="residual_mm")[0]


def _ada_body(c_ref, w_ref, b_ref, o_ref):
    c = c_ref[...]
    cond = c * _sigmoid(c)
    o_ref[...] = jnp.dot(cond.astype(BF16), w_ref[...].astype(BF16), preferred_element_type=F32) + b_ref[...]


def _ada_mod(c, ada_w, ada_b):
    depth, d, n = ada_w.shape
    tn = 1536
    c8 = jnp.broadcast_to(c, (8, d))
    out = pl.pallas_call(
        _ada_body,
        grid=(depth, n // tn),
        in_specs=[pl.BlockSpec((8, d), lambda l, j: (0, 0)),
                  pl.BlockSpec((None, d, tn), lambda l, j: (l, 0, j)),
                  pl.BlockSpec((None, 1, tn), lambda l, j: (l, 0, j))],
        out_specs=pl.BlockSpec((None, 8, tn), lambda l, j: (l, 0, j)),
        out_shape=jax.ShapeDtypeStruct((depth, 8, n), F32),
        compiler_params=_cparams("arbitrary", "arbitrary"),
        name="ada_mod",
    )(c8, ada_w, ada_b.reshape(depth, 1, n))
    return out[:, 0, :]


def _normmod_body(x_ref, g_ref, sc_ref, sh_ref, o_ref):
    x = x_ref[...]
    y = x * lax.rsqrt(jnp.mean(x * x, axis=1, keepdims=True) + RMS_EPS) * g_ref[...]
    o_ref[...] = (y * (1.0 + sc_ref[...]) + sh_ref[...]).astype(o_ref.dtype)


def _normmod(x, g, sc, sh, tm=256):
    s, d = x.shape
    tm = min(tm, s)
    vec = pl.BlockSpec((1, d), lambda m: (0, 0))
    return pl.pallas_call(
        _normmod_body,
        grid=(s // tm,),
        in_specs=[pl.BlockSpec((tm, d), lambda m: (m, 0)), vec, vec, vec],
        out_specs=pl.BlockSpec((tm, d), lambda m: (m, 0)),
        out_shape=jax.ShapeDtypeStruct((s, d), BF16),
        compiler_params=_cparams("arbitrary"),
        name="normmod",
    )(x, g.reshape(1, d), sc.reshape(1, d), sh.reshape(1, d))


def _softmax_scratch(n_maps, tq, tk):
    return [pltpu.VMEM((n_maps, tq, 2 * HEAD_DIM), F32), pltpu.VMEM((n_maps, tq, LANES), F32),
            pltpu.VMEM((n_maps, 2, tq, tk), F32), pltpu.VMEM((n_maps, 2, 2, tq, LANES), F32)]


def _softmax_pipeline(qs, k_ref, v_ref, scratch, *, tq, tk, first_chunk, first_keep, n_rest, bias_fn):
    acc_ref, m_ref, s_ref, st_ref = scratch
    ones = jnp.ones((tk, LANES), BF16)
    n_grp = tk // LANES
    maps = range(len(qs))

    def chunk_start(c):
        return pl.multiple_of(c * tk, tk)

    def stage1(c, slot, keep):
        start = chunk_start(c)
        k = k_ref[pl.ds(start, tk), :]
        for t in maps:
            s = _dot_nt(qs[t], k)
            if bias_fn is not None:
                s = s + bias_fn(start)
            if keep is not None:
                s = jnp.where(keep, s, NEG)
            m_run = m_ref[t]
            m_new = jnp.maximum(m_run, jnp.max(s, axis=1, keepdims=True))
            s_ref[t, slot] = s
            st_ref[t, slot, 0] = m_new
            st_ref[t, slot, 1] = jnp.exp2(m_run - m_new)
            m_ref[t] = m_new

    def stage2(c, slot):
        v1 = jnp.concatenate([v_ref[pl.ds(chunk_start(c), tk), :], ones], axis=1)
        for t in maps:
            m_new = st_ref[t, slot, 0]
            alpha = st_ref[t, slot, 1]
            p = jnp.concatenate([jnp.exp2(s_ref[t, slot, :, g * LANES:(g + 1) * LANES] - m_new)
                                 for g in range(n_grp)], axis=1)
            pv = jnp.dot(p.astype(BF16), v1, preferred_element_type=F32)
            acc_ref[t] = jnp.concatenate([alpha, alpha], axis=1) * acc_ref[t] + pv

    acc_ref[...] = jnp.zeros(acc_ref.shape, F32)
    m_ref[...] = jnp.full(m_ref.shape, NEG, F32)
    stage1(first_chunk, 0, first_keep)

    def body(t2, carry):
        c0 = 2 * t2
        stage1(c0, 1, None)
        stage2(jnp.where(t2 == 0, first_chunk, c0 - 1), 0)
        stage1(c0 + 1, 0, None)
        stage2(c0, 1)
        return carry

    n_pairs = n_rest // 2
    lax.fori_loop(0, n_pairs, body, 0)
    pending = jnp.where(n_pairs == 0, first_chunk, 2 * n_pairs - 1)

    @pl.when(n_rest % 2 == 1)
    def _odd_tail():
        stage1(n_rest - 1, 1, None)
        stage2(pending, 0)
        stage2(n_rest - 1, 1)

    @pl.when(n_rest % 2 == 0)
    def _even_tail():
        stage2(pending, 0)


def _causal_keep(tq, tk, col_shift):
    row = lax.broadcasted_iota(jnp.int32, (tq, 1), 0)
    col = lax.broadcasted_iota(jnp.int32, (1, tk), 1)
    return row >= col + col_shift


def _dense_attn_body(*refs, mode, tq, tk, lam_init):
    if mode == "mla":
        q_ref, k_ref, v_ref, o_ref = refs[:4]
        scratch = refs[4:]
    elif mode == "fox":
        q_ref, k_ref, v_ref, cum_ref, cumt_ref, gate_ref, o_ref = refs[:7]
        scratch = refs[7:]
    else:
        q_ref, k_ref, v_ref, lam_ref, subg_ref, o_ref = refs[:6]
        scratch = refs[6:]
    h = pl.program_id(0)
    i = pl.program_id(1)
    acc_ref = scratch[0]

    q = q_ref[...]
    if mode == "diff":
        lane = lax.broadcasted_iota(jnp.int32, (1, LANES), 1)
        zero = jnp.zeros_like(q)
        qs = [jnp.where(lane < DIFF_DIM, q, zero), jnp.where(lane >= DIFF_DIM, q, zero)]
    else:
        qs = [q]
    if mode == "fox":
        lane = lax.broadcasted_iota(jnp.int32, (1, LANES), 1)
        cq = jnp.sum(jnp.where(lane == h, cum_ref[...], 0.0), axis=1, keepdims=True) * LOG2E

        def bias_fn(start):
            return cq - cumt_ref[:, pl.ds(start, tk)] * LOG2E
    else:
        bias_fn = None

    _softmax_pipeline(qs, k_ref, v_ref, scratch, tq=tq, tk=tk, first_chunk=i, first_keep=_causal_keep(tq, tk, 0),
                      n_rest=i, bias_fn=bias_fn)

    def normalised(t):
        acc = acc_ref[t]
        return acc[:, :HEAD_DIM] / acc[:, HEAD_DIM:]

    if mode == "diff":
        lp = lam_ref[...]
        lam = (jnp.exp(jnp.sum(lp[0:1] * lp[1:2], axis=1, keepdims=True))
               - jnp.exp(jnp.sum(lp[2:3] * lp[3:4], axis=1, keepdims=True)) + lam_init)
        o = normalised(0) - lam * normalised(1)
        o = o * lax.rsqrt(jnp.mean(o * o, axis=1, keepdims=True) + RMS_EPS) * subg_ref[...]
        o = o * (1.0 - lam_init)
    else:
        o = normalised(0)
        if mode == "fox":
            o = o * gate_ref[...]
    o_ref[...] = o.astype(o_ref.dtype)


def _dense_attention(q, k, v, *, mode, dq, extra=(), lam_init=0.0, tq=512):
    s = q.shape[0]
    tq = min(tq, s)
    tk = tq
    in_specs = [pl.BlockSpec((tq, dq), lambda h, i: (i, h)),
                pl.BlockSpec((s, dq), lambda h, i: (0, h)),
                pl.BlockSpec((s, HEAD_DIM), lambda h, i: (0, h))]
    operands = [q, k, v]
    if mode == "fox":
        cum, cumt, gate = extra
        in_specs += [pl.BlockSpec((tq, LANES), lambda h, i: (i, 0)),
                     pl.BlockSpec((None, 1, s), lambda h, i: (h, 0, 0)),
                     pl.BlockSpec((tq, HEAD_DIM), lambda h, i: (i, h))]
        operands += [cum, cumt, gate]
    elif mode == "diff":
        lam_params, subln_g = extra
        in_specs += [pl.BlockSpec((4, DIFF_DIM), lambda h, i: (0, 0)),
                     pl.BlockSpec((1, HEAD_DIM), lambda h, i: (0, 0))]
        operands += [lam_params, subln_g]
    n_maps = 2 if mode == "diff" else 1
    body = functools.partial(_dense_attn_body, mode=mode, tq=tq, tk=tk, lam_init=lam_init)
    return pl.pallas_call(
        body,
        grid=(N_HEADS, s // tq),
        in_specs=in_specs,
        out_specs=pl.BlockSpec((tq, HEAD_DIM), lambda h, i: (i, h)),
        out_shape=jax.ShapeDtypeStruct((s, N_HEADS * HEAD_DIM), BF16),
        scratch_shapes=_softmax_scratch(n_maps, tq, tk),
        compiler_params=_cparams("arbitrary", "arbitrary"),
        name="attn_" + mode,
    )(*operands)


def _dsa_attn_body(q_ref, k_ref, v_ref, bias_ref, o_ref, *scratch, tq, tk):
    i = pl.program_id(0)

    def bias_fn(start):
        return bias_ref[:, pl.ds(start, tk)].astype(F32)

    _softmax_pipeline([q_ref[...]], k_ref, v_ref, scratch, tq=tq, tk=tk, first_chunk=i, first_keep=None,
                      n_rest=i, bias_fn=bias_fn)
    acc = scratch[0][0]
    o_ref[...] = (acc[:, :HEAD_DIM] / acc[:, HEAD_DIM:]).astype(o_ref.dtype)


def _dsa_attention(q, k, v, bias, *, tq=512):
    s = q.shape[0]
    tq = min(tq, s)
    tk = tq
    body = functools.partial(_dsa_attn_body, tq=tq, tk=tk)
    return pl.pallas_call(
        body,
        grid=(s // tq, N_HEADS),
        in_specs=[pl.BlockSpec((tq, HEAD_DIM), lambda i, h: (i, h)),
                  pl.BlockSpec((s, HEAD_DIM), lambda i, h: (0, h)),
                  pl.BlockSpec((s, HEAD_DIM), lambda i, h: (0, h)),
                  pl.BlockSpec((tq, s), lambda i, h: (i, 0))],
        out_specs=pl.BlockSpec((tq, HEAD_DIM), lambda i, h: (i, h)),
        out_shape=jax.ShapeDtypeStruct((s, N_HEADS * HEAD_DIM), BF16),
        scratch_shapes=_softmax_scratch(1, tq, tk),
        compiler_params=_cparams("arbitrary", "arbitrary"),
        name="attn_dsa",
    )(q, k, v, bias)


def _sortable_key(score):
    bits = lax.bitcast_convert_type(score, jnp.int32)
    return bits ^ ((bits >> 31) & 0x7FFFFFFF)


def _indexer_body(iq_ref, ik_ref, iw_ref, bias_ref, keys_ref, qz_ref, wb_ref, *, tq, n_sel):
    i = pl.program_id(0)
    s_len = ik_ref.shape[0]
    n_chunks = s_len // tq
    lane = lax.broadcasted_iota(jnp.int32, (1, LANES), 1)
    iw = iw_ref[...]
    for p in range(IDX_HEADS // 2):
        pair = iq_ref[:, p * LANES:(p + 1) * LANES]
        zero = jnp.zeros_like(pair)
        qz_ref[2 * p] = jnp.where(lane < IDX_DIM, pair, zero)
        qz_ref[2 * p + 1] = jnp.where(lane >= IDX_DIM, pair, zero)
    for hh in range(IDX_HEADS):
        wb_ref[hh] = jnp.broadcast_to(iw[:, hh:hh + 1], (tq, LANES))

    def score_chunk(start):
        ik = ik_ref[pl.ds(start, tq), :]
        sc = jnp.zeros((tq, tq), F32)
        for hh in range(IDX_HEADS):
            rel = jnp.maximum(_dot_nt(qz_ref[hh], ik), 0.0)
            sc = sc + jnp.tile(wb_ref[hh], (1, tq // LANES)) * rel
        return sc

    def full_body(j, carry):
        start = pl.multiple_of(j * tq, tq)
        keys_ref[:, pl.ds(start, tq)] = _sortable_key(score_chunk(start))
        return carry

    lax.fori_loop(0, i, full_body, 0)
    diag = pl.multiple_of(i * tq, tq)
    sc = jnp.where(_causal_keep(tq, tq, 0), score_chunk(diag), -jnp.inf)
    keys_ref[:, pl.ds(diag, tq)] = _sortable_key(sc)

    def bit_body(b, cand):
        trial = cand | lax.shift_left(jnp.int32(1), 31 - b)
        trial_signed = trial ^ INT_MIN

        def count_body(j, cnt):
            kk = keys_ref[:, pl.ds(pl.multiple_of(j * tq, tq), tq)]
            ge = (kk >= trial_signed).astype(jnp.int32)
            for g in range(tq // LANES):
                cnt = cnt + ge[:, g * LANES:(g + 1) * LANES]
            return cnt

        cnt = lax.fori_loop(0, i + 1, count_body, jnp.zeros((tq, LANES), jnp.int32))
        total = jnp.sum(cnt, axis=1, keepdims=True)
        return jnp.where(total >= n_sel, trial, cand)

    cand = lax.fori_loop(0, 32, bit_body, jnp.zeros((tq, 1), jnp.int32))
    thr = jnp.maximum(cand ^ INT_MIN, KEY_NEG_INF + 1)

    def write_body(j, carry):
        start = pl.multiple_of(j * tq, tq)
        kk = keys_ref[:, pl.ds(start, tq)]
        bias_ref[:, pl.ds(start, tq)] = jnp.where(kk >= thr, 0.0, NEG).astype(bias_ref.dtype)
        return carry

    lax.fori_loop(0, i + 1, write_body, 0)

    def fill_body(j, carry):
        bias_ref[:, pl.ds(pl.multiple_of(j * tq, tq), tq)] = jnp.full((tq, tq), NEG, bias_ref.dtype)
        return carry

    lax.fori_loop(i + 1, n_chunks, fill_body, 0)


def _dsa_select_bias(iq, ik2, iw, n_sel, tq=256):
    s = iq.shape[0]
    tq = min(tq, s)
    body = functools.partial(_indexer_body, tq=tq, n_sel=n_sel)
    return pl.pallas_call(
        body,
        grid=(s // tq,),
        in_specs=[pl.BlockSpec((tq, IDX_HEADS * IDX_DIM), lambda i: (i, 0)),
                  pl.BlockSpec((s, LANES), lambda i: (0, 0)),
                  pl.BlockSpec((tq, LANES), lambda i: (i, 0))],
        out_specs=pl.BlockSpec((tq, s), lambda i: (i, 0)),
        out_shape=jax.ShapeDtypeStruct((s, s), BF16),
        scratch_shapes=[pltpu.VMEM((tq, s), jnp.int32),
                        pltpu.VMEM((IDX_HEADS, tq, LANES), BF16),
                        pltpu.VMEM((IDX_HEADS, tq, LANES), F32)],
        compiler_params=_cparams("arbitrary"),
        name="dsa_select",
    )(iq, ik2, iw)


def _fox_cum_body(f_ref, b_ref, o_ref, carry_ref, *, tc):
    @pl.when(pl.program_id(0) == 0)
    def _():
        carry_ref[...] = jnp.zeros(carry_ref.shape, F32)

    z = f_ref[...] + b_ref[...]
    lf = jnp.minimum(z, 0.0) - jnp.log(1.0 + jnp.exp(-jnp.abs(z)))
    row = lax.broadcasted_iota(jnp.int32, (tc, tc), 0)
    col = lax.broadcasted_iota(jnp.int32, (tc, tc), 1)
    tri = (row >= col).astype(BF16)
    hi = lf.astype(BF16)
    r1 = lf - hi.astype(F32)
    mid = r1.astype(BF16)
    lo = (r1 - mid.astype(F32)).astype(BF16)
    cs = (jnp.dot(tri, hi, preferred_element_type=F32) + jnp.dot(tri, mid, preferred_element_type=F32)
          + jnp.dot(tri, lo, preferred_element_type=F32)) + carry_ref[...]
    o_ref[...] = cs
    carry_ref[...] = cs[tc - 1:tc, :]


def _fox_cumsum(f_raw, b_pad, tc=512):
    s = f_raw.shape[0]
    tc = min(tc, s)
    return pl.pallas_call(
        functools.partial(_fox_cum_body, tc=tc),
        grid=(s // tc,),
        in_specs=[pl.BlockSpec((tc, LANES), lambda m: (m, 0)), pl.BlockSpec((1, LANES), lambda m: (0, 0))],
        out_specs=pl.BlockSpec((tc, LANES), lambda m: (m, 0)),
        out_shape=jax.ShapeDtypeStruct((s, LANES), F32),
        scratch_shapes=[pltpu.VMEM((1, LANES), F32)],
        compiler_params=_cparams("arbitrary"),
        name="fox_cumsum",
    )(f_raw, b_pad)


def _rope_tables(positions, rot_dim, period):
    half = rot_dim // 2
    inv_freq = ROPE_THETA ** (-jnp.arange(0, rot_dim, 2, dtype=F32) / rot_dim)
    ang = positions.astype(F32)[:, None] * inv_freq
    cos, sin = jnp.cos(ang), jnp.sin(ang)
    lp = jnp.arange(LANES) % period
    in_x1 = lp < half
    in_x2 = (lp >= half) & (lp < 2 * half)
    idx = jnp.where(in_x1, lp, jnp.clip(lp - half, 0, half - 1))
    cos_l, sin_l = cos[:, idx], sin[:, idx]
    c = jnp.where(in_x1 | in_x2, cos_l, 1.0)
    s1 = jnp.where(in_x2, sin_l, 0.0)
    s2 = jnp.where(in_x1, -sin_l, 0.0)
    return c, s1, s2


def _pad_cols(w, width):
    return jnp.pad(w, ((0, 0), (0, width - w.shape[1])))


def _mla_mixer(h, tabs, w_in, q_a_g, kv_a_g, w_q_b, w_kv_b, q_g, k_g):
    s = h.shape[0]
    nope = HEAD_DIM
    scale = MLA_HEAD ** -0.5 * LOG2E
    gains = jnp.concatenate([q_a_g, kv_a_g])
    c_norm = _proj(h, w_in, 0, 0, 2 * MLA_LORA, tn=MLA_LORA, gain=gains, gw=MLA_LORA, real=MLA_LORA)
    kpe = _proj(h, _pad_cols(w_in[0][:, 2 * MLA_LORA:], LANES)[None], 0, 0, LANES, tn=LANES, out_dtype=F32,
                epi=_epi_plain)
    wq = w_q_b[0].reshape(MLA_LORA, N_HEADS, MLA_HEAD)
    wq = jnp.concatenate([wq[:, :, MLA_ROPE:], wq[:, :, :MLA_ROPE],
                          jnp.zeros((MLA_LORA, N_HEADS, 2 * LANES - MLA_HEAD), F32)], axis=2)
    wq = wq.reshape(1, MLA_LORA, N_HEADS * 2 * LANES)
    qg = jnp.concatenate([q_g[MLA_ROPE:], q_g[:MLA_ROPE], jnp.zeros((2 * LANES - MLA_HEAD,), F32)])
    q = _proj(c_norm, wq, 0, 0, N_HEADS * 2 * LANES, tn=512, gain=jnp.tile(qg, N_HEADS), tabs=tabs,
              gw=2 * LANES, real=MLA_HEAD, half=MLA_ROPE // 2, rope_groups=lambda g: g % 2 == 1, scale=scale)
    tm, tn = min(TM, s), TN
    aux = [(kpe, (tm, LANES), lambda n, m: (m, 0)),
           (k_g[MLA_ROPE:].reshape(1, nope), (1, nope), lambda n, m: (0, 0)),
           (jnp.pad(k_g[:MLA_ROPE], (0, LANES - MLA_ROPE)).reshape(1, LANES), (1, LANES), lambda n, m: (0, 0))]
    aux += _row_tab_aux(tabs, tm)
    outs = [((s, N_HEADS * 2 * LANES), BF16, (tm, tn), lambda n, m: (m, n)),
            ((s, N_HEADS * HEAD_DIM), BF16, (tm, tn // 2), lambda n, m: (m, n))]
    k, v = _mm(c_norm, [(w_kv_b, 0, 0)], aux, outs, _epi_mla_kv, tm=tm, tn=tn,
               n_tiles=N_HEADS * 2 * LANES // tn, a_blk=1, name="mla_kv")
    return _dense_attention(q, k, v, mode="mla", dq=2 * LANES)


def _dsa_mixer(h, tabs_head, tabs_idx, w_in, q_g, k_g, idx_k_g):
    s = h.shape[0]
    hd = N_HEADS * HEAD_DIM
    ih = IDX_HEADS * IDX_DIM
    half = HEAD_DIM // 8
    q = _proj(h, w_in, 0, 0, hd, tn=512, gain=jnp.tile(q_g, N_HEADS), tabs=tabs_head, half=half,
              scale=HEAD_DIM ** -0.5 * LOG2E)
    k = _proj(h, w_in, 0, hd, hd, tn=512, gain=jnp.tile(k_g, N_HEADS), tabs=tabs_head, half=half)
    v = _proj(h, w_in, 0, 2 * hd, hd, tn=512, epi=_epi_plain)
    iq = _proj(h, w_in, 0, 3 * hd, ih, tn=512, tabs=tabs_idx, half=IDX_DIM // 8)
    w_ik = w_in[0][:, 3 * hd + ih:3 * hd + ih + IDX_DIM]
    ik2 = _proj(h, jnp.concatenate([w_ik, w_ik], axis=1)[None], 0, 0, LANES, tn=LANES,
                gain=jnp.tile(idx_k_g, 2), tabs=tabs_idx, half=IDX_DIM // 8)
    w_iw = _pad_cols(w_in[0][:, 3 * hd + ih + IDX_DIM:], LANES)[None]
    iw = _proj(h, w_iw, 0, 0, LANES, tn=LANES, out_dtype=F32,
               epi=functools.partial(_epi_heads, gw=LANES, real=LANES, half=0, rope_groups=None,
                                     scale=IDX_HEADS ** -0.5 * IDX_DIM ** -0.5, use_gain=False, use_rope=False))
    bias = _dsa_select_bias(iq, ik2, iw, min(IDX_TOPK, s // 4))
    return _dsa_attention(q, k, v, bias)


def _diff_mixer(h, tabs, layer_idx, w_in, q_g, k_g, lam_params, subln_g):
    w = N_HEADS * 2 * DIFF_DIM
    half = DIFF_DIM // 8
    q = _proj(h, w_in, 0, 0, w, tn=512, gain=jnp.tile(q_g, 2 * N_HEADS), tabs=tabs, gw=DIFF_DIM, real=DIFF_DIM,
              half=half, scale=DIFF_DIM ** -0.5 * LOG2E)
    k = _proj(h, w_in, 0, w, w, tn=512, gain=jnp.tile(k_g, 2 * N_HEADS), tabs=tabs, gw=DIFF_DIM, real=DIFF_DIM,
              half=half)
    v = _proj(h, w_in, 0, 2 * w, w, tn=512, epi=_epi_plain)
    lam_init = 0.8 - 0.6 * math.exp(-0.3 * layer_idx)
    return _dense_attention(q, k, v, mode="diff", dq=HEAD_DIM, extra=(lam_params, subln_g.reshape(1, HEAD_DIM)),
                            lam_init=lam_init)


def _fox_mixer(h, w_in, b_f, q_g, k_g):
    s = h.shape[0]
    hd = N_HEADS * HEAD_DIM
    q = _proj(h, w_in, 0, 0, hd, tn=512, gain=jnp.tile(q_g, N_HEADS), scale=HEAD_DIM ** -0.5 * LOG2E)
    k = _proj(h, w_in, 0, hd, hd, tn=512, gain=jnp.tile(k_g, N_HEADS))
    v = _proj(h, w_in, 0, 2 * hd, hd, tn=512, epi=_epi_plain)
    w_f = _pad_cols(w_in[0][:, 3 * hd:3 * hd + N_HEADS], LANES)[None]
    f_raw = _proj(h, w_f, 0, 0, LANES, tn=LANES, out_dtype=F32, epi=_epi_plain)
    gate = _proj(h, w_in[:, :, 3 * hd + N_HEADS:], 0, 0, hd, tn=512, out_dtype=F32, epi=_epi_sigmoid)
    cum = _fox_cumsum(f_raw, jnp.pad(b_f, (0, LANES - N_HEADS)).reshape(1, LANES))
    cumt = cum[:, :N_HEADS].T.reshape(N_HEADS, 1, s)
    return _dense_attention(q, k, v, mode="fox", dq=HEAD_DIM, extra=(cum, cumt, gate))


def _swiglu_ffn(h, x, gate, w_gate_up, w_down, layer):
    s = h.shape[0]
    tm, tn = min(TM, s), TN
    outs = [((s, FFN_HIDDEN), BF16, (tm, tn), lambda n, m: (m, n))]
    act = _mm(h, [(w_gate_up, layer, 0), (w_gate_up, layer, FFN_HIDDEN // tn)], [], outs, _epi_swiglu,
              tm=tm, tn=tn, n_tiles=FFN_HIDDEN // tn, name="ffn_gate_up")[0]
    return _residual_mm(act, w_down, layer, x, gate, tm=TM_DEEP)


def kernel(x, c, positions, ln_mix_g, ln_ffn_g, ada_w, ada_b, ffn_w_gate_up, ffn_w_down, mla_w_in, mla_q_a_g, mla_kv_a_g, mla_w_q_b, mla_w_kv_b, mla_q_g, mla_k_g, mla_w_out, dsa_w_in, dsa_q_g, dsa_k_g, dsa_idx_k_g, dsa_w_out, diff_w_in, diff_q_g, diff_k_g, diff_lambda_q1, diff_lambda_k1, diff_lambda_q2, diff_lambda_k2, diff_subln_g, diff_w_out, fox_w_in, fox_b_f, fox_q_g, fox_k_g, fox_w_out):
    batch, s, d = x.shape
    assert batch == 1 and d == D_MODEL
    depth = ada_w.shape[0]
    pos = positions[0]
    tabs_head = _rope_tables(pos, HEAD_DIM // 4, LANES)
    tabs_small = _rope_tables(pos, IDX_DIM // 4, IDX_DIM)
    tabs_mla = _rope_tables(pos, MLA_ROPE, LANES)
    mod = _ada_mod(c, ada_w, ada_b)
    xs = x[0]
    for i in range(depth):
        sh1, sc1, g1, sh2, sc2, g2 = [mod[i, t * d:(t + 1) * d] for t in range(6)]
        h = _normmod(xs, ln_mix_g[i], sc1, sh1)
        kind, j = i % 4, i // 4
        if kind == 0:
            o = _mla_mixer(h, tabs_mla, mla_w_in[j:j + 1], mla_q_a_g[j], mla_kv_a_g[j], mla_w_q_b[j:j + 1],
                           mla_w_kv_b[j:j + 1], mla_q_g[j], mla_k_g[j])
            w_out = mla_w_out
        elif kind == 1:
            o = _dsa_mixer(h, tabs_head, tabs_small, dsa_w_in[j:j + 1], dsa_q_g[j], dsa_k_g[j], dsa_idx_k_g[j])
            w_out = dsa_w_out
        elif kind == 2:
            lam_params = jnp.stack([diff_lambda_q1[j], diff_lambda_k1[j], diff_lambda_q2[j], diff_lambda_k2[j]])
            o = _diff_mixer(h, tabs_small, i, diff_w_in[j:j + 1], diff_q_g[j], diff_k_g[j], lam_params,
                            diff_subln_g[j])
            w_out = diff_w_out
        else:
            o = _fox_mixer(h, fox_w_in[j:j + 1], fox_b_f[j], fox_q_g[j], fox_k_g[j])
            w_out = fox_w_out
        xs = _residual_mm(o, w_out, j, xs, g1)
        h = _normmod(xs, ln_ffn_g[i], sc2, sh2)
        xs = _swiglu_ffn(h, xs, g2, ffn_w_gate_up, ffn_w_down, i)
    return xs[None]
```

```python
import functools
import math

import jax
import jax.numpy as jnp
from jax import lax
from jax.experimental import pallas as pl
from jax.experimental.pallas import tpu as pltpu

F32 = jnp.float32
BF16 = jnp.bfloat16

D_MODEL = 2048
N_HEADS = 16
HEAD_DIM = 128
ROPE_THETA = 500000.0
RMS_EPS = 1e-6
FFN_HIDDEN = 5632
MLA_LORA = 512
MLA_ROPE = 64
MLA_HEAD = 192
IDX_HEADS = 16
IDX_DIM = 64
IDX_TOPK = 256
DIFF_DIM = 64

LANES = 128
LOG2E = 1.4426950408889634
NEG = -1e30
VMEM_LIMIT_BYTES = 56 * 1024 * 1024
TM = 1024
TM_DEEP = 512
SUB_ROWS = 256
TN = 512
HEADS_PER_STEP = 2
INT_MIN = -2 ** 31
KEY_NEG_INF = (0xFF800000 ^ 0x7FFFFFFF) - 2 ** 32


def _cparams(*sem):
    return pltpu.CompilerParams(dimension_semantics=sem, vmem_limit_bytes=VMEM_LIMIT_BYTES)


def _dot_nt(a, b):
    return lax.dot_general(a, b, (((1,), (1,)), ((), ())), preferred_element_type=F32)


def _sigmoid(z):
    return 1.0 / (1.0 + jnp.exp(-z))


def _mm_body(*refs, n_w, n_aux, n_out, epi, sub):
    a_ref = refs[0]
    w_refs = refs[1:1 + n_w]
    aux_refs = refs[1 + n_w:1 + n_w + n_aux]
    out_refs = refs[1 + n_w + n_aux:1 + n_w + n_aux + n_out]
    wb_refs = refs[1 + n_w + n_aux + n_out:]
    tm = a_ref.shape[0]

    @pl.when(pl.program_id(1) == 0)
    def _cast_weights():
        for w_ref, wb_ref in zip(w_refs, wb_refs):
            wb_ref[...] = w_ref[...].astype(BF16)

    for r in range(tm // sub):
        rows = pl.ds(r * sub, sub)
        accs = [jnp.dot(a_ref[rows, :], wb_ref[...], preferred_element_type=F32) for wb_ref in wb_refs]
        epi(accs, [ref.at[rows] if ref.shape[0] == tm else ref for ref in aux_refs],
            [ref.at[rows] for ref in out_refs])


def _mm(a, ws, aux, outs, epi, *, tm, tn, n_tiles, a_blk=0, name="mm", sub=SUB_ROWS):
    s = a.shape[0]
    k = ws[0][0].shape[1]
    tm = min(tm, s)
    in_specs = [pl.BlockSpec((tm, k), lambda n, m: (m, a_blk))]
    operands = [a]
    for w, layer, off in ws:
        assert w.shape[1] == k
        in_specs.append(pl.BlockSpec((None, k, tn), lambda n, m, layer=layer, off=off: (layer, 0, n + off)))
        operands.append(w)
    for arr, block, imap in aux:
        in_specs.append(pl.BlockSpec(block, imap))
        operands.append(arr)
    out_shape = [jax.ShapeDtypeStruct(shape, dtype) for shape, dtype, _, _ in outs]
    out_specs = [pl.BlockSpec(block, imap) for _, _, block, imap in outs]
    body = functools.partial(_mm_body, n_w=len(ws), n_aux=len(aux), n_out=len(outs), epi=epi, sub=min(sub, tm))
    res = pl.pallas_call(
        body,
        grid=(n_tiles, s // tm),
        in_specs=in_specs,
        out_specs=out_specs,
        out_shape=out_shape,
        scratch_shapes=[pltpu.VMEM((k, tn), BF16) for _ in ws],
        compiler_params=_cparams("arbitrary", "arbitrary"),
        name=name,
    )(*operands)
    return res


def _rope_group(y, tabs, half):
    c, s1, s2 = tabs
    return y * c + pltpu.roll(y, half, 1) * s1 + pltpu.roll(y, LANES - half, 1) * s2


def _group_inv_rms(y, gw, real):
    tn = y.shape[1]
    sq = y * y
    n_groups = tn // LANES
    if gw == DIFF_DIM:
        lane = lax.broadcasted_iota(jnp.int32, (1, LANES), 1)
        lo = lane < DIFF_DIM
        out = []
        for g in range(n_groups):
            sg = sq[:, g * LANES:(g + 1) * LANES]
            s_lo = jnp.sum(jnp.where(lo, sg, 0.0), axis=1, keepdims=True)
            s_hi = jnp.sum(jnp.where(lo, 0.0, sg), axis=1, keepdims=True)
            out.append(lax.rsqrt(jnp.where(lo, s_lo, s_hi) * (1.0 / real) + RMS_EPS))
        return out
    sums = [jnp.sum(sq[:, g * LANES:(g + 1) * LANES], axis=1, keepdims=True) for g in range(n_groups)]
    per = gw // LANES
    out = []
    for h in range(n_groups // per):
        tot = sums[h * per]
        for t in range(1, per):
            tot = tot + sums[h * per + t]
        r = lax.rsqrt(tot * (1.0 / real) + RMS_EPS)
        out.extend([r] * per)
    return out


def _epi_heads(accs, aux_refs, out_refs, *, gw, real, half, rope_groups, scale, use_gain, use_rope):
    y = accs[0]
    tn = y.shape[1]
    idx = 0
    gain = None
    if use_gain:
        gain = aux_refs[idx][...]
        idx += 1
        inv = _group_inv_rms(y, gw, real)
    if use_rope:
        tabs = tuple(aux_refs[idx + t][...] for t in range(3))
    for g in range(tn // LANES):
        yg = y[:, g * LANES:(g + 1) * LANES]
        if use_gain:
            yg = yg * inv[g] * gain[:, g * LANES:(g + 1) * LANES]
        if use_rope and rope_groups(g):
            yg = _rope_group(yg, tabs, half)
        if scale != 1.0:
            yg = yg * scale
        out_refs[0][:, g * LANES:(g + 1) * LANES] = yg.astype(out_refs[0].dtype)


def _epi_plain(accs, aux_refs, out_refs):
    out_refs[0][...] = accs[0].astype(out_refs[0].dtype)


def _epi_sigmoid(accs, aux_refs, out_refs):
    out_refs[0][...] = _sigmoid(accs[0]).astype(out_refs[0].dtype)


def _epi_residual(accs, aux_refs, out_refs):
    x_ref, g_ref = aux_refs
    out_refs[0][...] = x_ref[...] + g_ref[...] * accs[0]


def _epi_swiglu(accs, aux_refs, out_refs):
    gate, up = accs
    out_refs[0][...] = (gate * _sigmoid(gate) * up).astype(out_refs[0].dtype)


def _epi_mla_kv(accs, aux_refs, out_refs):
    kpe_ref, g_nope_ref, g_pe_ref, c_ref, s1_ref, s2_ref = aux_refs
    k_ref, v_ref = out_refs
    y = accs[0]
    kpe = kpe_ref[...]
    ss_pe = jnp.sum(kpe * kpe, axis=1, keepdims=True)
    tabs = (c_ref[...], s1_ref[...], s2_ref[...])
    g_nope = g_nope_ref[...]
    g_pe = g_pe_ref[...]
    for h in range(y.shape[1] // (2 * LANES)):
        kn = y[:, 2 * h * LANES:(2 * h + 1) * LANES]
        v = y[:, (2 * h + 1) * LANES:(2 * h + 2) * LANES]
        r = lax.rsqrt((jnp.sum(kn * kn, axis=1, keepdims=True) + ss_pe) * (1.0 / MLA_HEAD) + RMS_EPS)
        k_ref[:, 2 * h * LANES:(2 * h + 1) * LANES] = (kn * r * g_nope).astype(k_ref.dtype)
        pe = _rope_group(kpe * r * g_pe, tabs, MLA_ROPE // 2)
        k_ref[:, (2 * h + 1) * LANES:(2 * h + 2) * LANES] = pe.astype(k_ref.dtype)
        v_ref[:, h * LANES:(h + 1) * LANES] = v.astype(v_ref.dtype)


def _row_tab_aux(tabs, tm):
    return [(t, (tm, LANES), lambda n, m: (m, 0)) for t in tabs]


def _proj(a, w, layer, col_off, width, *, tn, out_dtype=BF16, gain=None, tabs=None, gw=LANES, real=LANES,
          half=0, rope_groups=lambda g: True, scale=1.0, epi=None, tm=TM, a_blk=0):
    s = a.shape[0]
    tm = min(tm, s)
    assert col_off % tn == 0 and width % tn == 0
    aux = []
    if gain is not None:
        aux.append((gain.reshape(1, width).astype(F32), (1, tn), lambda n, m: (0, n)))
    if tabs is not None:
        aux.extend(_row_tab_aux(tabs, tm))
    if epi is None:
        epi = functools.partial(_epi_heads, gw=gw, real=real, half=half, rope_groups=rope_groups, scale=scale,
                                use_gain=gain is not None, use_rope=tabs is not None)
    outs = [((s, width), out_dtype, (tm, tn), lambda n, m: (m, n))]
    return _mm(a, [(w, layer, col_off // tn)], aux, outs, epi, tm=tm, tn=tn, n_tiles=width // tn, a_blk=a_blk,
               name="proj")[0]


def _residual_mm(a, w, layer, x, gate, *, tn=TN, tm=TM):
    s, d = x.shape
    tm = min(tm, s)
    aux = [(x, (tm, tn), lambda n, m: (m, n)), (gate.reshape(1, d), (1, tn), lambda n, m: (0, n))]
    outs = [((s, d), F32, (tm, tn), lambda n, m: (m, n))]
    return _mm(a, [(w, layer, 0)], aux, outs, _epi_residual, tm=tm, tn=tn, n_tiles=d // tn, name="residual_mm")[0]


def _ada_body(c_ref, w_ref, b_ref, o_ref):
    c = c_ref[...]
    cond = c * _sigmoid(c)
    o_ref[...] = jnp.dot(cond.astype(BF16), w_ref[...].astype(BF16), preferred_element_type=F32) + b_ref[...]


def _ada_mod(c, ada_w, ada_b):
    depth, d, n = ada_w.shape
    tn = 1536
    c8 = jnp.broadcast_to(c, (8, d))
    out = pl.pallas_call(
        _ada_body,
        grid=(depth, n // tn),
        in_specs=[pl.BlockSpec((8, d), lambda l, j: (0, 0)),
                  pl.BlockSpec((None, d, tn), lambda l, j: (l, 0, j)),
                  pl.BlockSpec((None, 1, tn), lambda l, j: (l, 0, j))],
        out_specs=pl.BlockSpec((None, 8, tn), lambda l, j: (l, 0, j)),
        out_shape=jax.ShapeDtypeStruct((depth, 8, n), F32),
        compiler_params=_cparams("arbitrary", "arbitrary"),
        name="ada_mod",
    )(c8, ada_w, ada_b.reshape(depth, 1, n))
    return out[:, 0, :]


def _normmod_body(x_ref, g_ref, sc_ref, sh_ref, o_ref):
    x = x_ref[...]
    y = x * lax.rsqrt(jnp.mean(x * x, axis=1, keepdims=True) + RMS_EPS) * g_ref[...]
    o_ref[...] = (y * (1.0 + sc_ref[...]) + sh_ref[...]).astype(o_ref.dtype)


def _normmod(x, g, sc, sh, tm=256):
    s, d = x.shape
    tm = min(tm, s)
    vec = pl.BlockSpec((1, d), lambda m: (0, 0))
    return pl.pallas_call(
        _normmod_body,
        grid=(s // tm,),
        in_specs=[pl.BlockSpec((tm, d), lambda m: (m, 0)), vec, vec, vec],
        out_specs=pl.BlockSpec((tm, d), lambda m: (m, 0)),
        out_shape=jax.ShapeDtypeStruct((s, d), BF16),
        compiler_params=_cparams("arbitrary"),
        name="normmod",
    )(x, g.reshape(1, d), sc.reshape(1, d), sh.reshape(1, d))


def _softmax_scratch(n_maps, tq, tk):
    return [pltpu.VMEM((n_maps, tq, 2 * HEAD_DIM), F32), pltpu.VMEM((n_maps, tq, LANES), F32),
            pltpu.VMEM((n_maps, 2, tq, tk), F32), pltpu.VMEM((n_maps, 2, 2, tq, LANES), F32)]


def _softmax_pipeline(qs, kv_cols, k_ref, v_ref, scratch, *, tq, tk, first_chunk, first_keep, n_rest, bias_fn):
    acc_ref, m_ref, s_ref, st_ref = scratch
    ones = jnp.ones((tk, LANES), BF16)
    n_grp = tk // LANES
    maps = range(len(qs))

    def chunk_start(c):
        return pl.multiple_of(c * tk, tk)

    def stage1(c, slot, keep):
        start = chunk_start(c)
        for t in maps:
            s = _dot_nt(qs[t], k_ref[pl.ds(start, tk), kv_cols[t][0]])
            if bias_fn is not None:
                s = s + bias_fn(t, start)
            if keep is not None:
                s = jnp.where(keep, s, NEG)
            m_run = m_ref[t]
            m_new = jnp.maximum(m_run, jnp.max(s, axis=1, keepdims=True))
            s_ref[t, slot] = s
            st_ref[t, slot, 0] = m_new
            st_ref[t, slot, 1] = jnp.exp2(m_run - m_new)
            m_ref[t] = m_new

    def stage2(c, slot):
        start = chunk_start(c)
        for t in maps:
            v1 = jnp.concatenate([v_ref[pl.ds(start, tk), kv_cols[t][1]], ones], axis=1)
            m_new = st_ref[t, slot, 0]
            alpha = st_ref[t, slot, 1]
            p = jnp.concatenate([jnp.exp2(s_ref[t, slot, :, g * LANES:(g + 1) * LANES] - m_new)
                                 for g in range(n_grp)], axis=1)
            pv = jnp.dot(p.astype(BF16), v1, preferred_element_type=F32)
            acc_ref[t] = jnp.concatenate([alpha, alpha], axis=1) * acc_ref[t] + pv

    acc_ref[...] = jnp.zeros(acc_ref.shape, F32)
    m_ref[...] = jnp.full(m_ref.shape, NEG, F32)
    stage1(first_chunk, 0, first_keep)

    def body(t2, carry):
        c0 = 2 * t2
        stage1(c0, 1, None)
        stage2(jnp.where(t2 == 0, first_chunk, c0 - 1), 0)
        stage1(c0 + 1, 0, None)
        stage2(c0, 1)
        return carry

    n_pairs = n_rest // 2
    lax.fori_loop(0, n_pairs, body, 0)
    pending = jnp.where(n_pairs == 0, first_chunk, 2 * n_pairs - 1)

    @pl.when(n_rest % 2 == 1)
    def _odd_tail():
        stage1(n_rest - 1, 1, None)
        stage2(pending, 0)
        stage2(n_rest - 1, 1)

    @pl.when(n_rest % 2 == 0)
    def _even_tail():
        stage2(pending, 0)


def _causal_keep(tq, tk, col_shift):
    row = lax.broadcasted_iota(jnp.int32, (tq, 1), 0)
    col = lax.broadcasted_iota(jnp.int32, (1, tk), 1)
    return row >= col + col_shift


def _dense_attn_body(*refs, mode, tq, tk, lam_init):
    if mode == "mla":
        q_ref, k_ref, v_ref, o_ref = refs[:4]
        scratch = refs[4:]
    elif mode == "fox":
        q_ref, k_ref, v_ref, cum_ref, cumt_ref, gate_ref, o_ref = refs[:7]
        scratch = refs[7:]
    else:
        q_ref, k_ref, v_ref, lam_ref, subg_ref, o_ref = refs[:6]
        scratch = refs[6:]
    hg = pl.program_id(0)
    i = pl.program_id(1)
    acc_ref = scratch[0]
    dq = q_ref.shape[1] // HEADS_PER_STEP
    heads = range(HEADS_PER_STEP)
    lane = lax.broadcasted_iota(jnp.int32, (1, LANES), 1)

    def head_cols(hh, width=HEAD_DIM):
        return slice(hh * width, (hh + 1) * width)

    qs, kv_cols = [], []
    for hh in heads:
        q = q_ref[:, head_cols(hh, dq)]
        if mode == "diff":
            zero = jnp.zeros_like(q)
            qs += [jnp.where(lane < DIFF_DIM, q, zero), jnp.where(lane >= DIFF_DIM, q, zero)]
            kv_cols += [(head_cols(hh, dq), head_cols(hh))] * 2
        else:
            qs.append(q)
            kv_cols.append((head_cols(hh, dq), head_cols(hh)))
    if mode == "fox":
        cum = cum_ref[...]
        cqs = [jnp.sum(jnp.where(lane == hg * HEADS_PER_STEP + hh, cum, 0.0), axis=1, keepdims=True) * LOG2E
               for hh in heads]

        def bias_fn(t, start):
            return cqs[t] - cumt_ref[t, :, pl.ds(start, tk)] * LOG2E
    else:
        bias_fn = None

    _softmax_pipeline(qs, kv_cols, k_ref, v_ref, scratch, tq=tq, tk=tk, first_chunk=i,
                      first_keep=_causal_keep(tq, tk, 0), n_rest=i, bias_fn=bias_fn)

    def normalised(t):
        acc = acc_ref[t]
        return acc[:, :HEAD_DIM] / acc[:, HEAD_DIM:]

    for hh in heads:
        if mode == "diff":
            lp = lam_ref[...]
            lam = (jnp.exp(jnp.sum(lp[0:1] * lp[1:2], axis=1, keepdims=True))
                   - jnp.exp(jnp.sum(lp[2:3] * lp[3:4], axis=1, keepdims=True)) + lam_init)
            o = normalised(2 * hh) - lam * normalised(2 * hh + 1)
            o = o * lax.rsqrt(jnp.mean(o * o, axis=1, keepdims=True) + RMS_EPS) * subg_ref[...]
            o = o * (1.0 - lam_init)
        else:
            o = normalised(hh)
            if mode == "fox":
                o = o * gate_ref[:, head_cols(hh)]
        o_ref[:, head_cols(hh)] = o.astype(o_ref.dtype)


def _dense_attention(q, k, v, *, mode, dq, extra=(), lam_init=0.0, tq=512):
    s = q.shape[0]
    tq = min(tq, s)
    tk = tq
    g = HEADS_PER_STEP
    in_specs = [pl.BlockSpec((tq, g * dq), lambda h, i: (i, h)),
                pl.BlockSpec((s, g * dq), lambda h, i: (0, h)),
                pl.BlockSpec((s, g * HEAD_DIM), lambda h, i: (0, h))]
    operands = [q, k, v]
    if mode == "fox":
        cum, cumt, gate = extra
        in_specs += [pl.BlockSpec((tq, LANES), lambda h, i: (i, 0)),
                     pl.BlockSpec((g, 1, s), lambda h, i: (h, 0, 0)),
                     pl.BlockSpec((tq, g * HEAD_DIM), lambda h, i: (i, h))]
        operands += [cum, cumt, gate]
    elif mode == "diff":
        lam_params, subln_g = extra
        in_specs += [pl.BlockSpec((4, DIFF_DIM), lambda h, i: (0, 0)),
                     pl.BlockSpec((1, HEAD_DIM), lambda h, i: (0, 0))]
        operands += [lam_params, subln_g]
    n_maps = g * (2 if mode == "diff" else 1)
    body = functools.partial(_dense_attn_body, mode=mode, tq=tq, tk=tk, lam_init=lam_init)
    return pl.pallas_call(
        body,
        grid=(N_HEADS // g, s // tq),
        in_specs=in_specs,
        out_specs=pl.BlockSpec((tq, g * HEAD_DIM), lambda h, i: (i, h)),
        out_shape=jax.ShapeDtypeStruct((s, N_HEADS * HEAD_DIM), BF16),
        scratch_shapes=_softmax_scratch(n_maps, tq, tk),
        compiler_params=_cparams("arbitrary", "arbitrary"),
        name="attn_" + mode,
    )(*operands)


def _dsa_attn_body(q_ref, k_ref, v_ref, bias_ref, o_ref, *scratch, tq, tk):
    i = pl.program_id(0)

    def bias_fn(t, start):
        return bias_ref[:, pl.ds(start, tk)].astype(F32)

    cols = [slice(hh * HEAD_DIM, (hh + 1) * HEAD_DIM) for hh in range(HEADS_PER_STEP)]
    _softmax_pipeline([q_ref[:, c] for c in cols], [(c, c) for c in cols], k_ref, v_ref, scratch, tq=tq, tk=tk,
                      first_chunk=i, first_keep=None, n_rest=i, bias_fn=bias_fn)
    for t, c in enumerate(cols):
        acc = scratch[0][t]
        o_ref[:, c] = (acc[:, :HEAD_DIM] / acc[:, HEAD_DIM:]).astype(o_ref.dtype)


def _dsa_attention(q, k, v, bias, *, tq=512):
    s = q.shape[0]
    tq = min(tq, s)
    tk = tq
    body = functools.partial(_dsa_attn_body, tq=tq, tk=tk)
    return pl.pallas_call(
        body,
        grid=(s // tq, N_HEADS // HEADS_PER_STEP),
        in_specs=[pl.BlockSpec((tq, HEADS_PER_STEP * HEAD_DIM), lambda i, h: (i, h)),
                  pl.BlockSpec((s, HEADS_PER_STEP * HEAD_DIM), lambda i, h: (0, h)),
                  pl.BlockSpec((s, HEADS_PER_STEP * HEAD_DIM), lambda i, h: (0, h)),
                  pl.BlockSpec((tq, s), lambda i, h: (i, 0))],
        out_specs=pl.BlockSpec((tq, HEADS_PER_STEP * HEAD_DIM), lambda i, h: (i, h)),
        out_shape=jax.ShapeDtypeStruct((s, N_HEADS * HEAD_DIM), BF16),
        scratch_shapes=_softmax_scratch(HEADS_PER_STEP, tq, tk),
        compiler_params=_cparams("arbitrary", "arbitrary"),
        name="attn_dsa",
    )(q, k, v, bias)


def _sortable_key(score):
    bits = lax.bitcast_convert_type(score, jnp.int32)
    return bits ^ ((bits >> 31) & 0x7FFFFFFF)


def _indexer_body(iq_ref, ik_ref, iw_ref, bias_ref, keys_ref, qz_ref, wb_ref, *, tq, n_sel):
    i = pl.program_id(0)
    s_len = ik_ref.shape[0]
    n_chunks = s_len // tq
    lane = lax.broadcasted_iota(jnp.int32, (1, LANES), 1)
    iw = iw_ref[...]
    for p in range(IDX_HEADS // 2):
        pair = iq_ref[:, p * LANES:(p + 1) * LANES]
        zero = jnp.zeros_like(pair)
        qz_ref[2 * p] = jnp.where(lane < IDX_DIM, pair, zero)
        qz_ref[2 * p + 1] = jnp.where(lane >= IDX_DIM, pair, zero)
    for hh in range(IDX_HEADS):
        wb_ref[hh] = jnp.broadcast_to(iw[:, hh:hh + 1], (tq, LANES))

    def score_chunk(start):
        ik = ik_ref[pl.ds(start, tq), :]
        sc = jnp.zeros((tq, tq), F32)
        for hh in range(IDX_HEADS):
            rel = jnp.maximum(_dot_nt(qz_ref[hh], ik), 0.0)
            sc = sc + jnp.tile(wb_ref[hh], (1, tq // LANES)) * rel
        return sc

    def full_body(j, carry):
        start = pl.multiple_of(j * tq, tq)
        keys_ref[:, pl.ds(start, tq)] = _sortable_key(score_chunk(start))
        return carry

    lax.fori_loop(0, i, full_body, 0)
    diag = pl.multiple_of(i * tq, tq)
    sc = jnp.where(_causal_keep(tq, tq, 0), score_chunk(diag), -jnp.inf)
    keys_ref[:, pl.ds(diag, tq)] = _sortable_key(sc)

    def bit_body(b, cand):
        trial = cand | lax.shift_left(jnp.int32(1), 31 - b)
        trial_signed = trial ^ INT_MIN

        def count_body(j, cnt):
            kk = keys_ref[:, pl.ds(pl.multiple_of(j * tq, tq), tq)]
            ge = (kk >= trial_signed).astype(jnp.int32)
            for g in range(tq // LANES):
                cnt = cnt + ge[:, g * LANES:(g + 1) * LANES]
            return cnt

        cnt = lax.fori_loop(0, i + 1, count_body, jnp.zeros((tq, LANES), jnp.int32))
        total = jnp.sum(cnt, axis=1, keepdims=True)
        return jnp.where(total >= n_sel, trial, cand)

    cand = lax.fori_loop(0, 32, bit_body, jnp.zeros((tq, 1), jnp.int32))
    thr = jnp.maximum(cand ^ INT_MIN, KEY_NEG_INF + 1)

    def write_body(j, carry):
        start = pl.multiple_of(j * tq, tq)
        kk = keys_ref[:, pl.ds(start, tq)]
        bias_ref[:, pl.ds(start, tq)] = jnp.where(kk >= thr, 0.0, NEG).astype(bias_ref.dtype)
        return carry

    lax.fori_loop(0, i + 1, write_body, 0)

    def fill_body(j, carry):
        bias_ref[:, pl.ds(pl.multiple_of(j * tq, tq), tq)] = jnp.full((tq, tq), NEG, bias_ref.dtype)
        return carry

    lax.fori_loop(i + 1, n_chunks, fill_body, 0)


def _dsa_select_bias(iq, ik2, iw, n_sel, tq=256):
    s = iq.shape[0]
    tq = min(tq, s)
    body = functools.partial(_indexer_body, tq=tq, n_sel=n_sel)
    return pl.pallas_call(
        body,
        grid=(s // tq,),
        in_specs=[pl.BlockSpec((tq, IDX_HEADS * IDX_DIM), lambda i: (i, 0)),
                  pl.BlockSpec((s, LANES), lambda i: (0, 0)),
                  pl.BlockSpec((tq, LANES), lambda i: (i, 0))],
        out_specs=pl.BlockSpec((tq, s), lambda i: (i, 0)),
        out_shape=jax.ShapeDtypeStruct((s, s), BF16),
        scratch_shapes=[pltpu.VMEM((tq, s), jnp.int32),
                        pltpu.VMEM((IDX_HEADS, tq, LANES), BF16),
                        pltpu.VMEM((IDX_HEADS, tq, LANES), F32)],
        compiler_params=_cparams("arbitrary"),
        name="dsa_select",
    )(iq, ik2, iw)


def _fox_cum_body(f_ref, b_ref, o_ref, carry_ref, *, tc):
    @pl.when(pl.program_id(0) == 0)
    def _():
        carry_ref[...] = jnp.zeros(carry_ref.shape, F32)

    z = f_ref[...] + b_ref[...]
    lf = jnp.minimum(z, 0.0) - jnp.log(1.0 + jnp.exp(-jnp.abs(z)))
    row = lax.broadcasted_iota(jnp.int32, (tc, tc), 0)
    col = lax.broadcasted_iota(jnp.int32, (tc, tc), 1)
    tri = (row >= col).astype(BF16)
    hi = lf.astype(BF16)
    r1 = lf - hi.astype(F32)
    mid = r1.astype(BF16)
    lo = (r1 - mid.astype(F32)).astype(BF16)
    cs = (jnp.dot(tri, hi, preferred_element_type=F32) + jnp.dot(tri, mid, preferred_element_type=F32)
          + jnp.dot(tri, lo, preferred_element_type=F32)) + carry_ref[...]
    o_ref[...] = cs
    carry_ref[...] = cs[tc - 1:tc, :]


def _fox_cumsum(f_raw, b_pad, tc=512):
    s = f_raw.shape[0]
    tc = min(tc, s)
    return pl.pallas_call(
        functools.partial(_fox_cum_body, tc=tc),
        grid=(s // tc,),
        in_specs=[pl.BlockSpec((tc, LANES), lambda m: (m, 0)), pl.BlockSpec((1, LANES), lambda m: (0, 0))],
        out_specs=pl.BlockSpec((tc, LANES), lambda m: (m, 0)),
        out_shape=jax.ShapeDtypeStruct((s, LANES), F32),
        scratch_shapes=[pltpu.VMEM((1, LANES), F32)],
        compiler_params=_cparams("arbitrary"),
        name="fox_cumsum",
    )(f_raw, b_pad)


def _rope_tables(positions, rot_dim, period):
    half = rot_dim // 2
    inv_freq = ROPE_THETA ** (-jnp.arange(0, rot_dim, 2, dtype=F32) / rot_dim)
    ang = positions.astype(F32)[:, None] * inv_freq
    cos, sin = jnp.cos(ang), jnp.sin(ang)
    lp = jnp.arange(LANES) % period
    in_x1 = lp < half
    in_x2 = (lp >= half) & (lp < 2 * half)
    idx = jnp.where(in_x1, lp, jnp.clip(lp - half, 0, half - 1))
    cos_l, sin_l = cos[:, idx], sin[:, idx]
    c = jnp.where(in_x1 | in_x2, cos_l, 1.0)
    s1 = jnp.where(in_x2, sin_l, 0.0)
    s2 = jnp.where(in_x1, -sin_l, 0.0)
    return c, s1, s2


def _pad_cols(w, width):
    return jnp.pad(w, ((0, 0), (0, width - w.shape[1])))


def _mla_mixer(h, tabs, w_in, q_a_g, kv_a_g, w_q_b, w_kv_b, q_g, k_g):
    s = h.shape[0]
    nope = HEAD_DIM
    scale = MLA_HEAD ** -0.5 * LOG2E
    gains = jnp.concatenate([q_a_g, kv_a_g])
    c_norm = _proj(h, w_in, 0, 0, 2 * MLA_LORA, tn=MLA_LORA, gain=gains, gw=MLA_LORA, real=MLA_LORA)
    kpe = _proj(h, _pad_cols(w_in[0][:, 2 * MLA_LORA:], LANES)[None], 0, 0, LANES, tn=LANES, out_dtype=F32,
                epi=_epi_plain)
    wq = w_q_b[0].reshape(MLA_LORA, N_HEADS, MLA_HEAD)
    wq = jnp.concatenate([wq[:, :, MLA_ROPE:], wq[:, :, :MLA_ROPE],
                          jnp.zeros((MLA_LORA, N_HEADS, 2 * LANES - MLA_HEAD), F32)], axis=2)
    wq = wq.reshape(1, MLA_LORA, N_HEADS * 2 * LANES)
    qg = jnp.concatenate([q_g[MLA_ROPE:], q_g[:MLA_ROPE], jnp.zeros((2 * LANES - MLA_HEAD,), F32)])
    q = _proj(c_norm, wq, 0, 0, N_HEADS * 2 * LANES, tn=512, gain=jnp.tile(qg, N_HEADS), tabs=tabs,
              gw=2 * LANES, real=MLA_HEAD, half=MLA_ROPE // 2, rope_groups=lambda g: g % 2 == 1, scale=scale)
    tm, tn = min(TM, s), TN
    aux = [(kpe, (tm, LANES), lambda n, m: (m, 0)),
           (k_g[MLA_ROPE:].reshape(1, nope), (1, nope), lambda n, m: (0, 0)),
           (jnp.pad(k_g[:MLA_ROPE], (0, LANES - MLA_ROPE)).reshape(1, LANES), (1, LANES), lambda n, m: (0, 0))]
    aux += _row_tab_aux(tabs, tm)
    outs = [((s, N_HEADS * 2 * LANES), BF16, (tm, tn), lambda n, m: (m, n)),
            ((s, N_HEADS * HEAD_DIM), BF16, (tm, tn // 2), lambda n, m: (m, n))]
    k, v = _mm(c_norm, [(w_kv_b, 0, 0)], aux, outs, _epi_mla_kv, tm=tm, tn=tn,
               n_tiles=N_HEADS * 2 * LANES // tn, a_blk=1, name="mla_kv")
    return _dense_attention(q, k, v, mode="mla", dq=2 * LANES)


def _dsa_mixer(h, tabs_head, tabs_idx, w_in, q_g, k_g, idx_k_g):
    s = h.shape[0]
    hd = N_HEADS * HEAD_DIM
    ih = IDX_HEADS * IDX_DIM
    half = HEAD_DIM // 8
    q = _proj(h, w_in, 0, 0, hd, tn=512, gain=jnp.tile(q_g, N_HEADS), tabs=tabs_head, half=half,
              scale=HEAD_DIM ** -0.5 * LOG2E)
    k = _proj(h, w_in, 0, hd, hd, tn=512, gain=jnp.tile(k_g, N_HEADS), tabs=tabs_head, half=half)
    v = _proj(h, w_in, 0, 2 * hd, hd, tn=512, epi=_epi_plain)
    iq = _proj(h, w_in, 0, 3 * hd, ih, tn=512, tabs=tabs_idx, half=IDX_DIM // 8)
    w_ik = w_in[0][:, 3 * hd + ih:3 * hd + ih + IDX_DIM]
    ik2 = _proj(h, jnp.concatenate([w_ik, w_ik], axis=1)[None], 0, 0, LANES, tn=LANES,
                gain=jnp.tile(idx_k_g, 2), tabs=tabs_idx, half=IDX_DIM // 8)
    w_iw = _pad_cols(w_in[0][:, 3 * hd + ih + IDX_DIM:], LANES)[None]
    iw = _proj(h, w_iw, 0, 0, LANES, tn=LANES, out_dtype=F32,
               epi=functools.partial(_epi_heads, gw=LANES, real=LANES, half=0, rope_groups=None,
                                     scale=IDX_HEADS ** -0.5 * IDX_DIM ** -0.5, use_gain=False, use_rope=False))
    bias = _dsa_select_bias(iq, ik2, iw, min(IDX_TOPK, s // 4))
    return _dsa_attention(q, k, v, bias)


def _diff_mixer(h, tabs, layer_idx, w_in, q_g, k_g, lam_params, subln_g):
    w = N_HEADS * 2 * DIFF_DIM
    half = DIFF_DIM // 8
    q = _proj(h, w_in, 0, 0, w, tn=512, gain=jnp.tile(q_g, 2 * N_HEADS), tabs=tabs, gw=DIFF_DIM, real=DIFF_DIM,
              half=half, scale=DIFF_DIM ** -0.5 * LOG2E)
    k = _proj(h, w_in, 0, w, w, tn=512, gain=jnp.tile(k_g, 2 * N_HEADS), tabs=tabs, gw=DIFF_DIM, real=DIFF_DIM,
              half=half)
    v = _proj(h, w_in, 0, 2 * w, w, tn=512, epi=_epi_plain)
    lam_init = 0.8 - 0.6 * math.exp(-0.3 * layer_idx)
    return _dense_attention(q, k, v, mode="diff", dq=HEAD_DIM, extra=(lam_params, subln_g.reshape(1, HEAD_DIM)),
                            lam_init=lam_init)


def _fox_mixer(h, w_in, b_f, q_g, k_g):
    s = h.shape[0]
    hd = N_HEADS * HEAD_DIM
    q = _proj(h, w_in, 0, 0, hd, tn=512, gain=jnp.tile(q_g, N_HEADS), scale=HEAD_DIM ** -0.5 * LOG2E)
    k = _proj(h, w_in, 0, hd, hd, tn=512, gain=jnp.tile(k_g, N_HEADS))
    v = _proj(h, w_in, 0, 2 * hd, hd, tn=512, epi=_epi_plain)
    w_f = _pad_cols(w_in[0][:, 3 * hd:3 * hd + N_HEADS], LANES)[None]
    f_raw = _proj(h, w_f, 0, 0, LANES, tn=LANES, out_dtype=F32, epi=_epi_plain)
    gate = _proj(h, w_in[:, :, 3 * hd + N_HEADS:], 0, 0, hd, tn=512, out_dtype=F32, epi=_epi_sigmoid)
    cum = _fox_cumsum(f_raw, jnp.pad(b_f, (0, LANES - N_HEADS)).reshape(1, LANES))
    cumt = cum[:, :N_HEADS].T.reshape(N_HEADS, 1, s)
    return _dense_attention(q, k, v, mode="fox", dq=HEAD_DIM, extra=(cum, cumt, gate))


def _swiglu_ffn(h, x, gate, w_gate_up, w_down, layer):
    s = h.shape[0]
    tm, tn = min(TM, s), TN
    outs = [((s, FFN_HIDDEN), BF16, (tm, tn), lambda n, m: (m, n))]
    act = _mm(h, [(w_gate_up, layer, 0), (w_gate_up, layer, FFN_HIDDEN // tn)], [], outs, _epi_swiglu,
              tm=tm, tn=tn, n_tiles=FFN_HIDDEN // tn, name="ffn_gate_up")[0]
    return _residual_mm(act, w_down, layer, x, gate, tm=TM_DEEP)


def kernel(x, c, positions, ln_mix_g, ln_ffn_g, ada_w, ada_b, ffn_w_gate_up, ffn_w_down, mla_w_in, mla_q_a_g, mla_kv_a_g, mla_w_q_b, mla_w_kv_b, mla_q_g, mla_k_g, mla_w_out, dsa_w_in, dsa_q_g, dsa_k_g, dsa_idx_k_g, dsa_w_out, diff_w_in, diff_q_g, diff_k_g, diff_lambda_q1, diff_lambda_k1, diff_lambda_q2, diff_lambda_k2, diff_subln_g, diff_w_out, fox_w_in, fox_b_f, fox_q_g, fox_k_g, fox_w_out):
    batch, s, d = x.shape
    assert batch == 1 and d == D_MODEL
    depth = ada_w.shape[0]
    pos = positions[0]
    tabs_head = _rope_tables(pos, HEAD_DIM // 4, LANES)
    tabs_small = _rope_tables(pos, IDX_DIM // 4, IDX_DIM)
    tabs_mla = _rope_tables(pos, MLA_ROPE, LANES)
    mod = _ada_mod(c, ada_w, ada_b)
    xs = x[0]
    for i in range(depth):
        sh1, sc1, g1, sh2, sc2, g2 = [mod[i, t * d:(t + 1) * d] for t in range(6)]
        h = _normmod(xs, ln_mix_g[i], sc1, sh1)
        kind, j = i % 4, i // 4
        if kind == 0:
            o = _mla_mixer(h, tabs_mla, mla_w_in[j:j + 1], mla_q_a_g[j], mla_kv_a_g[j], mla_w_q_b[j:j + 1],
                           mla_w_kv_b[j:j + 1], mla_q_g[j], mla_k_g[j])
            w_out = mla_w_out
        elif kind == 1:
            o = _dsa_mixer(h, tabs_head, tabs_small, dsa_w_in[j:j + 1], dsa_q_g[j], dsa_k_g[j], dsa_idx_k_g[j])
            w_out = dsa_w_out
        elif kind == 2:
            lam_params = jnp.stack([diff_lambda_q1[j], diff_lambda_k1[j], diff_lambda_q2[j], diff_lambda_k2[j]])
            o = _diff_mixer(h, tabs_small, i, diff_w_in[j:j + 1], diff_q_g[j], diff_k_g[j], lam_params,
                            diff_subln_g[j])
            w_out = diff_w_out
        else:
            o = _fox_mixer(h, fox_w_in[j:j + 1], fox_b_f[j], fox_q_g[j], fox_k_g[j])
            w_out = fox_w_out
        xs = _residual_mm(o, w_out, j, xs, g1)
        h = _normmod(xs, ln_ffn_g[i], sc2, sh2)
        xs = _swiglu_ffn(h, xs, g2, ffn_w_gate_up, ffn_w_down, i)
    return xs[None]
```

```python
import functools
import math

import jax
import jax.numpy as jnp
from jax import lax
from jax.experimental import pallas as pl
from jax.experimental.pallas import tpu as pltpu

F32 = jnp.float32
BF16 = jnp.bfloat16

D_MODEL = 2048
N_HEADS = 16
HEAD_DIM = 128
ROPE_THETA = 500000.0
RMS_EPS = 1e-6
FFN_HIDDEN = 5632
MLA_LORA = 512
MLA_ROPE = 64
MLA_HEAD = 192
IDX_HEADS = 16
IDX_DIM = 64
IDX_TOPK = 256
DIFF_DIM = 64

LANES = 128
LOG2E = 1.4426950408889634
NEG = -1e30
VMEM_LIMIT_BYTES = 56 * 1024 * 1024
TM = 1024
TM_DEEP = 512
SUB_ROWS = 256
TN = 512
HEADS_PER_STEP = 2
BISECT_ROWS = 128
INT_MIN = -2 ** 31
KEY_NEG_INF = (0xFF800000 ^ 0x7FFFFFFF) - 2 ** 32


def _cparams(*sem):
    return pltpu.CompilerParams(dimension_semantics=sem, vmem_limit_bytes=VMEM_LIMIT_BYTES)


def _dot_nt(a, b):
    return lax.dot_general(a, b, (((1,), (1,)), ((), ())), preferred_element_type=F32)


def _sigmoid(z):
    return 1.0 / (1.0 + jnp.exp(-z))


def _mm_body(*refs, n_w, n_aux, n_out, epi, sub):
    a_ref = refs[0]
    w_refs = refs[1:1 + n_w]
    aux_refs = refs[1 + n_w:1 + n_w + n_aux]
    out_refs = refs[1 + n_w + n_aux:1 + n_w + n_aux + n_out]
    wb_refs = refs[1 + n_w + n_aux + n_out:]
    tm = a_ref.shape[0]

    @pl.when(pl.program_id(1) == 0)
    def _cast_weights():
        for w_ref, wb_ref in zip(w_refs, wb_refs):
            wb_ref[...] = w_ref[...].astype(BF16)

    for r in range(tm // sub):
        rows = pl.ds(r * sub, sub)
        accs = [jnp.dot(a_ref[rows, :], wb_ref[...], preferred_element_type=F32) for wb_ref in wb_refs]
        epi(accs, [ref.at[rows] if ref.shape[0] == tm else ref for ref in aux_refs],
            [ref.at[rows] for ref in out_refs])


def _mm(a, ws, aux, outs, epi, *, tm, tn, n_tiles, a_blk=0, name="mm", sub=SUB_ROWS):
    s = a.shape[0]
    k = ws[0][0].shape[1]
    tm = min(tm, s)
    in_specs = [pl.BlockSpec((tm, k), lambda n, m: (m, a_blk))]
    operands = [a]
    for w, layer, off in ws:
        assert w.shape[1] == k
        in_specs.append(pl.BlockSpec((None, k, tn), lambda n, m, layer=layer, off=off: (layer, 0, n + off)))
        operands.append(w)
    for arr, block, imap in aux:
        in_specs.append(pl.BlockSpec(block, imap))
        operands.append(arr)
    out_shape = [jax.ShapeDtypeStruct(shape, dtype) for shape, dtype, _, _ in outs]
    out_specs = [pl.BlockSpec(block, imap) for _, _, block, imap in outs]
    body = functools.partial(_mm_body, n_w=len(ws), n_aux=len(aux), n_out=len(outs), epi=epi, sub=min(sub, tm))
    res = pl.pallas_call(
        body,
        grid=(n_tiles, s // tm),
        in_specs=in_specs,
        out_specs=out_specs,
        out_shape=out_shape,
        scratch_shapes=[pltpu.VMEM((k, tn), BF16) for _ in ws],
        compiler_params=_cparams("arbitrary", "arbitrary"),
        name=name,
    )(*operands)
    return res


def _rope_group(y, tabs, half):
    c, s1, s2 = tabs
    return y * c + pltpu.roll(y, half, 1) * s1 + pltpu.roll(y, LANES - half, 1) * s2


def _group_inv_rms(y, gw, real):
    tn = y.shape[1]
    sq = y * y
    n_groups = tn // LANES
    if gw == DIFF_DIM:
        lane = lax.broadcasted_iota(jnp.int32, (1, LANES), 1)
        lo = lane < DIFF_DIM
        out = []
        for g in range(n_groups):
            sg = sq[:, g * LANES:(g + 1) * LANES]
            s_lo = jnp.sum(jnp.where(lo, sg, 0.0), axis=1, keepdims=True)
            s_hi = jnp.sum(jnp.where(lo, 0.0, sg), axis=1, keepdims=True)
            out.append(lax.rsqrt(jnp.where(lo, s_lo, s_hi) * (1.0 / real) + RMS_EPS))
        return out
    sums = [jnp.sum(sq[:, g * LANES:(g + 1) * LANES], axis=1, keepdims=True) for g in range(n_groups)]
    per = gw // LANES
    out = []
    for h in range(n_groups // per):
        tot = sums[h * per]
        for t in range(1, per):
            tot = tot + sums[h * per + t]
        r = lax.rsqrt(tot * (1.0 / real) + RMS_EPS)
        out.extend([r] * per)
    return out


def _epi_heads(accs, aux_refs, out_refs, *, gw, real, half, rope_groups, scale, use_gain, use_rope):
    y = accs[0]
    tn = y.shape[1]
    idx = 0
    gain = None
    if use_gain:
        gain = aux_refs[idx][...]
        idx += 1
        inv = _group_inv_rms(y, gw, real)
    if use_rope:
        tabs = tuple(aux_refs[idx + t][...] for t in range(3))
    for g in range(tn // LANES):
        yg = y[:, g * LANES:(g + 1) * LANES]
        if use_gain:
            yg = yg * inv[g] * gain[:, g * LANES:(g + 1) * LANES]
        if use_rope and rope_groups(g):
            yg = _rope_group(yg, tabs, half)
        if scale != 1.0:
            yg = yg * scale
        out_refs[0][:, g * LANES:(g + 1) * LANES] = yg.astype(out_refs[0].dtype)


def _epi_plain(accs, aux_refs, out_refs):
    out_refs[0][...] = accs[0].astype(out_refs[0].dtype)


def _epi_sigmoid(accs, aux_refs, out_refs):
    out_refs[0][...] = _sigmoid(accs[0]).astype(out_refs[0].dtype)


def _epi_residual(accs, aux_refs, out_refs):
    x_ref, g_ref = aux_refs
    out_refs[0][...] = x_ref[...] + g_ref[...] * accs[0]


def _epi_swiglu(accs, aux_refs, out_refs):
    gate, up = accs
    out_refs[0][...] = (gate * _sigmoid(gate) * up).astype(out_refs[0].dtype)


def _epi_mla_kv(accs, aux_refs, out_refs):
    kpe_ref, g_nope_ref, g_pe_ref, c_ref, s1_ref, s2_ref = aux_refs
    k_ref, v_ref = out_refs
    y = accs[0]
    kpe = kpe_ref[...]
    ss_pe = jnp.sum(kpe * kpe, axis=1, keepdims=True)
    tabs = (c_ref[...], s1_ref[...], s2_ref[...])
    g_nope = g_nope_ref[...]
    g_pe = g_pe_ref[...]
    for h in range(y.shape[1] // (2 * LANES)):
        kn = y[:, 2 * h * LANES:(2 * h + 1) * LANES]
        v = y[:, (2 * h + 1) * LANES:(2 * h + 2) * LANES]
        r = lax.rsqrt((jnp.sum(kn * kn, axis=1, keepdims=True) + ss_pe) * (1.0 / MLA_HEAD) + RMS_EPS)
        k_ref[:, 2 * h * LANES:(2 * h + 1) * LANES] = (kn * r * g_nope).astype(k_ref.dtype)
        pe = _rope_group(kpe * r * g_pe, tabs, MLA_ROPE // 2)
        k_ref[:, (2 * h + 1) * LANES:(2 * h + 2) * LANES] = pe.astype(k_ref.dtype)
        v_ref[:, h * LANES:(h + 1) * LANES] = v.astype(v_ref.dtype)


def _row_tab_aux(tabs, tm):
    return [(t, (tm, LANES), lambda n, m: (m, 0)) for t in tabs]


def _proj(a, w, layer, col_off, width, *, tn, out_dtype=BF16, gain=None, tabs=None, gw=LANES, real=LANES,
          half=0, rope_groups=lambda g: True, scale=1.0, epi=None, tm=TM, a_blk=0):
    s = a.shape[0]
    tm = min(tm, s)
    assert col_off % tn == 0 and width % tn == 0
    aux = []
    if gain is not None:
        aux.append((gain.reshape(1, width).astype(F32), (1, tn), lambda n, m: (0, n)))
    if tabs is not None:
        aux.extend(_row_tab_aux(tabs, tm))
    if epi is None:
        epi = functools.partial(_epi_heads, gw=gw, real=real, half=half, rope_groups=rope_groups, scale=scale,
                                use_gain=gain is not None, use_rope=tabs is not None)
    outs = [((s, width), out_dtype, (tm, tn), lambda n, m: (m, n))]
    return _mm(a, [(w, layer, col_off // tn)], aux, outs, epi, tm=tm, tn=tn, n_tiles=width // tn, a_blk=a_blk,
               name="proj")[0]


def _residual_mm(a, w, layer, x, gate, *, tn=TN, tm=TM):
    s, d = x.shape
    tm = min(tm, s)
    aux = [(x, (tm, tn), lambda n, m: (m, n)), (gate.reshape(1, d), (1, tn), lambda n, m: (0, n))]
    outs = [((s, d), F32, (tm, tn), lambda n, m: (m, n))]
    return _mm(a, [(w, layer, 0)], aux, outs, _epi_residual, tm=tm, tn=tn, n_tiles=d // tn, name="residual_mm")[0]


def _ada_body(c_ref, w_ref, b_ref, o_ref):
    c = c_ref[...]
    cond = c * _sigmoid(c)
    o_ref[...] = jnp.dot(cond.astype(BF16), w_ref[...].astype(BF16), preferred_element_type=F32) + b_ref[...]


def _ada_mod(c, ada_w, ada_b):
    depth, d, n = ada_w.shape
    tn = 1536
    c8 = jnp.broadcast_to(c, (8, d))
    out = pl.pallas_call(
        _ada_body,
        grid=(depth, n // tn),
        in_specs=[pl.BlockSpec((8, d), lambda l, j: (0, 0)),
                  pl.BlockSpec((None, d, tn), lambda l, j: (l, 0, j)),
                  pl.BlockSpec((None, 1, tn), lambda l, j: (l, 0, j))],
        out_specs=pl.BlockSpec((None, 8, tn), lambda l, j: (l, 0, j)),
        out_shape=jax.ShapeDtypeStruct((depth, 8, n), F32),
        compiler_params=_cparams("arbitrary", "arbitrary"),
        name="ada_mod",
    )(c8, ada_w, ada_b.reshape(depth, 1, n))
    return out[:, 0, :]


def _normmod_body(x_ref, g_ref, sc_ref, sh_ref, o_ref):
    x = x_ref[...]
    y = x * lax.rsqrt(jnp.mean(x * x, axis=1, keepdims=True) + RMS_EPS) * g_ref[...]
    o_ref[...] = (y * (1.0 + sc_ref[...]) + sh_ref[...]).astype(o_ref.dtype)


def _normmod(x, g, sc, sh, tm=256):
    s, d = x.shape
    tm = min(tm, s)
    vec = pl.BlockSpec((1, d), lambda m: (0, 0))
    return pl.pallas_call(
        _normmod_body,
        grid=(s // tm,),
        in_specs=[pl.BlockSpec((tm, d), lambda m: (m, 0)), vec, vec, vec],
        out_specs=pl.BlockSpec((tm, d), lambda m: (m, 0)),
        out_shape=jax.ShapeDtypeStruct((s, d), BF16),
        compiler_params=_cparams("arbitrary"),
        name="normmod",
    )(x, g.reshape(1, d), sc.reshape(1, d), sh.reshape(1, d))


def _softmax_scratch(n_maps, tq, tk):
    return [pltpu.VMEM((n_maps, tq, 2 * HEAD_DIM), F32), pltpu.VMEM((n_maps, tq, LANES), F32),
            pltpu.VMEM((n_maps, 2, tq, tk), F32), pltpu.VMEM((n_maps, 2, 2, tq, LANES), F32)]


def _softmax_pipeline(qs, kv_cols, k_ref, v_ref, scratch, *, tq, tk, first_chunk, first_keep, n_rest, bias_fn):
    acc_ref, m_ref, s_ref, st_ref = scratch
    ones = jnp.ones((tk, LANES), BF16)
    n_grp = tk // LANES
    maps = range(len(qs))

    def chunk_start(c):
        return pl.multiple_of(c * tk, tk)

    def stage1(c, slot, keep):
        start = chunk_start(c)
        for t in maps:
            s = _dot_nt(qs[t], k_ref[pl.ds(start, tk), kv_cols[t][0]])
            if bias_fn is not None:
                s = s + bias_fn(t, start)
            if keep is not None:
                s = jnp.where(keep, s, NEG)
            m_run = m_ref[t]
            m_new = jnp.maximum(m_run, jnp.max(s, axis=1, keepdims=True))
            s_ref[t, slot] = s
            st_ref[t, slot, 0] = m_new
            st_ref[t, slot, 1] = jnp.exp2(m_run - m_new)
            m_ref[t] = m_new

    def stage2(c, slot):
        start = chunk_start(c)
        for t in maps:
            v1 = jnp.concatenate([v_ref[pl.ds(start, tk), kv_cols[t][1]], ones], axis=1)
            m_new = st_ref[t, slot, 0]
            alpha = st_ref[t, slot, 1]
            p = jnp.concatenate([jnp.exp2(s_ref[t, slot, :, g * LANES:(g + 1) * LANES] - m_new)
                                 for g in range(n_grp)], axis=1)
            pv = jnp.dot(p.astype(BF16), v1, preferred_element_type=F32)
            acc_ref[t] = jnp.concatenate([alpha, alpha], axis=1) * acc_ref[t] + pv

    acc_ref[...] = jnp.zeros(acc_ref.shape, F32)
    m_ref[...] = jnp.full(m_ref.shape, NEG, F32)
    stage1(first_chunk, 0, first_keep)

    def body(t2, carry):
        c0 = 2 * t2
        stage1(c0, 1, None)
        stage2(jnp.where(t2 == 0, first_chunk, c0 - 1), 0)
        stage1(c0 + 1, 0, None)
        stage2(c0, 1)
        return carry

    n_pairs = n_rest // 2
    lax.fori_loop(0, n_pairs, body, 0)
    pending = jnp.where(n_pairs == 0, first_chunk, 2 * n_pairs - 1)

    @pl.when(n_rest % 2 == 1)
    def _odd_tail():
        stage1(n_rest - 1, 1, None)
        stage2(pending, 0)
        stage2(n_rest - 1, 1)

    @pl.when(n_rest % 2 == 0)
    def _even_tail():
        stage2(pending, 0)


def _causal_keep(tq, tk, col_shift):
    row = lax.broadcasted_iota(jnp.int32, (tq, 1), 0)
    col = lax.broadcasted_iota(jnp.int32, (1, tk), 1)
    return row >= col + col_shift


def _dense_attn_body(*refs, mode, tq, tk, lam_init):
    if mode == "mla":
        q_ref, k_ref, v_ref, o_ref = refs[:4]
        scratch = refs[4:]
    elif mode == "fox":
        q_ref, k_ref, v_ref, cum_ref, cumt_ref, gate_ref, o_ref = refs[:7]
        scratch = refs[7:]
    else:
        q_ref, k_ref, v_ref, lam_ref, subg_ref, o_ref = refs[:6]
        scratch = refs[6:]
    hg = pl.program_id(0)
    i = pl.program_id(1)
    acc_ref = scratch[0]
    dq = q_ref.shape[1] // HEADS_PER_STEP
    heads = range(HEADS_PER_STEP)
    lane = lax.broadcasted_iota(jnp.int32, (1, LANES), 1)

    def head_cols(hh, width=HEAD_DIM):
        return slice(hh * width, (hh + 1) * width)

    qs, kv_cols = [], []
    for hh in heads:
        q = q_ref[:, head_cols(hh, dq)]
        if mode == "diff":
            zero = jnp.zeros_like(q)
            qs += [jnp.where(lane < DIFF_DIM, q, zero), jnp.where(lane >= DIFF_DIM, q, zero)]
            kv_cols += [(head_cols(hh, dq), head_cols(hh))] * 2
        else:
            qs.append(q)
            kv_cols.append((head_cols(hh, dq), head_cols(hh)))
    if mode == "fox":
        cum = cum_ref[...]
        cqs = [jnp.sum(jnp.where(lane == hg * HEADS_PER_STEP + hh, cum, 0.0), axis=1, keepdims=True) * LOG2E
               for hh in heads]

        def bias_fn(t, start):
            return cqs[t] - cumt_ref[t, :, pl.ds(start, tk)] * LOG2E
    else:
        bias_fn = None

    _softmax_pipeline(qs, kv_cols, k_ref, v_ref, scratch, tq=tq, tk=tk, first_chunk=i,
                      first_keep=_causal_keep(tq, tk, 0), n_rest=i, bias_fn=bias_fn)

    def normalised(t):
        acc = acc_ref[t]
        return acc[:, :HEAD_DIM] / acc[:, HEAD_DIM:]

    for hh in heads:
        if mode == "diff":
            lp = lam_ref[...]
            lam = (jnp.exp(jnp.sum(lp[0:1] * lp[1:2], axis=1, keepdims=True))
                   - jnp.exp(jnp.sum(lp[2:3] * lp[3:4], axis=1, keepdims=True)) + lam_init)
            o = normalised(2 * hh) - lam * normalised(2 * hh + 1)
            o = o * lax.rsqrt(jnp.mean(o * o, axis=1, keepdims=True) + RMS_EPS) * subg_ref[...]
            o = o * (1.0 - lam_init)
        else:
            o = normalised(hh)
            if mode == "fox":
                o = o * gate_ref[:, head_cols(hh)]
        o_ref[:, head_cols(hh)] = o.astype(o_ref.dtype)


def _dense_attention(q, k, v, *, mode, dq, extra=(), lam_init=0.0, tq=512):
    s = q.shape[0]
    tq = min(tq, s)
    tk = tq
    g = HEADS_PER_STEP
    in_specs = [pl.BlockSpec((tq, g * dq), lambda h, i: (i, h)),
                pl.BlockSpec((s, g * dq), lambda h, i: (0, h)),
                pl.BlockSpec((s, g * HEAD_DIM), lambda h, i: (0, h))]
    operands = [q, k, v]
    if mode == "fox":
        cum, cumt, gate = extra
        in_specs += [pl.BlockSpec((tq, LANES), lambda h, i: (i, 0)),
                     pl.BlockSpec((g, 1, s), lambda h, i: (h, 0, 0)),
                     pl.BlockSpec((tq, g * HEAD_DIM), lambda h, i: (i, h))]
        operands += [cum, cumt, gate]
    elif mode == "diff":
        lam_params, subln_g = extra
        in_specs += [pl.BlockSpec((4, DIFF_DIM), lambda h, i: (0, 0)),
                     pl.BlockSpec((1, HEAD_DIM), lambda h, i: (0, 0))]
        operands += [lam_params, subln_g]
    n_maps = g * (2 if mode == "diff" else 1)
    body = functools.partial(_dense_attn_body, mode=mode, tq=tq, tk=tk, lam_init=lam_init)
    return pl.pallas_call(
        body,
        grid=(N_HEADS // g, s // tq),
        in_specs=in_specs,
        out_specs=pl.BlockSpec((tq, g * HEAD_DIM), lambda h, i: (i, h)),
        out_shape=jax.ShapeDtypeStruct((s, N_HEADS * HEAD_DIM), BF16),
        scratch_shapes=_softmax_scratch(n_maps, tq, tk),
        compiler_params=_cparams("arbitrary", "arbitrary"),
        name="attn_" + mode,
    )(*operands)


def _dsa_attn_body(q_ref, k_ref, v_ref, bias_ref, o_ref, *scratch, tq, tk):
    i = pl.program_id(0)

    def bias_fn(t, start):
        return bias_ref[:, pl.ds(start, tk)].astype(F32)

    cols = [slice(hh * HEAD_DIM, (hh + 1) * HEAD_DIM) for hh in range(HEADS_PER_STEP)]
    _softmax_pipeline([q_ref[:, c] for c in cols], [(c, c) for c in cols], k_ref, v_ref, scratch, tq=tq, tk=tk,
                      first_chunk=i, first_keep=None, n_rest=i, bias_fn=bias_fn)
    for t, c in enumerate(cols):
        acc = scratch[0][t]
        o_ref[:, c] = (acc[:, :HEAD_DIM] / acc[:, HEAD_DIM:]).astype(o_ref.dtype)


def _dsa_attention(q, k, v, bias, *, tq=512):
    s = q.shape[0]
    tq = min(tq, s)
    tk = tq
    body = functools.partial(_dsa_attn_body, tq=tq, tk=tk)
    return pl.pallas_call(
        body,
        grid=(s // tq, N_HEADS // HEADS_PER_STEP),
        in_specs=[pl.BlockSpec((tq, HEADS_PER_STEP * HEAD_DIM), lambda i, h: (i, h)),
                  pl.BlockSpec((s, HEADS_PER_STEP * HEAD_DIM), lambda i, h: (0, h)),
                  pl.BlockSpec((s, HEADS_PER_STEP * HEAD_DIM), lambda i, h: (0, h)),
                  pl.BlockSpec((tq, s), lambda i, h: (i, 0))],
        out_specs=pl.BlockSpec((tq, HEADS_PER_STEP * HEAD_DIM), lambda i, h: (i, h)),
        out_shape=jax.ShapeDtypeStruct((s, N_HEADS * HEAD_DIM), BF16),
        scratch_shapes=_softmax_scratch(HEADS_PER_STEP, tq, tk),
        compiler_params=_cparams("arbitrary", "arbitrary"),
        name="attn_dsa",
    )(q, k, v, bias)


def _sortable_key(score):
    bits = lax.bitcast_convert_type(score, jnp.int32)
    return bits ^ ((bits >> 31) & 0x7FFFFFFF)


def _indexer_body(iq_ref, ik_ref, iw_ref, bias_ref, keys_ref, qz_ref, wb_ref, thr_ref, *, tq, n_sel):
    i = pl.program_id(0)
    s_len = ik_ref.shape[0]
    n_chunks = s_len // tq
    lane = lax.broadcasted_iota(jnp.int32, (1, LANES), 1)
    iw = iw_ref[...]
    for p in range(IDX_HEADS // 2):
        pair = iq_ref[:, p * LANES:(p + 1) * LANES]
        zero = jnp.zeros_like(pair)
        qz_ref[2 * p] = jnp.where(lane < IDX_DIM, pair, zero)
        qz_ref[2 * p + 1] = jnp.where(lane >= IDX_DIM, pair, zero)
    for hh in range(IDX_HEADS):
        wb_ref[hh] = jnp.broadcast_to(iw[:, hh:hh + 1], (tq, LANES))

    def score_chunk(start):
        ik = ik_ref[pl.ds(start, tq), :]
        sc = jnp.zeros((tq, tq), F32)
        for hh in range(IDX_HEADS):
            rel = jnp.maximum(_dot_nt(qz_ref[hh], ik), 0.0)
            sc = sc + jnp.tile(wb_ref[hh], (1, tq // LANES)) * rel
        return sc

    def full_body(j, carry):
        start = pl.multiple_of(j * tq, tq)
        keys_ref[:, pl.ds(start, tq)] = _sortable_key(score_chunk(start))
        return carry

    lax.fori_loop(0, i, full_body, 0)
    diag = pl.multiple_of(i * tq, tq)
    sc = jnp.where(_causal_keep(tq, tq, 0), score_chunk(diag), -jnp.inf)
    keys_ref[:, pl.ds(diag, tq)] = _sortable_key(sc)

    keys_ref[:, pl.ds(pl.multiple_of((i + 1) * tq, tq), tq)] = jnp.full((tq, tq), KEY_NEG_INF, jnp.int32)
    strips = [pl.ds(r0, BISECT_ROWS) for r0 in range(0, tq, BISECT_ROWS)]

    def bit_body(b, cands):
        bit = lax.shift_left(jnp.int32(1), 31 - b)
        trials = [cand ^ bit for cand in cands]
        counts = []
        for rows, trial in zip(strips, trials):
            def count_body(j, cnt, rows=rows, trial=trial):
                start = pl.multiple_of(j * (2 * tq), 2 * tq)
                for g in range(2 * tq // LANES):
                    kk = keys_ref[rows, pl.ds(start + g * LANES, LANES)]
                    cnt = cnt + (kk >= trial).astype(jnp.int32)
                return cnt

            counts.append(lax.fori_loop(0, (i + 2) // 2, count_body, jnp.zeros((BISECT_ROWS, LANES), jnp.int32)))
        return tuple(jnp.where(jnp.sum(cnt, axis=1, keepdims=True) >= n_sel, trial, cand)
                     for cnt, trial, cand in zip(counts, trials, cands))

    cands = lax.fori_loop(0, 32, bit_body,
                          tuple(jnp.full((BISECT_ROWS, LANES), INT_MIN, jnp.int32) for _ in strips))
    for rows, cand in zip(strips, cands):
        thr_ref[rows, :] = jnp.maximum(cand, KEY_NEG_INF + 1)

    def write_body(j, carry):
        start = pl.multiple_of(j * tq, tq)
        thr = thr_ref[...]
        for g in range(tq // LANES):
            cols = pl.ds(start + g * LANES, LANES)
            bias_ref[:, cols] = jnp.where(keys_ref[:, cols] >= thr, 0.0, NEG).astype(bias_ref.dtype)
        return carry

    lax.fori_loop(0, i + 1, write_body, 0)

    def fill_body(j, carry):
        bias_ref[:, pl.ds(pl.multiple_of(j * tq, tq), tq)] = jnp.full((tq, tq), NEG, bias_ref.dtype)
        return carry

    lax.fori_loop(i + 1, n_chunks, fill_body, 0)


def _dsa_select_bias(iq, ik2, iw, n_sel, tq=256):
    s = iq.shape[0]
    tq = min(tq, s)
    body = functools.partial(_indexer_body, tq=tq, n_sel=n_sel)
    return pl.pallas_call(
        body,
        grid=(s // tq,),
        in_specs=[pl.BlockSpec((tq, IDX_HEADS * IDX_DIM), lambda i: (i, 0)),
                  pl.BlockSpec((s, LANES), lambda i: (0, 0)),
                  pl.BlockSpec((tq, LANES), lambda i: (i, 0))],
        out_specs=pl.BlockSpec((tq, s), lambda i: (i, 0)),
        out_shape=jax.ShapeDtypeStruct((s, s), BF16),
        scratch_shapes=[pltpu.VMEM((tq, s + tq), jnp.int32),
                        pltpu.VMEM((IDX_HEADS, tq, LANES), BF16),
                        pltpu.VMEM((IDX_HEADS, tq, LANES), F32),
                        pltpu.VMEM((tq, LANES), jnp.int32)],
        compiler_params=_cparams("arbitrary"),
        name="dsa_select",
    )(iq, ik2, iw)


def _fox_cum_body(f_ref, b_ref, o_ref, carry_ref, *, tc):
    @pl.when(pl.program_id(0) == 0)
    def _():
        carry_ref[...] = jnp.zeros(carry_ref.shape, F32)

    z = f_ref[...] + b_ref[...]
    lf = jnp.minimum(z, 0.0) - jnp.log(1.0 + jnp.exp(-jnp.abs(z)))
    row = lax.broadcasted_iota(jnp.int32, (tc, tc), 0)
    col = lax.broadcasted_iota(jnp.int32, (tc, tc), 1)
    tri = (row >= col).astype(BF16)
    hi = lf.astype(BF16)
    r1 = lf - hi.astype(F32)
    mid = r1.astype(BF16)
    lo = (r1 - mid.astype(F32)).astype(BF16)
    cs = (jnp.dot(tri, hi, preferred_element_type=F32) + jnp.dot(tri, mid, preferred_element_type=F32)
          + jnp.dot(tri, lo, preferred_element_type=F32)) + carry_ref[...]
    o_ref[...] = cs
    carry_ref[...] = cs[tc - 1:tc, :]


def _fox_cumsum(f_raw, b_pad, tc=512):
    s = f_raw.shape[0]
    tc = min(tc, s)
    return pl.pallas_call(
        functools.partial(_fox_cum_body, tc=tc),
        grid=(s // tc,),
        in_specs=[pl.BlockSpec((tc, LANES), lambda m: (m, 0)), pl.BlockSpec((1, LANES), lambda m: (0, 0))],
        out_specs=pl.BlockSpec((tc, LANES), lambda m: (m, 0)),
        out_shape=jax.ShapeDtypeStruct((s, LANES), F32),
        scratch_shapes=[pltpu.VMEM((1, LANES), F32)],
        compiler_params=_cparams("arbitrary"),
        name="fox_cumsum",
    )(f_raw, b_pad)


def _rope_tables(positions, rot_dim, period):
    half = rot_dim // 2
    inv_freq = ROPE_THETA ** (-jnp.arange(0, rot_dim, 2, dtype=F32) / rot_dim)
    ang = positions.astype(F32)[:, None] * inv_freq
    cos, sin = jnp.cos(ang), jnp.sin(ang)
    lp = jnp.arange(LANES) % period
    in_x1 = lp < half
    in_x2 = (lp >= half) & (lp < 2 * half)
    idx = jnp.where(in_x1, lp, jnp.clip(lp - half, 0, half - 1))
    cos_l, sin_l = cos[:, idx], sin[:, idx]
    c = jnp.where(in_x1 | in_x2, cos_l, 1.0)
    s1 = jnp.where(in_x2, sin_l, 0.0)
    s2 = jnp.where(in_x1, -sin_l, 0.0)
    return c, s1, s2


def _pad_cols(w, width):
    return jnp.pad(w, ((0, 0), (0, width - w.shape[1])))


def _mla_mixer(h, tabs, w_in, q_a_g, kv_a_g, w_q_b, w_kv_b, q_g, k_g):
    s = h.shape[0]
    nope = HEAD_DIM
    scale = MLA_HEAD ** -0.5 * LOG2E
    gains = jnp.concatenate([q_a_g, kv_a_g])
    c_norm = _proj(h, w_in, 0, 0, 2 * MLA_LORA, tn=MLA_LORA, gain=gains, gw=MLA_LORA, real=MLA_LORA)
    kpe = _proj(h, _pad_cols(w_in[0][:, 2 * MLA_LORA:], LANES)[None], 0, 0, LANES, tn=LANES, out_dtype=F32,
                epi=_epi_plain)
    wq = w_q_b[0].reshape(MLA_LORA, N_HEADS, MLA_HEAD)
    wq = jnp.concatenate([wq[:, :, MLA_ROPE:], wq[:, :, :MLA_ROPE],
                          jnp.zeros((MLA_LORA, N_HEADS, 2 * LANES - MLA_HEAD), F32)], axis=2)
    wq = wq.reshape(1, MLA_LORA, N_HEADS * 2 * LANES)
    qg = jnp.concatenate([q_g[MLA_ROPE:], q_g[:MLA_ROPE], jnp.zeros((2 * LANES - MLA_HEAD,), F32)])
    q = _proj(c_norm, wq, 0, 0, N_HEADS * 2 * LANES, tn=512, gain=jnp.tile(qg, N_HEADS), tabs=tabs,
              gw=2 * LANES, real=MLA_HEAD, half=MLA_ROPE // 2, rope_groups=lambda g: g % 2 == 1, scale=scale)
    tm, tn = min(TM, s), TN
    aux = [(kpe, (tm, LANES), lambda n, m: (m, 0)),
           (k_g[MLA_ROPE:].reshape(1, nope), (1, nope), lambda n, m: (0, 0)),
           (jnp.pad(k_g[:MLA_ROPE], (0, LANES - MLA_ROPE)).reshape(1, LANES), (1, LANES), lambda n, m: (0, 0))]
    aux += _row_tab_aux(tabs, tm)
    outs = [((s, N_HEADS * 2 * LANES), BF16, (tm, tn), lambda n, m: (m, n)),
            ((s, N_HEADS * HEAD_DIM), BF16, (tm, tn // 2), lambda n, m: (m, n))]
    k, v = _mm(c_norm, [(w_kv_b, 0, 0)], aux, outs, _epi_mla_kv, tm=tm, tn=tn,
               n_tiles=N_HEADS * 2 * LANES // tn, a_blk=1, name="mla_kv")
    return _dense_attention(q, k, v, mode="mla", dq=2 * LANES)


def _dsa_mixer(h, tabs_head, tabs_idx, w_in, q_g, k_g, idx_k_g):
    s = h.shape[0]
    hd = N_HEADS * HEAD_DIM
    ih = IDX_HEADS * IDX_DIM
    half = HEAD_DIM // 8
    q = _proj(h, w_in, 0, 0, hd, tn=512, gain=jnp.tile(q_g, N_HEADS), tabs=tabs_head, half=half,
              scale=HEAD_DIM ** -0.5 * LOG2E)
    k = _proj(h, w_in, 0, hd, hd, tn=512, gain=jnp.tile(k_g, N_HEADS), tabs=tabs_head, half=half)
    v = _proj(h, w_in, 0, 2 * hd, hd, tn=512, epi=_epi_plain)
    iq = _proj(h, w_in, 0, 3 * hd, ih, tn=512, tabs=tabs_idx, half=IDX_DIM // 8)
    w_ik = w_in[0][:, 3 * hd + ih:3 * hd + ih + IDX_DIM]
    ik2 = _proj(h, jnp.concatenate([w_ik, w_ik], axis=1)[None], 0, 0, LANES, tn=LANES,
                gain=jnp.tile(idx_k_g, 2), tabs=tabs_idx, half=IDX_DIM // 8)
    w_iw = _pad_cols(w_in[0][:, 3 * hd + ih + IDX_DIM:], LANES)[None]
    iw = _proj(h, w_iw, 0, 0, LANES, tn=LANES, out_dtype=F32,
               epi=functools.partial(_epi_heads, gw=LANES, real=LANES, half=0, rope_groups=None,
                                     scale=IDX_HEADS ** -0.5 * IDX_DIM ** -0.5, use_gain=False, use_rope=False))
    bias = _dsa_select_bias(iq, ik2, iw, min(IDX_TOPK, s // 4))
    return _dsa_attention(q, k, v, bias)


def _diff_mixer(h, tabs, layer_idx, w_in, q_g, k_g, lam_params, subln_g):
    w = N_HEADS * 2 * DIFF_DIM
    half = DIFF_DIM // 8
    q = _proj(h, w_in, 0, 0, w, tn=512, gain=jnp.tile(q_g, 2 * N_HEADS), tabs=tabs, gw=DIFF_DIM, real=DIFF_DIM,
              half=half, scale=DIFF_DIM ** -0.5 * LOG2E)
    k = _proj(h, w_in, 0, w, w, tn=512, gain=jnp.tile(k_g, 2 * N_HEADS), tabs=tabs, gw=DIFF_DIM, real=DIFF_DIM,
              half=half)
    v = _proj(h, w_in, 0, 2 * w, w, tn=512, epi=_epi_plain)
    lam_init = 0.8 - 0.6 * math.exp(-0.3 * layer_idx)
    return _dense_attention(q, k, v, mode="diff", dq=HEAD_DIM, extra=(lam_params, subln_g.reshape(1, HEAD_DIM)),
                            lam_init=lam_init)


def _fox_mixer(h, w_in, b_f, q_g, k_g):
    s = h.shape[0]
    hd = N_HEADS * HEAD_DIM
    q = _proj(h, w_in, 0, 0, hd, tn=512, gain=jnp.tile(q_g, N_HEADS), scale=HEAD_DIM ** -0.5 * LOG2E)
    k = _proj(h, w_in, 0, hd, hd, tn=512, gain=jnp.tile(k_g, N_HEADS))
    v = _proj(h, w_in, 0, 2 * hd, hd, tn=512, epi=_epi_plain)
    w_f = _pad_cols(w_in[0][:, 3 * hd:3 * hd + N_HEADS], LANES)[None]
    f_raw = _proj(h, w_f, 0, 0, LANES, tn=LANES, out_dtype=F32, epi=_epi_plain)
    gate = _proj(h, w_in[:, :, 3 * hd + N_HEADS:], 0, 0, hd, tn=512, out_dtype=F32, epi=_epi_sigmoid)
    cum = _fox_cumsum(f_raw, jnp.pad(b_f, (0, LANES - N_HEADS)).reshape(1, LANES))
    cumt = cum[:, :N_HEADS].T.reshape(N_HEADS, 1, s)
    return _dense_attention(q, k, v, mode="fox", dq=HEAD_DIM, extra=(cum, cumt, gate))


def _swiglu_ffn(h, x, gate, w_gate_up, w_down, layer):
    s = h.shape[0]
    tm, tn = min(TM, s), TN
    outs = [((s, FFN_HIDDEN), BF16, (tm, tn), lambda n, m: (m, n))]
    act = _mm(h, [(w_gate_up, layer, 0), (w_gate_up, layer, FFN_HIDDEN // tn)], [], outs, _epi_swiglu,
              tm=tm, tn=tn, n_tiles=FFN_HIDDEN // tn, name="ffn_gate_up")[0]
    return _residual_mm(act, w_down, layer, x, gate, tm=TM_DEEP)


def kernel(x, c, positions, ln_mix_g, ln_ffn_g, ada_w, ada_b, ffn_w_gate_up, ffn_w_down, mla_w_in, mla_q_a_g, mla_kv_a_g, mla_w_q_b, mla_w_kv_b, mla_q_g, mla_k_g, mla_w_out, dsa_w_in, dsa_q_g, dsa_k_g, dsa_idx_k_g, dsa_w_out, diff_w_in, diff_q_g, diff_k_g, diff_lambda_q1, diff_lambda_k1, diff_lambda_q2, diff_lambda_k2, diff_subln_g, diff_w_out, fox_w_in, fox_b_f, fox_q_g, fox_k_g, fox_w_out):
    batch, s, d = x.shape
    assert batch == 1 and d == D_MODEL
    depth = ada_w.shape[0]
    pos = positions[0]
    tabs_head = _rope_tables(pos, HEAD_DIM // 4, LANES)
    tabs_small = _rope_tables(pos, IDX_DIM // 4, IDX_DIM)
    tabs_mla = _rope_tables(pos, MLA_ROPE, LANES)
    mod = _ada_mod(c, ada_w, ada_b)
    xs = x[0]
    for i in range(depth):
        sh1, sc1, g1, sh2, sc2, g2 = [mod[i, t * d:(t + 1) * d] for t in range(6)]
        h = _normmod(xs, ln_mix_g[i], sc1, sh1)
        kind, j = i % 4, i // 4
        if kind == 0:
            o = _mla_mixer(h, tabs_mla, mla_w_in[j:j + 1], mla_q_a_g[j], mla_kv_a_g[j], mla_w_q_b[j:j + 1],
                           mla_w_kv_b[j:j + 1], mla_q_g[j], mla_k_g[j])
            w_out = mla_w_out
        elif kind == 1:
            o = _dsa_mixer(h, tabs_head, tabs_small, dsa_w_in[j:j + 1], dsa_q_g[j], dsa_k_g[j], dsa_idx_k_g[j])
            w_out = dsa_w_out
        elif kind == 2:
            lam_params = jnp.stack([diff_lambda_q1[j], diff_lambda_k1[j], diff_lambda_q2[j], diff_lambda_k2[j]])
            o = _diff_mixer(h, tabs_small, i, diff_w_in[j:j + 1], diff_q_g[j], diff_k_g[j], lam_params,
                            diff_subln_g[j])
            w_out = diff_w_out
        else:
            o = _fox_mixer(h, fox_w_in[j:j + 1], fox_b_f[j], fox_q_g[j], fox_k_g[j])
            w_out = fox_w_out
        xs = _residual_mm(o, w_out, j, xs, g1)
        h = _normmod(xs, ln_ffn_g[i], sc2, sh2)
        xs = _swiglu_ffn(h, xs, g2, ffn_w_gate_up, ffn_w_down, i)
    return xs[None]
```

```python
import functools
import math

import jax
import jax.numpy as jnp
from jax import lax
from jax.experimental import pallas as pl
from jax.experimental.pallas import tpu as pltpu

F32 = jnp.float32
BF16 = jnp.bfloat16

D_MODEL = 2048
N_HEADS = 16
HEAD_DIM = 128
ROPE_THETA = 500000.0
RMS_EPS = 1e-6
FFN_HIDDEN = 5632
MLA_LORA = 512
MLA_ROPE = 64
MLA_HEAD = 192
IDX_HEADS = 16
IDX_DIM = 64
IDX_TOPK = 256
DIFF_DIM = 64

LANES = 128
LOG2E = 1.4426950408889634
NEG = -1e30
VMEM_LIMIT_BYTES = 56 * 1024 * 1024
TM = 1024
TM_DEEP = 512
SUB_ROWS = 256
TN = 512
TN_WIDE = 1024
HEADS_PER_STEP = 2
BISECT_ROWS = 128
INT_MIN = -2 ** 31
KEY_NEG_INF = (0xFF800000 ^ 0x7FFFFFFF) - 2 ** 32


def _cparams(*sem):
    return pltpu.CompilerParams(dimension_semantics=sem, vmem_limit_bytes=VMEM_LIMIT_BYTES)


def _dot_nt(a, b):
    return lax.dot_general(a, b, (((1,), (1,)), ((), ())), preferred_element_type=F32)


def _sigmoid(z):
    return 1.0 / (1.0 + jnp.exp(-z))


def _mm_body(*refs, n_w, n_aux, n_out, epi, sub):
    a_ref = refs[0]
    w_refs = refs[1:1 + n_w]
    aux_refs = refs[1 + n_w:1 + n_w + n_aux]
    out_refs = refs[1 + n_w + n_aux:1 + n_w + n_aux + n_out]
    wb_refs = refs[1 + n_w + n_aux + n_out:]
    tm = a_ref.shape[0]

    @pl.when(pl.program_id(1) == 0)
    def _cast_weights():
        for w_ref, wb_ref in zip(w_refs, wb_refs):
            wb_ref[...] = w_ref[...].astype(BF16)

    for r in range(tm // sub):
        rows = pl.ds(r * sub, sub)
        accs = [jnp.dot(a_ref[rows, :], wb_ref[...], preferred_element_type=F32) for wb_ref in wb_refs]
        epi(accs, [ref.at[rows] if ref.shape[0] == tm else ref for ref in aux_refs],
            [ref.at[rows] for ref in out_refs])


def _mm(a, ws, aux, outs, epi, *, tm, tn, n_tiles, a_blk=0, name="mm", sub=SUB_ROWS):
    s = a.shape[0]
    k = ws[0][0].shape[1]
    tm = min(tm, s)
    in_specs = [pl.BlockSpec((tm, k), lambda n, m: (m, a_blk))]
    operands = [a]
    for w, layer, off in ws:
        assert w.shape[1] == k
        in_specs.append(pl.BlockSpec((None, k, tn), lambda n, m, layer=layer, off=off: (layer, 0, n + off)))
        operands.append(w)
    for arr, block, imap in aux:
        in_specs.append(pl.BlockSpec(block, imap))
        operands.append(arr)
    out_shape = [jax.ShapeDtypeStruct(shape, dtype) for shape, dtype, _, _ in outs]
    out_specs = [pl.BlockSpec(block, imap) for _, _, block, imap in outs]
    body = functools.partial(_mm_body, n_w=len(ws), n_aux=len(aux), n_out=len(outs), epi=epi, sub=min(sub, tm))
    res = pl.pallas_call(
        body,
        grid=(n_tiles, s // tm),
        in_specs=in_specs,
        out_specs=out_specs,
        out_shape=out_shape,
        scratch_shapes=[pltpu.VMEM((k, tn), BF16) for _ in ws],
        compiler_params=_cparams("arbitrary", "arbitrary"),
        name=name,
    )(*operands)
    return res


def _rope_group(y, tabs, half):
    c, s1, s2 = tabs
    return y * c + pltpu.roll(y, half, 1) * s1 + pltpu.roll(y, LANES - half, 1) * s2


def _group_inv_rms(y, gw, real):
    tn = y.shape[1]
    sq = y * y
    n_groups = tn // LANES
    if gw == DIFF_DIM:
        lane = lax.broadcasted_iota(jnp.int32, (1, LANES), 1)
        lo = lane < DIFF_DIM
        out = []
        for g in range(n_groups):
            sg = sq[:, g * LANES:(g + 1) * LANES]
            s_lo = jnp.sum(jnp.where(lo, sg, 0.0), axis=1, keepdims=True)
            s_hi = jnp.sum(jnp.where(lo, 0.0, sg), axis=1, keepdims=True)
            out.append(lax.rsqrt(jnp.where(lo, s_lo, s_hi) * (1.0 / real) + RMS_EPS))
        return out
    sums = [jnp.sum(sq[:, g * LANES:(g + 1) * LANES], axis=1, keepdims=True) for g in range(n_groups)]
    per = gw // LANES
    out = []
    for h in range(n_groups // per):
        tot = sums[h * per]
        for t in range(1, per):
            tot = tot + sums[h * per + t]
        r = lax.rsqrt(tot * (1.0 / real) + RMS_EPS)
        out.extend([r] * per)
    return out


def _epi_heads(accs, aux_refs, out_refs, *, gw, real, half, rope_groups, scale, use_gain, use_rope):
    y = accs[0]
    tn = y.shape[1]
    idx = 0
    gain = None
    if use_gain:
        gain = aux_refs[idx][...]
        idx += 1
        inv = _group_inv_rms(y, gw, real)
    if use_rope:
        tabs = tuple(aux_refs[idx + t][...] for t in range(3))
    for g in range(tn // LANES):
        yg = y[:, g * LANES:(g + 1) * LANES]
        if use_gain:
            yg = yg * inv[g] * gain[:, g * LANES:(g + 1) * LANES]
        if use_rope and rope_groups(g):
            yg = _rope_group(yg, tabs, half)
        if scale != 1.0:
            yg = yg * scale
        out_refs[0][:, g * LANES:(g + 1) * LANES] = yg.astype(out_refs[0].dtype)


def _epi_plain(accs, aux_refs, out_refs):
    out_refs[0][...] = accs[0].astype(out_refs[0].dtype)


def _epi_sigmoid(accs, aux_refs, out_refs):
    out_refs[0][...] = _sigmoid(accs[0]).astype(out_refs[0].dtype)


def _epi_residual(accs, aux_refs, out_refs):
    x_ref, g_ref = aux_refs
    out_refs[0][...] = x_ref[...] + g_ref[...] * accs[0]


def _epi_swiglu(accs, aux_refs, out_refs):
    gate, up = accs
    out_refs[0][...] = (gate * _sigmoid(gate) * up).astype(out_refs[0].dtype)


def _epi_mla_kv(accs, aux_refs, out_refs):
    kpe_ref, g_nope_ref, g_pe_ref, c_ref, s1_ref, s2_ref = aux_refs
    k_ref, v_ref = out_refs
    y = accs[0]
    kpe = kpe_ref[...]
    ss_pe = jnp.sum(kpe * kpe, axis=1, keepdims=True)
    tabs = (c_ref[...], s1_ref[...], s2_ref[...])
    g_nope = g_nope_ref[...]
    g_pe = g_pe_ref[...]
    for h in range(y.shape[1] // (2 * LANES)):
        kn = y[:, 2 * h * LANES:(2 * h + 1) * LANES]
        v = y[:, (2 * h + 1) * LANES:(2 * h + 2) * LANES]
        r = lax.rsqrt((jnp.sum(kn * kn, axis=1, keepdims=True) + ss_pe) * (1.0 / MLA_HEAD) + RMS_EPS)
        k_ref[:, 2 * h * LANES:(2 * h + 1) * LANES] = (kn * r * g_nope).astype(k_ref.dtype)
        pe = _rope_group(kpe * r * g_pe, tabs, MLA_ROPE // 2)
        k_ref[:, (2 * h + 1) * LANES:(2 * h + 2) * LANES] = pe.astype(k_ref.dtype)
        v_ref[:, h * LANES:(h + 1) * LANES] = v.astype(v_ref.dtype)


def _row_tab_aux(tabs, tm):
    return [(t, (tm, LANES), lambda n, m: (m, 0)) for t in tabs]


def _proj(a, w, layer, col_off, width, *, tn, out_dtype=BF16, gain=None, tabs=None, gw=LANES, real=LANES,
          half=0, rope_groups=lambda g: True, scale=1.0, epi=None, tm=TM, a_blk=0):
    s = a.shape[0]
    tm = min(tm, s)
    assert col_off % tn == 0 and width % tn == 0
    aux = []
    if gain is not None:
        aux.append((gain.reshape(1, width).astype(F32), (1, tn), lambda n, m: (0, n)))
    if tabs is not None:
        aux.extend(_row_tab_aux(tabs, tm))
    if epi is None:
        epi = functools.partial(_epi_heads, gw=gw, real=real, half=half, rope_groups=rope_groups, scale=scale,
                                use_gain=gain is not None, use_rope=tabs is not None)
    outs = [((s, width), out_dtype, (tm, tn), lambda n, m: (m, n))]
    return _mm(a, [(w, layer, col_off // tn)], aux, outs, epi, tm=tm, tn=tn, n_tiles=width // tn, a_blk=a_blk,
               name="proj")[0]


def _residual_mm(a, w, layer, x, gate, *, tn=TN, tm=TM):
    s, d = x.shape
    tm = min(tm, s)
    aux = [(x, (tm, tn), lambda n, m: (m, n)), (gate.reshape(1, d), (1, tn), lambda n, m: (0, n))]
    outs = [((s, d), F32, (tm, tn), lambda n, m: (m, n))]
    return _mm(a, [(w, layer, 0)], aux, outs, _epi_residual, tm=tm, tn=tn, n_tiles=d // tn, name="residual_mm")[0]


def _ada_body(c_ref, w_ref, b_ref, o_ref):
    c = c_ref[...]
    cond = c * _sigmoid(c)
    o_ref[...] = jnp.dot(cond.astype(BF16), w_ref[...].astype(BF16), preferred_element_type=F32) + b_ref[...]


def _ada_mod(c, ada_w, ada_b):
    depth, d, n = ada_w.shape
    tn = 1536
    c8 = jnp.broadcast_to(c, (8, d))
    out = pl.pallas_call(
        _ada_body,
        grid=(depth, n // tn),
        in_specs=[pl.BlockSpec((8, d), lambda l, j: (0, 0)),
                  pl.BlockSpec((None, d, tn), lambda l, j: (l, 0, j)),
                  pl.BlockSpec((None, 1, tn), lambda l, j: (l, 0, j))],
        out_specs=pl.BlockSpec((None, 8, tn), lambda l, j: (l, 0, j)),
        out_shape=jax.ShapeDtypeStruct((depth, 8, n), F32),
        compiler_params=_cparams("arbitrary", "arbitrary"),
        name="ada_mod",
    )(c8, ada_w, ada_b.reshape(depth, 1, n))
    return out[:, 0, :]


def _normmod_body(x_ref, g_ref, sc_ref, sh_ref, o_ref):
    x = x_ref[...]
    y = x * lax.rsqrt(jnp.mean(x * x, axis=1, keepdims=True) + RMS_EPS) * g_ref[...]
    o_ref[...] = (y * (1.0 + sc_ref[...]) + sh_ref[...]).astype(o_ref.dtype)


def _normmod(x, g, sc, sh, tm=256):
    s, d = x.shape
    tm = min(tm, s)
    vec = pl.BlockSpec((1, d), lambda m: (0, 0))
    return pl.pallas_call(
        _normmod_body,
        grid=(s // tm,),
        in_specs=[pl.BlockSpec((tm, d), lambda m: (m, 0)), vec, vec, vec],
        out_specs=pl.BlockSpec((tm, d), lambda m: (m, 0)),
        out_shape=jax.ShapeDtypeStruct((s, d), BF16),
        compiler_params=_cparams("arbitrary"),
        name="normmod",
    )(x, g.reshape(1, d), sc.reshape(1, d), sh.reshape(1, d))


def _softmax_scratch(n_maps, tq, tk):
    return [pltpu.VMEM((n_maps, tq, 2 * HEAD_DIM), F32), pltpu.VMEM((n_maps, tq, LANES), F32),
            pltpu.VMEM((n_maps, 2, tq, tk), F32), pltpu.VMEM((n_maps, 2, 2, tq, LANES), F32)]


def _softmax_pipeline(qs, kv_cols, k_ref, v_ref, scratch, *, tq, tk, first_chunk, first_keep, n_rest, bias_fn):
    acc_ref, m_ref, s_ref, st_ref = scratch
    ones = jnp.ones((tk, LANES), BF16)
    n_grp = tk // LANES
    maps = range(len(qs))

    def chunk_start(c):
        return pl.multiple_of(c * tk, tk)

    def stage1(c, slot, keep):
        start = chunk_start(c)
        for t in maps:
            s = _dot_nt(qs[t], k_ref[pl.ds(start, tk), kv_cols[t][0]])
            if bias_fn is not None:
                s = s + bias_fn(t, start)
            if keep is not None:
                s = jnp.where(keep, s, NEG)
            m_run = m_ref[t]
            m_new = jnp.maximum(m_run, jnp.max(s, axis=1, keepdims=True))
            s_ref[t, slot] = s
            st_ref[t, slot, 0] = m_new
            st_ref[t, slot, 1] = jnp.exp2(m_run - m_new)
            m_ref[t] = m_new

    def stage2(c, slot):
        start = chunk_start(c)
        for t in maps:
            v1 = jnp.concatenate([v_ref[pl.ds(start, tk), kv_cols[t][1]], ones], axis=1)
            m_new = st_ref[t, slot, 0]
            alpha = st_ref[t, slot, 1]
            p = jnp.concatenate([jnp.exp2(s_ref[t, slot, :, g * LANES:(g + 1) * LANES] - m_new)
                                 for g in range(n_grp)], axis=1)
            pv = jnp.dot(p.astype(BF16), v1, preferred_element_type=F32)
            acc_ref[t] = jnp.concatenate([alpha, alpha], axis=1) * acc_ref[t] + pv

    acc_ref[...] = jnp.zeros(acc_ref.shape, F32)
    m_ref[...] = jnp.full(m_ref.shape, NEG, F32)
    stage1(first_chunk, 0, first_keep)

    def two_chunks(c0):
        stage1(c0, 1, None)
        stage2(jnp.where(c0 == 0, first_chunk, c0 - 1), 0)
        stage1(c0 + 1, 0, None)
        stage2(c0, 1)

    def body(t4, carry):
        two_chunks(4 * t4)
        two_chunks(4 * t4 + 2)
        return carry

    n_quads = n_rest // 4
    lax.fori_loop(0, n_quads, body, 0)
    n_pairs = n_rest // 2

    @pl.when(n_pairs > 2 * n_quads)
    def _pair_tail():
        two_chunks(4 * n_quads)

    pending = jnp.where(n_pairs == 0, first_chunk, 2 * n_pairs - 1)

    @pl.when(n_rest % 2 == 1)
    def _odd_tail():
        stage1(n_rest - 1, 1, None)
        stage2(pending, 0)
        stage2(n_rest - 1, 1)

    @pl.when(n_rest % 2 == 0)
    def _even_tail():
        stage2(pending, 0)


def _causal_keep(tq, tk, col_shift):
    row = lax.broadcasted_iota(jnp.int32, (tq, 1), 0)
    col = lax.broadcasted_iota(jnp.int32, (1, tk), 1)
    return row >= col + col_shift


def _dense_attn_body(*refs, mode, tq, tk, lam_init):
    if mode == "mla":
        q_ref, k_ref, v_ref, o_ref = refs[:4]
        scratch = refs[4:]
    elif mode == "fox":
        q_ref, k_ref, v_ref, cum_ref, cumt_ref, gate_ref, o_ref = refs[:7]
        scratch = refs[7:]
    else:
        q_ref, k_ref, v_ref, lam_ref, subg_ref, o_ref = refs[:6]
        scratch = refs[6:]
    hg = pl.program_id(0)
    i = pl.program_id(1)
    acc_ref = scratch[0]
    dq = q_ref.shape[1] // HEADS_PER_STEP
    heads = range(HEADS_PER_STEP)
    lane = lax.broadcasted_iota(jnp.int32, (1, LANES), 1)

    def head_cols(hh, width=HEAD_DIM):
        return slice(hh * width, (hh + 1) * width)

    qs, kv_cols = [], []
    for hh in heads:
        q = q_ref[:, head_cols(hh, dq)]
        if mode == "diff":
            zero = jnp.zeros_like(q)
            qs += [jnp.where(lane < DIFF_DIM, q, zero), jnp.where(lane >= DIFF_DIM, q, zero)]
            kv_cols += [(head_cols(hh, dq), head_cols(hh))] * 2
        else:
            qs.append(q)
            kv_cols.append((head_cols(hh, dq), head_cols(hh)))
    if mode == "fox":
        cum = cum_ref[...]
        cqs = [jnp.sum(jnp.where(lane == hg * HEADS_PER_STEP + hh, cum, 0.0), axis=1, keepdims=True) * LOG2E
               for hh in heads]

        def bias_fn(t, start):
            return cqs[t] - cumt_ref[t, :, pl.ds(start, tk)] * LOG2E
    else:
        bias_fn = None

    _softmax_pipeline(qs, kv_cols, k_ref, v_ref, scratch, tq=tq, tk=tk, first_chunk=i,
                      first_keep=_causal_keep(tq, tk, 0), n_rest=i, bias_fn=bias_fn)

    def normalised(t):
        acc = acc_ref[t]
        return acc[:, :HEAD_DIM] / acc[:, HEAD_DIM:]

    for hh in heads:
        if mode == "diff":
            lp = lam_ref[...]
            lam = (jnp.exp(jnp.sum(lp[0:1] * lp[1:2], axis=1, keepdims=True))
                   - jnp.exp(jnp.sum(lp[2:3] * lp[3:4], axis=1, keepdims=True)) + lam_init)
            o = normalised(2 * hh) - lam * normalised(2 * hh + 1)
            o = o * lax.rsqrt(jnp.mean(o * o, axis=1, keepdims=True) + RMS_EPS) * subg_ref[...]
            o = o * (1.0 - lam_init)
        else:
            o = normalised(hh)
            if mode == "fox":
                o = o * gate_ref[:, head_cols(hh)]
        o_ref[:, head_cols(hh)] = o.astype(o_ref.dtype)


def _dense_attention(q, k, v, *, mode, dq, extra=(), lam_init=0.0, tq=512):
    s = q.shape[0]
    tq = min(tq, s)
    tk = tq
    g = HEADS_PER_STEP
    in_specs = [pl.BlockSpec((tq, g * dq), lambda h, i: (i, h)),
                pl.BlockSpec((s, g * dq), lambda h, i: (0, h)),
                pl.BlockSpec((s, g * HEAD_DIM), lambda h, i: (0, h))]
    operands = [q, k, v]
    if mode == "fox":
        cum, cumt, gate = extra
        in_specs += [pl.BlockSpec((tq, LANES), lambda h, i: (i, 0)),
                     pl.BlockSpec((g, 1, s), lambda h, i: (h, 0, 0)),
                     pl.BlockSpec((tq, g * HEAD_DIM), lambda h, i: (i, h))]
        operands += [cum, cumt, gate]
    elif mode == "diff":
        lam_params, subln_g = extra
        in_specs += [pl.BlockSpec((4, DIFF_DIM), lambda h, i: (0, 0)),
                     pl.BlockSpec((1, HEAD_DIM), lambda h, i: (0, 0))]
        operands += [lam_params, subln_g]
    n_maps = g * (2 if mode == "diff" else 1)
    body = functools.partial(_dense_attn_body, mode=mode, tq=tq, tk=tk, lam_init=lam_init)
    return pl.pallas_call(
        body,
        grid=(N_HEADS // g, s // tq),
        in_specs=in_specs,
        out_specs=pl.BlockSpec((tq, g * HEAD_DIM), lambda h, i: (i, h)),
        out_shape=jax.ShapeDtypeStruct((s, N_HEADS * HEAD_DIM), BF16),
        scratch_shapes=_softmax_scratch(n_maps, tq, tk),
        compiler_params=_cparams("arbitrary", "arbitrary"),
        name="attn_" + mode,
    )(*operands)


def _dsa_attn_body(q_ref, k_ref, v_ref, bias_ref, o_ref, *scratch, tq, tk):
    i = pl.program_id(0)

    def bias_fn(t, start):
        return bias_ref[:, pl.ds(start, tk)].astype(F32)

    cols = [slice(hh * HEAD_DIM, (hh + 1) * HEAD_DIM) for hh in range(HEADS_PER_STEP)]
    _softmax_pipeline([q_ref[:, c] for c in cols], [(c, c) for c in cols], k_ref, v_ref, scratch, tq=tq, tk=tk,
                      first_chunk=i, first_keep=None, n_rest=i, bias_fn=bias_fn)
    for t, c in enumerate(cols):
        acc = scratch[0][t]
        o_ref[:, c] = (acc[:, :HEAD_DIM] / acc[:, HEAD_DIM:]).astype(o_ref.dtype)


def _dsa_attention(q, k, v, bias, *, tq=512):
    s = q.shape[0]
    tq = min(tq, s)
    tk = tq
    body = functools.partial(_dsa_attn_body, tq=tq, tk=tk)
    return pl.pallas_call(
        body,
        grid=(s // tq, N_HEADS // HEADS_PER_STEP),
        in_specs=[pl.BlockSpec((tq, HEADS_PER_STEP * HEAD_DIM), lambda i, h: (i, h)),
                  pl.BlockSpec((s, HEADS_PER_STEP * HEAD_DIM), lambda i, h: (0, h)),
                  pl.BlockSpec((s, HEADS_PER_STEP * HEAD_DIM), lambda i, h: (0, h)),
                  pl.BlockSpec((tq, s), lambda i, h: (i, 0))],
        out_specs=pl.BlockSpec((tq, HEADS_PER_STEP * HEAD_DIM), lambda i, h: (i, h)),
        out_shape=jax.ShapeDtypeStruct((s, N_HEADS * HEAD_DIM), BF16),
        scratch_shapes=_softmax_scratch(HEADS_PER_STEP, tq, tk),
        compiler_params=_cparams("arbitrary", "arbitrary"),
        name="attn_dsa",
    )(q, k, v, bias)


def _sortable_key(score):
    bits = lax.bitcast_convert_type(score, jnp.int32)
    return bits ^ ((bits >> 31) & 0x7FFFFFFF)


def _indexer_body(iq_ref, ik_ref, iw_ref, bias_ref, keys_ref, qz_ref, wb_ref, thr_ref, *, tq, n_sel):
    i = pl.program_id(0)
    s_len = ik_ref.shape[0]
    n_chunks = s_len // tq
    lane = lax.broadcasted_iota(jnp.int32, (1, LANES), 1)
    iw = iw_ref[...]
    for p in range(IDX_HEADS // 2):
        pair = iq_ref[:, p * LANES:(p + 1) * LANES]
        zero = jnp.zeros_like(pair)
        qz_ref[2 * p] = jnp.where(lane < IDX_DIM, pair, zero)
        qz_ref[2 * p + 1] = jnp.where(lane >= IDX_DIM, pair, zero)
    for hh in range(IDX_HEADS):
        wb_ref[hh] = jnp.broadcast_to(iw[:, hh:hh + 1], (tq, LANES))

    def score_chunk(start):
        ik = ik_ref[pl.ds(start, tq), :]
        sc = jnp.zeros((tq, tq), F32)
        for hh in range(IDX_HEADS):
            rel = jnp.maximum(_dot_nt(qz_ref[hh], ik), 0.0)
            sc = sc + jnp.tile(wb_ref[hh], (1, tq // LANES)) * rel
        return sc

    def full_body(j, carry):
        start = pl.multiple_of(j * tq, tq)
        keys_ref[:, pl.ds(start, tq)] = _sortable_key(score_chunk(start))
        return carry

    lax.fori_loop(0, i, full_body, 0)
    diag = pl.multiple_of(i * tq, tq)
    sc = jnp.where(_causal_keep(tq, tq, 0), score_chunk(diag), -jnp.inf)
    keys_ref[:, pl.ds(diag, tq)] = _sortable_key(sc)

    keys_ref[:, pl.ds(pl.multiple_of((i + 1) * tq, tq), tq)] = jnp.full((tq, tq), KEY_NEG_INF, jnp.int32)
    strips = [pl.ds(r0, BISECT_ROWS) for r0 in range(0, tq, BISECT_ROWS)]

    def bit_body(b, cands):
        bit = lax.shift_left(jnp.int32(1), 31 - b)
        trials = [cand ^ bit for cand in cands]
        counts = []
        for rows, trial in zip(strips, trials):
            def count_body(j, cnt, rows=rows, trial=trial):
                start = pl.multiple_of(j * (2 * tq), 2 * tq)
                for g in range(2 * tq // LANES):
                    kk = keys_ref[rows, pl.ds(start + g * LANES, LANES)]
                    cnt = cnt + (kk >= trial).astype(jnp.int32)
                return cnt

            counts.append(lax.fori_loop(0, (i + 2) // 2, count_body, jnp.zeros((BISECT_ROWS, LANES), jnp.int32)))
        return tuple(jnp.where(jnp.sum(cnt, axis=1, keepdims=True) >= n_sel, trial, cand)
                     for cnt, trial, cand in zip(counts, trials, cands))

    cands = lax.fori_loop(0, 32, bit_body,
                          tuple(jnp.full((BISECT_ROWS, LANES), INT_MIN, jnp.int32) for _ in strips))
    for rows, cand in zip(strips, cands):
        thr_ref[rows, :] = jnp.maximum(cand, KEY_NEG_INF + 1)

    def write_body(j, carry):
        start = pl.multiple_of(j * tq, tq)
        thr = thr_ref[...]
        for g in range(tq // LANES):
            cols = pl.ds(start + g * LANES, LANES)
            bias_ref[:, cols] = jnp.where(keys_ref[:, cols] >= thr, 0.0, NEG).astype(bias_ref.dtype)
        return carry

    lax.fori_loop(0, i + 1, write_body, 0)

    def fill_body(j, carry):
        bias_ref[:, pl.ds(pl.multiple_of(j * tq, tq), tq)] = jnp.full((tq, tq), NEG, bias_ref.dtype)
        return carry

    lax.fori_loop(i + 1, n_chunks, fill_body, 0)


def _dsa_select_bias(iq, ik2, iw, n_sel, tq=256):
    s = iq.shape[0]
    tq = min(tq, s)
    body = functools.partial(_indexer_body, tq=tq, n_sel=n_sel)
    return pl.pallas_call(
        body,
        grid=(s // tq,),
        in_specs=[pl.BlockSpec((tq, IDX_HEADS * IDX_DIM), lambda i: (i, 0)),
                  pl.BlockSpec((s, LANES), lambda i: (0, 0)),
                  pl.BlockSpec((tq, LANES), lambda i: (i, 0))],
        out_specs=pl.BlockSpec((tq, s), lambda i: (i, 0)),
        out_shape=jax.ShapeDtypeStruct((s, s), BF16),
        scratch_shapes=[pltpu.VMEM((tq, s + tq), jnp.int32),
                        pltpu.VMEM((IDX_HEADS, tq, LANES), BF16),
                        pltpu.VMEM((IDX_HEADS, tq, LANES), F32),
                        pltpu.VMEM((tq, LANES), jnp.int32)],
        compiler_params=_cparams("arbitrary"),
        name="dsa_select",
    )(iq, ik2, iw)


def _fox_cum_body(f_ref, b_ref, o_ref, carry_ref, *, tc):
    @pl.when(pl.program_id(0) == 0)
    def _():
        carry_ref[...] = jnp.zeros(carry_ref.shape, F32)

    z = f_ref[...] + b_ref[...]
    lf = jnp.minimum(z, 0.0) - jnp.log(1.0 + jnp.exp(-jnp.abs(z)))
    row = lax.broadcasted_iota(jnp.int32, (tc, tc), 0)
    col = lax.broadcasted_iota(jnp.int32, (tc, tc), 1)
    tri = (row >= col).astype(BF16)
    hi = lf.astype(BF16)
    r1 = lf - hi.astype(F32)
    mid = r1.astype(BF16)
    lo = (r1 - mid.astype(F32)).astype(BF16)
    cs = (jnp.dot(tri, hi, preferred_element_type=F32) + jnp.dot(tri, mid, preferred_element_type=F32)
          + jnp.dot(tri, lo, preferred_element_type=F32)) + carry_ref[...]
    o_ref[...] = cs
    carry_ref[...] = cs[tc - 1:tc, :]


def _fox_cumsum(f_raw, b_pad, tc=512):
    s = f_raw.shape[0]
    tc = min(tc, s)
    return pl.pallas_call(
        functools.partial(_fox_cum_body, tc=tc),
        grid=(s // tc,),
        in_specs=[pl.BlockSpec((tc, LANES), lambda m: (m, 0)), pl.BlockSpec((1, LANES), lambda m: (0, 0))],
        out_specs=pl.BlockSpec((tc, LANES), lambda m: (m, 0)),
        out_shape=jax.ShapeDtypeStruct((s, LANES), F32),
        scratch_shapes=[pltpu.VMEM((1, LANES), F32)],
        compiler_params=_cparams("arbitrary"),
        name="fox_cumsum",
    )(f_raw, b_pad)


def _rope_tables(positions, rot_dim, period):
    half = rot_dim // 2
    inv_freq = ROPE_THETA ** (-jnp.arange(0, rot_dim, 2, dtype=F32) / rot_dim)
    ang = positions.astype(F32)[:, None] * inv_freq
    cos, sin = jnp.cos(ang), jnp.sin(ang)
    lp = jnp.arange(LANES) % period
    in_x1 = lp < half
    in_x2 = (lp >= half) & (lp < 2 * half)
    idx = jnp.where(in_x1, lp, jnp.clip(lp - half, 0, half - 1))
    cos_l, sin_l = cos[:, idx], sin[:, idx]
    c = jnp.where(in_x1 | in_x2, cos_l, 1.0)
    s1 = jnp.where(in_x2, sin_l, 0.0)
    s2 = jnp.where(in_x1, -sin_l, 0.0)
    return c, s1, s2


def _pad_cols(w, width):
    return jnp.pad(w, ((0, 0), (0, width - w.shape[1])))


def _mla_mixer(h, tabs, w_in, q_a_g, kv_a_g, w_q_b, w_kv_b, q_g, k_g):
    s = h.shape[0]
    nope = HEAD_DIM
    scale = MLA_HEAD ** -0.5 * LOG2E
    gains = jnp.concatenate([q_a_g, kv_a_g])
    c_norm = _proj(h, w_in, 0, 0, 2 * MLA_LORA, tn=MLA_LORA, gain=gains, gw=MLA_LORA, real=MLA_LORA)
    kpe = _proj(h, _pad_cols(w_in[0][:, 2 * MLA_LORA:], LANES)[None], 0, 0, LANES, tn=LANES, out_dtype=F32,
                epi=_epi_plain)
    wq = w_q_b[0].reshape(MLA_LORA, N_HEADS, MLA_HEAD)
    wq = jnp.concatenate([wq[:, :, MLA_ROPE:], wq[:, :, :MLA_ROPE],
                          jnp.zeros((MLA_LORA, N_HEADS, 2 * LANES - MLA_HEAD), F32)], axis=2)
    wq = wq.reshape(1, MLA_LORA, N_HEADS * 2 * LANES)
    qg = jnp.concatenate([q_g[MLA_ROPE:], q_g[:MLA_ROPE], jnp.zeros((2 * LANES - MLA_HEAD,), F32)])
    q = _proj(c_norm, wq, 0, 0, N_HEADS * 2 * LANES, tn=TN_WIDE, gain=jnp.tile(qg, N_HEADS), tabs=tabs,
              gw=2 * LANES, real=MLA_HEAD, half=MLA_ROPE // 2, rope_groups=lambda g: g % 2 == 1, scale=scale)
    tm, tn = min(TM, s), TN
    aux = [(kpe, (tm, LANES), lambda n, m: (m, 0)),
           (k_g[MLA_ROPE:].reshape(1, nope), (1, nope), lambda n, m: (0, 0)),
           (jnp.pad(k_g[:MLA_ROPE], (0, LANES - MLA_ROPE)).reshape(1, LANES), (1, LANES), lambda n, m: (0, 0))]
    aux += _row_tab_aux(tabs, tm)
    outs = [((s, N_HEADS * 2 * LANES), BF16, (tm, tn), lambda n, m: (m, n)),
            ((s, N_HEADS * HEAD_DIM), BF16, (tm, tn // 2), lambda n, m: (m, n))]
    k, v = _mm(c_norm, [(w_kv_b, 0, 0)], aux, outs, _epi_mla_kv, tm=tm, tn=tn,
               n_tiles=N_HEADS * 2 * LANES // tn, a_blk=1, name="mla_kv")
    return _dense_attention(q, k, v, mode="mla", dq=2 * LANES)


def _dsa_mixer(h, tabs_head, tabs_idx, w_in, q_g, k_g, idx_k_g):
    s = h.shape[0]
    hd = N_HEADS * HEAD_DIM
    ih = IDX_HEADS * IDX_DIM
    half = HEAD_DIM // 8
    q = _proj(h, w_in, 0, 0, hd, tn=TN_WIDE, gain=jnp.tile(q_g, N_HEADS), tabs=tabs_head, half=half,
              scale=HEAD_DIM ** -0.5 * LOG2E)
    k = _proj(h, w_in, 0, hd, hd, tn=TN_WIDE, gain=jnp.tile(k_g, N_HEADS), tabs=tabs_head, half=half)
    v = _proj(h, w_in, 0, 2 * hd, hd, tn=TN_WIDE, epi=_epi_plain)
    iq = _proj(h, w_in, 0, 3 * hd, ih, tn=TN_WIDE, tabs=tabs_idx, half=IDX_DIM // 8)
    w_ik = w_in[0][:, 3 * hd + ih:3 * hd + ih + IDX_DIM]
    ik2 = _proj(h, jnp.concatenate([w_ik, w_ik], axis=1)[None], 0, 0, LANES, tn=LANES,
                gain=jnp.tile(idx_k_g, 2), tabs=tabs_idx, half=IDX_DIM // 8)
    w_iw = _pad_cols(w_in[0][:, 3 * hd + ih + IDX_DIM:], LANES)[None]
    iw = _proj(h, w_iw, 0, 0, LANES, tn=LANES, out_dtype=F32,
               epi=functools.partial(_epi_heads, gw=LANES, real=LANES, half=0, rope_groups=None,
                                     scale=IDX_HEADS ** -0.5 * IDX_DIM ** -0.5, use_gain=False, use_rope=False))
    bias = _dsa_select_bias(iq, ik2, iw, min(IDX_TOPK, s // 4))
    return _dsa_attention(q, k, v, bias)


def _diff_mixer(h, tabs, layer_idx, w_in, q_g, k_g, lam_params, subln_g):
    w = N_HEADS * 2 * DIFF_DIM
    half = DIFF_DIM // 8
    q = _proj(h, w_in, 0, 0, w, tn=TN_WIDE, gain=jnp.tile(q_g, 2 * N_HEADS), tabs=tabs, gw=DIFF_DIM, real=DIFF_DIM,
              half=half, scale=DIFF_DIM ** -0.5 * LOG2E)
    k = _proj(h, w_in, 0, w, w, tn=TN_WIDE, gain=jnp.tile(k_g, 2 * N_HEADS), tabs=tabs, gw=DIFF_DIM, real=DIFF_DIM,
              half=half)
    v = _proj(h, w_in, 0, 2 * w, w, tn=TN_WIDE, epi=_epi_plain)
    lam_init = 0.8 - 0.6 * math.exp(-0.3 * layer_idx)
    return _dense_attention(q, k, v, mode="diff", dq=HEAD_DIM, extra=(lam_params, subln_g.reshape(1, HEAD_DIM)),
                            lam_init=lam_init)


def _fox_mixer(h, w_in, b_f, q_g, k_g):
    s = h.shape[0]
    hd = N_HEADS * HEAD_DIM
    q = _proj(h, w_in, 0, 0, hd, tn=TN_WIDE, gain=jnp.tile(q_g, N_HEADS), scale=HEAD_DIM ** -0.5 * LOG2E)
    k = _proj(h, w_in, 0, hd, hd, tn=TN_WIDE, gain=jnp.tile(k_g, N_HEADS))
    v = _proj(h, w_in, 0, 2 * hd, hd, tn=TN_WIDE, epi=_epi_plain)
    w_f = _pad_cols(w_in[0][:, 3 * hd:3 * hd + N_HEADS], LANES)[None]
    f_raw = _proj(h, w_f, 0, 0, LANES, tn=LANES, out_dtype=F32, epi=_epi_plain)
    gate = _proj(h, w_in[:, :, 3 * hd + N_HEADS:], 0, 0, hd, tn=TN_WIDE, out_dtype=F32, epi=_epi_sigmoid)
    cum = _fox_cumsum(f_raw, jnp.pad(b_f, (0, LANES - N_HEADS)).reshape(1, LANES))
    cumt = cum[:, :N_HEADS].T.reshape(N_HEADS, 1, s)
    return _dense_attention(q, k, v, mode="fox", dq=HEAD_DIM, extra=(cum, cumt, gate))


def _swiglu_ffn(h, x, gate, w_gate_up, w_down, layer):
    s = h.shape[0]
    tm, tn = min(TM, s), TN
    outs = [((s, FFN_HIDDEN), BF16, (tm, tn), lambda n, m: (m, n))]
    act = _mm(h, [(w_gate_up, layer, 0), (w_gate_up, layer, FFN_HIDDEN // tn)], [], outs, _epi_swiglu,
              tm=tm, tn=tn, n_tiles=FFN_HIDDEN // tn, name="ffn_gate_up")[0]
    return _residual_mm(act, w_down, layer, x, gate, tm=TM_DEEP)


def kernel(x, c, positions, ln_mix_g, ln_ffn_g, ada_w, ada_b, ffn_w_gate_up, ffn_w_down, mla_w_in, mla_q_a_g, mla_kv_a_g, mla_w_q_b, mla_w_kv_b, mla_q_g, mla_k_g, mla_w_out, dsa_w_in, dsa_q_g, dsa_k_g, dsa_idx_k_g, dsa_w_out, diff_w_in, diff_q_g, diff_k_g, diff_lambda_q1, diff_lambda_k1, diff_lambda_q2, diff_lambda_k2, diff_subln_g, diff_w_out, fox_w_in, fox_b_f, fox_q_g, fox_k_g, fox_w_out):
    batch, s, d = x.shape
    assert batch == 1 and d == D_MODEL
    depth = ada_w.shape[0]
    pos = positions[0]
    tabs_head = _rope_tables(pos, HEAD_DIM // 4, LANES)
    tabs_small = _rope_tables(pos, IDX_DIM // 4, IDX_DIM)
    tabs_mla = _rope_tables(pos, MLA_ROPE, LANES)
    mod = _ada_mod(c, ada_w, ada_b)
    xs = x[0]
    for i in range(depth):
        sh1, sc1, g1, sh2, sc2, g2 = [mod[i, t * d:(t + 1) * d] for t in range(6)]
        h = _normmod(xs, ln_mix_g[i], sc1, sh1)
        kind, j = i % 4, i // 4
        if kind == 0:
            o = _mla_mixer(h, tabs_mla, mla_w_in[j:j + 1], mla_q_a_g[j], mla_kv_a_g[j], mla_w_q_b[j:j + 1],
                           mla_w_kv_b[j:j + 1], mla_q_g[j], mla_k_g[j])
            w_out = mla_w_out
        elif kind == 1:
            o = _dsa_mixer(h, tabs_head, tabs_small, dsa_w_in[j:j + 1], dsa_q_g[j], dsa_k_g[j], dsa_idx_k_g[j])
            w_out = dsa_w_out
        elif kind == 2:
            lam_params = jnp.stack([diff_lambda_q1[j], diff_lambda_k1[j], diff_lambda_q2[j], diff_lambda_k2[j]])
            o = _diff_mixer(h, tabs_small, i, diff_w_in[j:j + 1], diff_q_g[j], diff_k_g[j], lam_params,
                            diff_subln_g[j])
            w_out = diff_w_out
        else:
            o = _fox_mixer(h, fox_w_in[j:j + 1], fox_b_f[j], fox_q_g[j], fox_k_g[j])
            w_out = fox_w_out
        xs = _residual_mm(o, w_out, j, xs, g1, tn=TN_WIDE)
        h = _normmod(xs, ln_ffn_g[i], sc2, sh2)
        xs = _swiglu_ffn(h, xs, g2, ffn_w_gate_up, ffn_w_down, i)
    return xs[None]
```

```python
import functools
import math

import jax
import jax.numpy as jnp
from jax import lax
from jax.experimental import pallas as pl
from jax.experimental.pallas import tpu as pltpu

F32 = jnp.float32
BF16 = jnp.bfloat16

D_MODEL = 2048
N_HEADS = 16
HEAD_DIM = 128
ROPE_THETA = 500000.0
RMS_EPS = 1e-6
FFN_HIDDEN = 5632
MLA_LORA = 512
MLA_ROPE = 64
MLA_HEAD = 192
IDX_HEADS = 16
IDX_DIM = 64
IDX_TOPK = 256
DIFF_DIM = 64

LANES = 128
LOG2E = 1.4426950408889634
NEG = -1e30
VMEM_LIMIT_BYTES = 56 * 1024 * 1024
TM = 1024
TM_DEEP = 512
SUB_ROWS = 256
TN = 512
TN_WIDE = 1024
HEADS_PER_STEP = 2
BISECT_ROWS = 128
INT_MIN = -2 ** 31
KEY_NEG_INF = (0xFF800000 ^ 0x7FFFFFFF) - 2 ** 32


def _cparams(*sem):
    return pltpu.CompilerParams(dimension_semantics=sem, vmem_limit_bytes=VMEM_LIMIT_BYTES)


def _dot_nt(a, b):
    return lax.dot_general(a, b, (((1,), (1,)), ((), ())), preferred_element_type=F32)


def _sigmoid(z):
    return 1.0 / (1.0 + jnp.exp(-z))


def _mm_body(*refs, n_w, n_aux, n_out, epi, sub):
    a_ref = refs[0]
    w_refs = refs[1:1 + n_w]
    aux_refs = refs[1 + n_w:1 + n_w + n_aux]
    out_refs = refs[1 + n_w + n_aux:1 + n_w + n_aux + n_out]
    wb_refs = refs[1 + n_w + n_aux + n_out:]
    tm = a_ref.shape[0]

    @pl.when(pl.program_id(1) == 0)
    def _cast_weights():
        for w_ref, wb_ref in zip(w_refs, wb_refs):
            wb_ref[...] = w_ref[...].astype(BF16)

    for r in range(tm // sub):
        rows = pl.ds(r * sub, sub)
        accs = [jnp.dot(a_ref[rows, :], wb_ref[...], preferred_element_type=F32) for wb_ref in wb_refs]
        epi(accs, [ref.at[rows] if ref.shape[0] == tm else ref for ref in aux_refs],
            [ref.at[rows] for ref in out_refs])


def _mm(a, ws, aux, outs, epi, *, tm, tn, n_tiles, a_blk=0, name="mm", sub=SUB_ROWS):
    s = a.shape[0]
    k = ws[0][0].shape[1]
    tm = min(tm, s)
    in_specs = [pl.BlockSpec((tm, k), lambda n, m: (m, a_blk))]
    operands = [a]
    for w, layer, off in ws:
        assert w.shape[1] == k
        in_specs.append(pl.BlockSpec((None, k, tn), lambda n, m, layer=layer, off=off: (layer, 0, n + off)))
        operands.append(w)
    for arr, block, imap in aux:
        in_specs.append(pl.BlockSpec(block, imap))
        operands.append(arr)
    out_shape = [jax.ShapeDtypeStruct(shape, dtype) for shape, dtype, _, _ in outs]
    out_specs = [pl.BlockSpec(block, imap) for _, _, block, imap in outs]
    body = functools.partial(_mm_body, n_w=len(ws), n_aux=len(aux), n_out=len(outs), epi=epi, sub=min(sub, tm))
    res = pl.pallas_call(
        body,
        grid=(n_tiles, s // tm),
        in_specs=in_specs,
        out_specs=out_specs,
        out_shape=out_shape,
        scratch_shapes=[pltpu.VMEM((k, tn), BF16) for _ in ws],
        compiler_params=_cparams("arbitrary", "arbitrary"),
        name=name,
    )(*operands)
    return res


def _rope_group(y, tabs, half):
    c, s1, s2 = tabs
    return y * c + pltpu.roll(y, half, 1) * s1 + pltpu.roll(y, LANES - half, 1) * s2


def _group_inv_rms(y, gw, real):
    tn = y.shape[1]
    sq = y * y
    n_groups = tn // LANES
    if gw == DIFF_DIM:
        lane = lax.broadcasted_iota(jnp.int32, (1, LANES), 1)
        lo = lane < DIFF_DIM
        out = []
        for g in range(n_groups):
            sg = sq[:, g * LANES:(g + 1) * LANES]
            s_lo = jnp.sum(jnp.where(lo, sg, 0.0), axis=1, keepdims=True)
            s_hi = jnp.sum(jnp.where(lo, 0.0, sg), axis=1, keepdims=True)
            out.append(lax.rsqrt(jnp.where(lo, s_lo, s_hi) * (1.0 / real) + RMS_EPS))
        return out
    sums = [jnp.sum(sq[:, g * LANES:(g + 1) * LANES], axis=1, keepdims=True) for g in range(n_groups)]
    per = gw // LANES
    out = []
    for h in range(n_groups // per):
        tot = sums[h * per]
        for t in range(1, per):
            tot = tot + sums[h * per + t]
        r = lax.rsqrt(tot * (1.0 / real) + RMS_EPS)
        out.extend([r] * per)
    return out


def _epi_heads(accs, aux_refs, out_refs, *, gw, real, half, rope_groups, scale, use_gain, use_rope):
    y = accs[0]
    tn = y.shape[1]
    idx = 0
    gain = None
    if use_gain:
        gain = aux_refs[idx][...]
        idx += 1
        inv = _group_inv_rms(y, gw, real)
    if use_rope:
        tabs = tuple(aux_refs[idx + t][...] for t in range(3))
    for g in range(tn // LANES):
        yg = y[:, g * LANES:(g + 1) * LANES]
        if use_gain:
            yg = yg * inv[g] * gain[:, g * LANES:(g + 1) * LANES]
        if use_rope and rope_groups(g):
            yg = _rope_group(yg, tabs, half)
        if scale != 1.0:
            yg = yg * scale
        out_refs[0][:, g * LANES:(g + 1) * LANES] = yg.astype(out_refs[0].dtype)


def _epi_plain(accs, aux_refs, out_refs):
    out_refs[0][...] = accs[0].astype(out_refs[0].dtype)


def _epi_sigmoid(accs, aux_refs, out_refs):
    out_refs[0][...] = _sigmoid(accs[0]).astype(out_refs[0].dtype)


def _epi_residual(accs, aux_refs, out_refs):
    x_ref, g_ref = aux_refs
    out_refs[0][...] = x_ref[...] + g_ref[...] * accs[0]


def _epi_swiglu(accs, aux_refs, out_refs):
    gate, up = accs
    out_refs[0][...] = (gate * _sigmoid(gate) * up).astype(out_refs[0].dtype)


def _epi_mla_kv(accs, aux_refs, out_refs):
    kpe_ref, g_nope_ref, g_pe_ref, c_ref, s1_ref, s2_ref = aux_refs
    k_ref, v_ref = out_refs
    y = accs[0]
    kpe = kpe_ref[...]
    ss_pe = jnp.sum(kpe * kpe, axis=1, keepdims=True)
    tabs = (c_ref[...], s1_ref[...], s2_ref[...])
    g_nope = g_nope_ref[...]
    g_pe = g_pe_ref[...]
    for h in range(y.shape[1] // (2 * LANES)):
        kn = y[:, 2 * h * LANES:(2 * h + 1) * LANES]
        v = y[:, (2 * h + 1) * LANES:(2 * h + 2) * LANES]
        r = lax.rsqrt((jnp.sum(kn * kn, axis=1, keepdims=True) + ss_pe) * (1.0 / MLA_HEAD) + RMS_EPS)
        k_ref[:, 2 * h * LANES:(2 * h + 1) * LANES] = (kn * r * g_nope).astype(k_ref.dtype)
        pe = _rope_group(kpe * r * g_pe, tabs, MLA_ROPE // 2)
        k_ref[:, (2 * h + 1) * LANES:(2 * h + 2) * LANES] = pe.astype(k_ref.dtype)
        v_ref[:, h * LANES:(h + 1) * LANES] = v.astype(v_ref.dtype)


def _row_tab_aux(tabs, tm):
    return [(t, (tm, LANES), lambda n, m: (m, 0)) for t in tabs]


def _proj(a, w, layer, col_off, width, *, tn, out_dtype=BF16, gain=None, tabs=None, gw=LANES, real=LANES,
          half=0, rope_groups=lambda g: True, scale=1.0, epi=None, tm=TM, a_blk=0):
    s = a.shape[0]
    tm = min(tm, s)
    assert col_off % tn == 0 and width % tn == 0
    aux = []
    if gain is not None:
        aux.append((gain.reshape(1, width).astype(F32), (1, tn), lambda n, m: (0, n)))
    if tabs is not None:
        aux.extend(_row_tab_aux(tabs, tm))
    if epi is None:
        epi = functools.partial(_epi_heads, gw=gw, real=real, half=half, rope_groups=rope_groups, scale=scale,
                                use_gain=gain is not None, use_rope=tabs is not None)
    outs = [((s, width), out_dtype, (tm, tn), lambda n, m: (m, n))]
    return _mm(a, [(w, layer, col_off // tn)], aux, outs, epi, tm=tm, tn=tn, n_tiles=width // tn, a_blk=a_blk,
               name="proj")[0]


def _residual_mm(a, w, layer, x, gate, *, tn=TN, tm=TM):
    s, d = x.shape
    tm = min(tm, s)
    aux = [(x, (tm, tn), lambda n, m: (m, n)), (gate.reshape(1, d), (1, tn), lambda n, m: (0, n))]
    outs = [((s, d), F32, (tm, tn), lambda n, m: (m, n))]
    return _mm(a, [(w, layer, 0)], aux, outs, _epi_residual, tm=tm, tn=tn, n_tiles=d // tn, name="residual_mm")[0]


def _ada_body(c_ref, w_ref, b_ref, o_ref):
    c = c_ref[...]
    cond = c * _sigmoid(c)
    o_ref[...] = jnp.dot(cond.astype(BF16), w_ref[...].astype(BF16), preferred_element_type=F32) + b_ref[...]


def _ada_mod(c, ada_w, ada_b):
    depth, d, n = ada_w.shape
    tn = 1536
    c8 = jnp.broadcast_to(c, (8, d))
    out = pl.pallas_call(
        _ada_body,
        grid=(depth, n // tn),
        in_specs=[pl.BlockSpec((8, d), lambda l, j: (0, 0)),
                  pl.BlockSpec((None, d, tn), lambda l, j: (l, 0, j)),
                  pl.BlockSpec((None, 1, tn), lambda l, j: (l, 0, j))],
        out_specs=pl.BlockSpec((None, 8, tn), lambda l, j: (l, 0, j)),
        out_shape=jax.ShapeDtypeStruct((depth, 8, n), F32),
        compiler_params=_cparams("arbitrary", "arbitrary"),
        name="ada_mod",
    )(c8, ada_w, ada_b.reshape(depth, 1, n))
    return out[:, 0, :]


def _normmod_body(x_ref, g_ref, sc_ref, sh_ref, o_ref):
    x = x_ref[...]
    y = x * lax.rsqrt(jnp.mean(x * x, axis=1, keepdims=True) + RMS_EPS) * g_ref[...]
    o_ref[...] = (y * (1.0 + sc_ref[...]) + sh_ref[...]).astype(o_ref.dtype)


def _normmod(x, g, sc, sh, tm=TM):
    s, d = x.shape
    tm = min(tm, s)
    vec = pl.BlockSpec((1, d), lambda m: (0, 0))
    return pl.pallas_call(
        _normmod_body,
        grid=(s // tm,),
        in_specs=[pl.BlockSpec((tm, d), lambda m: (m, 0)), vec, vec, vec],
        out_specs=pl.BlockSpec((tm, d), lambda m: (m, 0)),
        out_shape=jax.ShapeDtypeStruct((s, d), BF16),
        compiler_params=_cparams("arbitrary"),
        name="normmod",
    )(x, g.reshape(1, d), sc.reshape(1, d), sh.reshape(1, d))


def _softmax_scratch(n_maps, tq, tk):
    return [pltpu.VMEM((n_maps, tq, 2 * HEAD_DIM), F32), pltpu.VMEM((n_maps, tq, LANES), F32),
            pltpu.VMEM((n_maps, 2, tq, tk), F32), pltpu.VMEM((n_maps, 2, 2, tq, LANES), F32)]


def _softmax_pipeline(qs, kv_cols, k_ref, v_ref, scratch, *, tq, tk, first_chunk, first_keep, n_rest, bias_fn):
    acc_ref, m_ref, s_ref, st_ref = scratch
    ones = jnp.ones((tk, LANES), BF16)
    n_grp = tk // LANES
    maps = range(len(qs))

    def chunk_start(c):
        return pl.multiple_of(c * tk, tk)

    def stage1(c, slot, keep):
        start = chunk_start(c)
        for t in maps:
            s = _dot_nt(qs[t], k_ref[pl.ds(start, tk), kv_cols[t][0]])
            if bias_fn is not None:
                s = s + bias_fn(t, start)
            if keep is not None:
                s = jnp.where(keep, s, NEG)
            m_run = m_ref[t]
            m_new = jnp.maximum(m_run, jnp.max(s, axis=1, keepdims=True))
            s_ref[t, slot] = s
            st_ref[t, slot, 0] = m_new
            st_ref[t, slot, 1] = jnp.exp2(m_run - m_new)
            m_ref[t] = m_new

    def stage2(c, slot):
        start = chunk_start(c)
        for t in maps:
            v1 = jnp.concatenate([v_ref[pl.ds(start, tk), kv_cols[t][1]], ones], axis=1)
            m_new = st_ref[t, slot, 0]
            alpha = st_ref[t, slot, 1]
            p = jnp.concatenate([jnp.exp2(s_ref[t, slot, :, g * LANES:(g + 1) * LANES] - m_new)
                                 for g in range(n_grp)], axis=1)
            pv = jnp.dot(p.astype(BF16), v1, preferred_element_type=F32)
            acc_ref[t] = jnp.concatenate([alpha, alpha], axis=1) * acc_ref[t] + pv

    acc_ref[...] = jnp.zeros(acc_ref.shape, F32)
    m_ref[...] = jnp.full(m_ref.shape, NEG, F32)
    stage1(first_chunk, 0, first_keep)

    def two_chunks(c0):
        stage1(c0, 1, None)
        stage2(jnp.where(c0 == 0, first_chunk, c0 - 1), 0)
        stage1(c0 + 1, 0, None)
        stage2(c0, 1)

    def body(t4, carry):
        two_chunks(4 * t4)
        two_chunks(4 * t4 + 2)
        return carry

    n_quads = n_rest // 4
    lax.fori_loop(0, n_quads, body, 0)
    n_pairs = n_rest // 2

    @pl.when(n_pairs > 2 * n_quads)
    def _pair_tail():
        two_chunks(4 * n_quads)

    pending = jnp.where(n_pairs == 0, first_chunk, 2 * n_pairs - 1)

    @pl.when(n_rest % 2 == 1)
    def _odd_tail():
        stage1(n_rest - 1, 1, None)
        stage2(pending, 0)
        stage2(n_rest - 1, 1)

    @pl.when(n_rest % 2 == 0)
    def _even_tail():
        stage2(pending, 0)


def _causal_keep(tq, tk, col_shift):
    row = lax.broadcasted_iota(jnp.int32, (tq, 1), 0)
    col = lax.broadcasted_iota(jnp.int32, (1, tk), 1)
    return row >= col + col_shift


def _dense_attn_body(*refs, mode, tq, tk, lam_init):
    if mode == "mla":
        q_ref, k_ref, v_ref, o_ref = refs[:4]
        scratch = refs[4:]
    elif mode == "fox":
        q_ref, k_ref, v_ref, cum_ref, cumt_ref, gate_ref, o_ref = refs[:7]
        scratch = refs[7:]
    else:
        q_ref, k_ref, v_ref, lam_ref, subg_ref, o_ref = refs[:6]
        scratch = refs[6:]
    hg = pl.program_id(0)
    i = pl.program_id(1)
    acc_ref = scratch[0]
    dq = q_ref.shape[1] // HEADS_PER_STEP
    heads = range(HEADS_PER_STEP)
    lane = lax.broadcasted_iota(jnp.int32, (1, LANES), 1)

    def head_cols(hh, width=HEAD_DIM):
        return slice(hh * width, (hh + 1) * width)

    qs, kv_cols = [], []
    for hh in heads:
        q = q_ref[:, head_cols(hh, dq)]
        if mode == "diff":
            zero = jnp.zeros_like(q)
            qs += [jnp.where(lane < DIFF_DIM, q, zero), jnp.where(lane >= DIFF_DIM, q, zero)]
            kv_cols += [(head_cols(hh, dq), head_cols(hh))] * 2
        else:
            qs.append(q)
            kv_cols.append((head_cols(hh, dq), head_cols(hh)))
    if mode == "fox":
        cum = cum_ref[...]
        cqs = [jnp.sum(jnp.where(lane == hg * HEADS_PER_STEP + hh, cum, 0.0), axis=1, keepdims=True) * LOG2E
               for hh in heads]

        def bias_fn(t, start):
            return cqs[t] - cumt_ref[t, :, pl.ds(start, tk)] * LOG2E
    else:
        bias_fn = None

    _softmax_pipeline(qs, kv_cols, k_ref, v_ref, scratch, tq=tq, tk=tk, first_chunk=i,
                      first_keep=_causal_keep(tq, tk, 0), n_rest=i, bias_fn=bias_fn)

    def normalised(t):
        acc = acc_ref[t]
        return acc[:, :HEAD_DIM] / acc[:, HEAD_DIM:]

    for hh in heads:
        if mode == "diff":
            lp = lam_ref[...]
            lam = (jnp.exp(jnp.sum(lp[0:1] * lp[1:2], axis=1, keepdims=True))
                   - jnp.exp(jnp.sum(lp[2:3] * lp[3:4], axis=1, keepdims=True)) + lam_init)
            o = normalised(2 * hh) - lam * normalised(2 * hh + 1)
            o = o * lax.rsqrt(jnp.mean(o * o, axis=1, keepdims=True) + RMS_EPS) * subg_ref[...]
            o = o * (1.0 - lam_init)
        else:
            o = normalised(hh)
            if mode == "fox":
                o = o * gate_ref[:, head_cols(hh)]
        o_ref[:, head_cols(hh)] = o.astype(o_ref.dtype)


def _dense_attention(q, k, v, *, mode, dq, extra=(), lam_init=0.0, tq=512):
    s = q.shape[0]
    tq = min(tq, s)
    tk = tq
    g = HEADS_PER_STEP
    in_specs = [pl.BlockSpec((tq, g * dq), lambda h, i: (i, h)),
                pl.BlockSpec((s, g * dq), lambda h, i: (0, h)),
                pl.BlockSpec((s, g * HEAD_DIM), lambda h, i: (0, h))]
    operands = [q, k, v]
    if mode == "fox":
        cum, cumt, gate = extra
        in_specs += [pl.BlockSpec((tq, LANES), lambda h, i: (i, 0)),
                     pl.BlockSpec((g, 1, s), lambda h, i: (h, 0, 0)),
                     pl.BlockSpec((tq, g * HEAD_DIM), lambda h, i: (i, h))]
        operands += [cum, cumt, gate]
    elif mode == "diff":
        lam_params, subln_g = extra
        in_specs += [pl.BlockSpec((4, DIFF_DIM), lambda h, i: (0, 0)),
                     pl.BlockSpec((1, HEAD_DIM), lambda h, i: (0, 0))]
        operands += [lam_params, subln_g]
    n_maps = g * (2 if mode == "diff" else 1)
    body = functools.partial(_dense_attn_body, mode=mode, tq=tq, tk=tk, lam_init=lam_init)
    return pl.pallas_call(
        body,
        grid=(N_HEADS // g, s // tq),
        in_specs=in_specs,
        out_specs=pl.BlockSpec((tq, g * HEAD_DIM), lambda h, i: (i, h)),
        out_shape=jax.ShapeDtypeStruct((s, N_HEADS * HEAD_DIM), BF16),
        scratch_shapes=_softmax_scratch(n_maps, tq, tk),
        compiler_params=_cparams("arbitrary", "arbitrary"),
        name="attn_" + mode,
    )(*operands)


def _dsa_attn_body(q_ref, k_ref, v_ref, bias_ref, o_ref, *scratch, tq, tk):
    i = pl.program_id(0)

    def bias_fn(t, start):
        return bias_ref[:, pl.ds(start, tk)].astype(F32)

    cols = [slice(hh * HEAD_DIM, (hh + 1) * HEAD_DIM) for hh in range(HEADS_PER_STEP)]
    _softmax_pipeline([q_ref[:, c] for c in cols], [(c, c) for c in cols], k_ref, v_ref, scratch, tq=tq, tk=tk,
                      first_chunk=i, first_keep=None, n_rest=i, bias_fn=bias_fn)
    for t, c in enumerate(cols):
        acc = scratch[0][t]
        o_ref[:, c] = (acc[:, :HEAD_DIM] / acc[:, HEAD_DIM:]).astype(o_ref.dtype)


def _dsa_attention(q, k, v, bias, *, tq=512):
    s = q.shape[0]
    tq = min(tq, s)
    tk = tq
    body = functools.partial(_dsa_attn_body, tq=tq, tk=tk)
    return pl.pallas_call(
        body,
        grid=(s // tq, N_HEADS // HEADS_PER_STEP),
        in_specs=[pl.BlockSpec((tq, HEADS_PER_STEP * HEAD_DIM), lambda i, h: (i, h)),
                  pl.BlockSpec((s, HEADS_PER_STEP * HEAD_DIM), lambda i, h: (0, h)),
                  pl.BlockSpec((s, HEADS_PER_STEP * HEAD_DIM), lambda i, h: (0, h)),
                  pl.BlockSpec((tq, s), lambda i, h: (i, 0))],
        out_specs=pl.BlockSpec((tq, HEADS_PER_STEP * HEAD_DIM), lambda i, h: (i, h)),
        out_shape=jax.ShapeDtypeStruct((s, N_HEADS * HEAD_DIM), BF16),
        scratch_shapes=_softmax_scratch(HEADS_PER_STEP, tq, tk),
        compiler_params=_cparams("arbitrary", "arbitrary"),
        name="attn_dsa",
    )(q, k, v, bias)


def _sortable_key(score):
    bits = lax.bitcast_convert_type(score, jnp.int32)
    return bits ^ ((bits >> 31) & 0x7FFFFFFF)


def _indexer_body(iq_ref, ik_ref, iw_ref, bias_ref, keys_ref, qz_ref, wb_ref, thr_ref, need_ref, *, tq, n_sel):
    i = pl.program_id(0)
    s_len = ik_ref.shape[0]
    n_chunks = s_len // tq
    lane = lax.broadcasted_iota(jnp.int32, (1, LANES), 1)
    iw = iw_ref[...]
    for p in range(IDX_HEADS // 2):
        pair = iq_ref[:, p * LANES:(p + 1) * LANES]
        zero = jnp.zeros_like(pair)
        qz_ref[2 * p] = jnp.where(lane < IDX_DIM, pair, zero)
        qz_ref[2 * p + 1] = jnp.where(lane >= IDX_DIM, pair, zero)
    for hh in range(IDX_HEADS):
        wb_ref[hh] = jnp.broadcast_to(iw[:, hh:hh + 1], (tq, LANES))

    def score_chunk(start):
        ik = ik_ref[pl.ds(start, tq), :]
        sc = jnp.zeros((tq, tq), F32)
        for hh in range(IDX_HEADS):
            rel = jnp.maximum(_dot_nt(qz_ref[hh], ik), 0.0)
            sc = sc + jnp.tile(wb_ref[hh], (1, tq // LANES)) * rel
        return sc

    def full_body(j, carry):
        start = pl.multiple_of(j * tq, tq)
        keys_ref[:, pl.ds(start, tq)] = _sortable_key(score_chunk(start))
        return carry

    lax.fori_loop(0, i, full_body, 0)
    diag = pl.multiple_of(i * tq, tq)
    sc = jnp.where(_causal_keep(tq, tq, 0), score_chunk(diag), -jnp.inf)
    keys_ref[:, pl.ds(diag, tq)] = _sortable_key(sc)

    keys_ref[:, pl.ds(pl.multiple_of((i + 1) * tq, tq), tq)] = jnp.full((tq, tq), KEY_NEG_INF, jnp.int32)
    strips = [pl.ds(r0, BISECT_ROWS) for r0 in range(0, tq, BISECT_ROWS)]

    def count_keys(rows, trial, strict):
        def count_body(j, cnt):
            start = pl.multiple_of(j * (2 * tq), 2 * tq)
            for g in range(2 * tq // LANES):
                kk = keys_ref[rows, pl.ds(start + g * LANES, LANES)]
                cnt = cnt + ((kk > trial) if strict else (kk >= trial)).astype(jnp.int32)
            return cnt

        return lax.fori_loop(0, (i + 2) // 2, count_body, jnp.zeros((BISECT_ROWS, LANES), jnp.int32))

    def bit_body(b, cands):
        bit = lax.shift_left(jnp.int32(1), 31 - b)
        trials = [cand ^ bit for cand in cands]
        counts = [count_keys(rows, trial, False) for rows, trial in zip(strips, trials)]
        return tuple(jnp.where(jnp.sum(cnt, axis=1, keepdims=True) >= n_sel, trial, cand)
                     for cnt, trial, cand in zip(counts, trials, cands))

    cands = lax.fori_loop(0, 32, bit_body,
                          tuple(jnp.full((BISECT_ROWS, LANES), INT_MIN, jnp.int32) for _ in strips))
    surplus = jnp.zeros((BISECT_ROWS, LANES), jnp.int32)
    for rows, cand in zip(strips, cands):
        thr = jnp.maximum(cand, KEY_NEG_INF)
        above = jnp.sum(count_keys(rows, thr, True), axis=1, keepdims=True)
        tied = jnp.sum(count_keys(rows, thr, False), axis=1, keepdims=True) - above
        need = jnp.where(cand > KEY_NEG_INF, n_sel - above, 0)
        thr_ref[rows, :] = thr
        need_ref[rows, :] = need.astype(F32)
        surplus = jnp.maximum(surplus, jnp.where(cand > KEY_NEG_INF, tied - need, 0))
    any_surplus = jnp.max(surplus) > 0
    n_grp = tq // LANES

    @pl.when(jnp.logical_not(any_surplus))
    def _write_no_ties():
        def write_body(j, carry):
            start = pl.multiple_of(j * tq, tq)
            thr = jnp.where(need_ref[...] > 0.0, thr_ref[...], KEY_NEG_INF + 1)
            for g in range(n_grp):
                cols = pl.ds(start + g * LANES, LANES)
                bias_ref[:, cols] = jnp.where(keys_ref[:, cols] >= thr, 0.0, NEG).astype(bias_ref.dtype)
            return carry

        lax.fori_loop(0, i + 1, write_body, 0)

    @pl.when(any_surplus)
    def _write_with_ties():
        ca = lax.broadcasted_iota(jnp.int32, (tq, tq), 0)
        cb = lax.broadcasted_iota(jnp.int32, (tq, tq), 1)
        prefix_ones = (ca <= cb).astype(BF16)
        all_ones = jnp.ones((tq, LANES), BF16)

        def write_body(j, seen):
            start = pl.multiple_of(j * tq, tq)
            kk = keys_ref[:, pl.ds(start, tq)]
            thr = jnp.concatenate([thr_ref[...]] * n_grp, axis=1)
            eq = kk == thr
            eq_b = jnp.where(eq, 1.0, 0.0).astype(BF16)
            rank = jnp.dot(eq_b, prefix_ones, preferred_element_type=F32) + jnp.concatenate([seen] * n_grp, axis=1)
            take = (kk > thr) | (eq & (rank <= jnp.concatenate([need_ref[...]] * n_grp, axis=1)))
            bias_ref[:, pl.ds(start, tq)] = jnp.where(take, 0.0, NEG).astype(bias_ref.dtype)
            return seen + jnp.dot(eq_b, all_ones, preferred_element_type=F32)

        lax.fori_loop(0, i + 1, write_body, jnp.zeros((tq, LANES), F32))

    def fill_body(j, carry):
        bias_ref[:, pl.ds(pl.multiple_of(j * tq, tq), tq)] = jnp.full((tq, tq), NEG, bias_ref.dtype)
        return carry

    lax.fori_loop(i + 1, n_chunks, fill_body, 0)


def _dsa_select_bias(iq, ik2, iw, n_sel, tq=256):
    s = iq.shape[0]
    tq = min(tq, s)
    body = functools.partial(_indexer_body, tq=tq, n_sel=n_sel)
    return pl.pallas_call(
        body,
        grid=(s // tq,),
        in_specs=[pl.BlockSpec((tq, IDX_HEADS * IDX_DIM), lambda i: (i, 0)),
                  pl.BlockSpec((s, LANES), lambda i: (0, 0)),
                  pl.BlockSpec((tq, LANES), lambda i: (i, 0))],
        out_specs=pl.BlockSpec((tq, s), lambda i: (i, 0)),
        out_shape=jax.ShapeDtypeStruct((s, s), BF16),
        scratch_shapes=[pltpu.VMEM((tq, s + tq), jnp.int32),
                        pltpu.VMEM((IDX_HEADS, tq, LANES), BF16),
                        pltpu.VMEM((IDX_HEADS, tq, LANES), F32),
                        pltpu.VMEM((tq, LANES), jnp.int32),
                        pltpu.VMEM((tq, LANES), F32)],
        compiler_params=_cparams("arbitrary"),
        name="dsa_select",
    )(iq, ik2, iw)


def _fox_cum_body(f_ref, b_ref, o_ref, carry_ref, *, tc):
    @pl.when(pl.program_id(0) == 0)
    def _():
        carry_ref[...] = jnp.zeros(carry_ref.shape, F32)

    z = f_ref[...] + b_ref[...]
    lf = jnp.minimum(z, 0.0) - jnp.log(1.0 + jnp.exp(-jnp.abs(z)))
    row = lax.broadcasted_iota(jnp.int32, (tc, tc), 0)
    col = lax.broadcasted_iota(jnp.int32, (tc, tc), 1)
    tri = (row >= col).astype(BF16)
    hi = lf.astype(BF16)
    r1 = lf - hi.astype(F32)
    mid = r1.astype(BF16)
    lo = (r1 - mid.astype(F32)).astype(BF16)
    cs = (jnp.dot(tri, hi, preferred_element_type=F32) + jnp.dot(tri, mid, preferred_element_type=F32)
          + jnp.dot(tri, lo, preferred_element_type=F32)) + carry_ref[...]
    o_ref[...] = cs
    carry_ref[...] = cs[tc - 1:tc, :]


def _fox_cumsum(f_raw, b_pad, tc=512):
    s = f_raw.shape[0]
    tc = min(tc, s)
    return pl.pallas_call(
        functools.partial(_fox_cum_body, tc=tc),
        grid=(s // tc,),
        in_specs=[pl.BlockSpec((tc, LANES), lambda m: (m, 0)), pl.BlockSpec((1, LANES), lambda m: (0, 0))],
        out_specs=pl.BlockSpec((tc, LANES), lambda m: (m, 0)),
        out_shape=jax.ShapeDtypeStruct((s, LANES), F32),
        scratch_shapes=[pltpu.VMEM((1, LANES), F32)],
        compiler_params=_cparams("arbitrary"),
        name="fox_cumsum",
    )(f_raw, b_pad)


def _rope_tables(positions, rot_dim, period):
    half = rot_dim // 2
    inv_freq = ROPE_THETA ** (-jnp.arange(0, rot_dim, 2, dtype=F32) / rot_dim)
    ang = positions.astype(F32)[:, None] * inv_freq
    cos, sin = jnp.cos(ang), jnp.sin(ang)
    lp = jnp.arange(LANES) % period
    in_x1 = lp < half
    in_x2 = (lp >= half) & (lp < 2 * half)
    idx = jnp.where(in_x1, lp, jnp.clip(lp - half, 0, half - 1))
    cos_l, sin_l = cos[:, idx], sin[:, idx]
    c = jnp.where(in_x1 | in_x2, cos_l, 1.0)
    s1 = jnp.where(in_x2, sin_l, 0.0)
    s2 = jnp.where(in_x1, -sin_l, 0.0)
    return c, s1, s2


def _pad_cols(w, width):
    return jnp.pad(w, ((0, 0), (0, width - w.shape[1])))


def _mla_mixer(h, tabs, w_in, q_a_g, kv_a_g, w_q_b, w_kv_b, q_g, k_g):
    s = h.shape[0]
    nope = HEAD_DIM
    scale = MLA_HEAD ** -0.5 * LOG2E
    gains = jnp.concatenate([q_a_g, kv_a_g])
    c_norm = _proj(h, w_in, 0, 0, 2 * MLA_LORA, tn=MLA_LORA, gain=gains, gw=MLA_LORA, real=MLA_LORA)
    kpe = _proj(h, _pad_cols(w_in[0][:, 2 * MLA_LORA:], LANES)[None], 0, 0, LANES, tn=LANES, out_dtype=F32,
                epi=_epi_plain)
    wq = w_q_b[0].reshape(MLA_LORA, N_HEADS, MLA_HEAD)
    wq = jnp.concatenate([wq[:, :, MLA_ROPE:], wq[:, :, :MLA_ROPE],
                          jnp.zeros((MLA_LORA, N_HEADS, 2 * LANES - MLA_HEAD), F32)], axis=2)
    wq = wq.reshape(1, MLA_LORA, N_HEADS * 2 * LANES)
    qg = jnp.concatenate([q_g[MLA_ROPE:], q_g[:MLA_ROPE], jnp.zeros((2 * LANES - MLA_HEAD,), F32)])
    q = _proj(c_norm, wq, 0, 0, N_HEADS * 2 * LANES, tn=TN_WIDE, gain=jnp.tile(qg, N_HEADS), tabs=tabs,
              gw=2 * LANES, real=MLA_HEAD, half=MLA_ROPE // 2, rope_groups=lambda g: g % 2 == 1, scale=scale)
    tm, tn = min(TM, s), TN
    aux = [(kpe, (tm, LANES), lambda n, m: (m, 0)),
           (k_g[MLA_ROPE:].reshape(1, nope), (1, nope), lambda n, m: (0, 0)),
           (jnp.pad(k_g[:MLA_ROPE], (0, LANES - MLA_ROPE)).reshape(1, LANES), (1, LANES), lambda n, m: (0, 0))]
    aux += _row_tab_aux(tabs, tm)
    outs = [((s, N_HEADS * 2 * LANES), BF16, (tm, tn), lambda n, m: (m, n)),
            ((s, N_HEADS * HEAD_DIM), BF16, (tm, tn // 2), lambda n, m: (m, n))]
    k, v = _mm(c_norm, [(w_kv_b, 0, 0)], aux, outs, _epi_mla_kv, tm=tm, tn=tn,
               n_tiles=N_HEADS * 2 * LANES // tn, a_blk=1, name="mla_kv")
    return _dense_attention(q, k, v, mode="mla", dq=2 * LANES)


def _dsa_mixer(h, tabs_head, tabs_idx, w_in, q_g, k_g, idx_k_g):
    s = h.shape[0]
    hd = N_HEADS * HEAD_DIM
    ih = IDX_HEADS * IDX_DIM
    half = HEAD_DIM // 8
    q = _proj(h, w_in, 0, 0, hd, tn=TN_WIDE, gain=jnp.tile(q_g, N_HEADS), tabs=tabs_head, half=half,
              scale=HEAD_DIM ** -0.5 * LOG2E)
    k = _proj(h, w_in, 0, hd, hd, tn=TN_WIDE, gain=jnp.tile(k_g, N_HEADS), tabs=tabs_head, half=half)
    v = _proj(h, w_in, 0, 2 * hd, hd, tn=TN_WIDE, epi=_epi_plain)
    iq = _proj(h, w_in, 0, 3 * hd, ih, tn=TN_WIDE, tabs=tabs_idx, half=IDX_DIM // 8)
    w_ik = w_in[0][:, 3 * hd + ih:3 * hd + ih + IDX_DIM]
    ik2 = _proj(h, jnp.concatenate([w_ik, w_ik], axis=1)[None], 0, 0, LANES, tn=LANES,
                gain=jnp.tile(idx_k_g, 2), tabs=tabs_idx, half=IDX_DIM // 8)
    w_iw = _pad_cols(w_in[0][:, 3 * hd + ih + IDX_DIM:], LANES)[None]
    iw = _proj(h, w_iw, 0, 0, LANES, tn=LANES, out_dtype=F32,
               epi=functools.partial(_epi_heads, gw=LANES, real=LANES, half=0, rope_groups=None,
                                     scale=IDX_HEADS ** -0.5 * IDX_DIM ** -0.5, use_gain=False, use_rope=False))
    bias = _dsa_select_bias(iq, ik2, iw, min(IDX_TOPK, s // 4))
    return _dsa_attention(q, k, v, bias)


def _diff_mixer(h, tabs, layer_idx, w_in, q_g, k_g, lam_params, subln_g):
    w = N_HEADS * 2 * DIFF_DIM
    half = DIFF_DIM // 8
    q = _proj(h, w_in, 0, 0, w, tn=TN_WIDE, gain=jnp.tile(q_g, 2 * N_HEADS), tabs=tabs, gw=DIFF_DIM, real=DIFF_DIM,
              half=half, scale=DIFF_DIM ** -0.5 * LOG2E)
    k = _proj(h, w_in, 0, w, w, tn=TN_WIDE, gain=jnp.tile(k_g, 2 * N_HEADS), tabs=tabs, gw=DIFF_DIM, real=DIFF_DIM,
              half=half)
    v = _proj(h, w_in, 0, 2 * w, w, tn=TN_WIDE, epi=_epi_plain)
    lam_init = 0.8 - 0.6 * math.exp(-0.3 * layer_idx)
    return _dense_attention(q, k, v, mode="diff", dq=HEAD_DIM, extra=(lam_params, subln_g.reshape(1, HEAD_DIM)),
                            lam_init=lam_init)


def _fox_mixer(h, w_in, b_f, q_g, k_g):
    s = h.shape[0]
    hd = N_HEADS * HEAD_DIM
    q = _proj(h, w_in, 0, 0, hd, tn=TN_WIDE, gain=jnp.tile(q_g, N_HEADS), scale=HEAD_DIM ** -0.5 * LOG2E)
    k = _proj(h, w_in, 0, hd, hd, tn=TN_WIDE, gain=jnp.tile(k_g, N_HEADS))
    v = _proj(h, w_in, 0, 2 * hd, hd, tn=TN_WIDE, epi=_epi_plain)
    w_f = _pad_cols(w_in[0][:, 3 * hd:3 * hd + N_HEADS], LANES)[None]
    f_raw = _proj(h, w_f, 0, 0, LANES, tn=LANES, out_dtype=F32, epi=_epi_plain)
    gate = _proj(h, w_in[:, :, 3 * hd + N_HEADS:], 0, 0, hd, tn=TN_WIDE, out_dtype=F32, epi=_epi_sigmoid)
    cum = _fox_cumsum(f_raw, jnp.pad(b_f, (0, LANES - N_HEADS)).reshape(1, LANES))
    cumt = cum[:, :N_HEADS].T.reshape(N_HEADS, 1, s)
    return _dense_attention(q, k, v, mode="fox", dq=HEAD_DIM, extra=(cum, cumt, gate))


def _swiglu_ffn(h, x, gate, w_gate_up, w_down, layer):
    s = h.shape[0]
    tm, tn = min(TM, s), TN
    outs = [((s, FFN_HIDDEN), BF16, (tm, tn), lambda n, m: (m, n))]
    act = _mm(h, [(w_gate_up, layer, 0), (w_gate_up, layer, FFN_HIDDEN // tn)], [], outs, _epi_swiglu,
              tm=tm, tn=tn, n_tiles=FFN_HIDDEN // tn, name="ffn_gate_up")[0]
    return _residual_mm(act, w_down, layer, x, gate, tm=TM_DEEP)


def kernel(x, c, positions, ln_mix_g, ln_ffn_g, ada_w, ada_b, ffn_w_gate_up, ffn_w_down, mla_w_in, mla_q_a_g, mla_kv_a_g, mla_w_q_b, mla_w_kv_b, mla_q_g, mla_k_g, mla_w_out, dsa_w_in, dsa_q_g, dsa_k_g, dsa_idx_k_g, dsa_w_out, diff_w_in, diff_q_g, diff_k_g, diff_lambda_q1, diff_lambda_k1, diff_lambda_q2, diff_lambda_k2, diff_subln_g, diff_w_out, fox_w_in, fox_b_f, fox_q_g, fox_k_g, fox_w_out):
    batch, s, d = x.shape
    assert batch == 1 and d == D_MODEL
    depth = ada_w.shape[0]
    pos = positions[0]
    tabs_head = _rope_tables(pos, HEAD_DIM // 4, LANES)
    tabs_small = _rope_tables(pos, IDX_DIM // 4, IDX_DIM)
    tabs_mla = _rope_tables(pos, MLA_ROPE, LANES)
    mod = _ada_mod(c, ada_w, ada_b)
    xs = x[0]
    for i in range(depth):
        sh1, sc1, g1, sh2, sc2, g2 = [mod[i, t * d:(t + 1) * d] for t in range(6)]
        h = _normmod(xs, ln_mix_g[i], sc1, sh1)
        kind, j = i % 4, i // 4
        if kind == 0:
            o = _mla_mixer(h, tabs_mla, mla_w_in[j:j + 1], mla_q_a_g[j], mla_kv_a_g[j], mla_w_q_b[j:j + 1],
                           mla_w_kv_b[j:j + 1], mla_q_g[j], mla_k_g[j])
            w_out = mla_w_out
        elif kind == 1:
            o = _dsa_mixer(h, tabs_head, tabs_small, dsa_w_in[j:j + 1], dsa_q_g[j], dsa_k_g[j], dsa_idx_k_g[j])
            w_out = dsa_w_out
        elif kind == 2:
            lam_params = jnp.stack([diff_lambda_q1[j], diff_lambda_k1[j], diff_lambda_q2[j], diff_lambda_k2[j]])
            o = _diff_mixer(h, tabs_small, i, diff_w_in[j:j + 1], diff_q_g[j], diff_k_g[j], lam_params,
                            diff_subln_g[j])
            w_out = diff_w_out
        else:
            o = _fox_mixer(h, fox_w_in[j:j + 1], fox_b_f[j], fox_q_g[j], fox_k_g[j])
            w_out = fox_w_out
        xs = _residual_mm(o, w_out, j, xs, g1, tn=TN_WIDE)
        h = _normmod(xs, ln_ffn_g[i], sc2, sh2)
        xs = _swiglu_ffn(h, xs, g2, ffn_w_gate_up, ffn_w_down, i)
    return xs[None]
```

```python
import functools
import math

import jax
import jax.numpy as jnp
from jax import lax
from jax.experimental import pallas as pl
from jax.experimental.pallas import tpu as pltpu

F32 = jnp.float32
BF16 = jnp.bfloat16

D_MODEL = 2048
N_HEADS = 16
HEAD_DIM = 128
ROPE_THETA = 500000.0
RMS_EPS = 1e-6
FFN_HIDDEN = 5632
MLA_LORA = 512
MLA_ROPE = 64
MLA_HEAD = 192
IDX_HEADS = 16
IDX_DIM = 64
IDX_TOPK = 256
DIFF_DIM = 64

LANES = 128
LOG2E = 1.4426950408889634
NEG = -1e30
VMEM_LIMIT_BYTES = 56 * 1024 * 1024
TM = 1024
TM_DEEP = 512
SUB_ROWS = 256
TN = 512
TN_WIDE = 1024
HEADS_PER_STEP = 2
BISECT_ROWS = 128
BISECT_UNCHECKED = 8
INT_MIN = -2 ** 31
KEY_NEG_INF = (0xFF800000 ^ 0x7FFFFFFF) - 2 ** 32


def _cparams(*sem):
    return pltpu.CompilerParams(dimension_semantics=sem, vmem_limit_bytes=VMEM_LIMIT_BYTES)


def _dot_nt(a, b):
    return lax.dot_general(a, b, (((1,), (1,)), ((), ())), preferred_element_type=F32)


def _sigmoid(z):
    return 1.0 / (1.0 + jnp.exp(-z))


def _mm_body(*refs, n_w, n_aux, n_out, epi, sub):
    a_ref = refs[0]
    w_refs = refs[1:1 + n_w]
    aux_refs = refs[1 + n_w:1 + n_w + n_aux]
    out_refs = refs[1 + n_w + n_aux:1 + n_w + n_aux + n_out]
    wb_refs = refs[1 + n_w + n_aux + n_out:]
    tm = a_ref.shape[0]

    @pl.when(pl.program_id(1) == 0)
    def _cast_weights():
        for w_ref, wb_ref in zip(w_refs, wb_refs):
            wb_ref[...] = w_ref[...].astype(BF16)

    for r in range(tm // sub):
        rows = pl.ds(r * sub, sub)
        accs = [jnp.dot(a_ref[rows, :], wb_ref[...], preferred_element_type=F32) for wb_ref in wb_refs]
        epi(accs, [ref.at[rows] if ref.shape[0] == tm else ref for ref in aux_refs],
            [ref.at[rows] for ref in out_refs])


def _mm(a, ws, aux, outs, epi, *, tm, tn, n_tiles, a_blk=0, name="mm", sub=SUB_ROWS):
    s = a.shape[0]
    k = ws[0][0].shape[1]
    tm = min(tm, s)
    in_specs = [pl.BlockSpec((tm, k), lambda n, m: (m, a_blk))]
    operands = [a]
    for w, layer, off in ws:
        assert w.shape[1] == k
        in_specs.append(pl.BlockSpec((None, k, tn), lambda n, m, layer=layer, off=off: (layer, 0, n + off)))
        operands.append(w)
    for arr, block, imap in aux:
        in_specs.append(pl.BlockSpec(block, imap))
        operands.append(arr)
    out_shape = [jax.ShapeDtypeStruct(shape, dtype) for shape, dtype, _, _ in outs]
    out_specs = [pl.BlockSpec(block, imap) for _, _, block, imap in outs]
    body = functools.partial(_mm_body, n_w=len(ws), n_aux=len(aux), n_out=len(outs), epi=epi, sub=min(sub, tm))
    res = pl.pallas_call(
        body,
        grid=(n_tiles, s // tm),
        in_specs=in_specs,
        out_specs=out_specs,
        out_shape=out_shape,
        scratch_shapes=[pltpu.VMEM((k, tn), BF16) for _ in ws],
        compiler_params=_cparams("arbitrary", "arbitrary"),
        name=name,
    )(*operands)
    return res


def _rope_group(y, tabs, half):
    c, s1, s2 = tabs
    return y * c + pltpu.roll(y, half, 1) * s1 + pltpu.roll(y, LANES - half, 1) * s2


def _group_inv_rms(y, gw, real):
    tn = y.shape[1]
    sq = y * y
    n_groups = tn // LANES
    if gw == DIFF_DIM:
        lane = lax.broadcasted_iota(jnp.int32, (1, LANES), 1)
        lo = lane < DIFF_DIM
        out = []
        for g in range(n_groups):
            sg = sq[:, g * LANES:(g + 1) * LANES]
            s_lo = jnp.sum(jnp.where(lo, sg, 0.0), axis=1, keepdims=True)
            s_hi = jnp.sum(jnp.where(lo, 0.0, sg), axis=1, keepdims=True)
            out.append(lax.rsqrt(jnp.where(lo, s_lo, s_hi) * (1.0 / real) + RMS_EPS))
        return out
    sums = [jnp.sum(sq[:, g * LANES:(g + 1) * LANES], axis=1, keepdims=True) for g in range(n_groups)]
    per = gw // LANES
    out = []
    for h in range(n_groups // per):
        tot = sums[h * per]
        for t in range(1, per):
            tot = tot + sums[h * per + t]
        r = lax.rsqrt(tot * (1.0 / real) + RMS_EPS)
        out.extend([r] * per)
    return out


def _epi_heads(accs, aux_refs, out_refs, *, gw, real, half, rope_groups, scale, use_gain, use_rope):
    y = accs[0]
    tn = y.shape[1]
    idx = 0
    gain = None
    if use_gain:
        gain = aux_refs[idx][...]
        idx += 1
        inv = _group_inv_rms(y, gw, real)
    if use_rope:
        tabs = tuple(aux_refs[idx + t][...] for t in range(3))
    for g in range(tn // LANES):
        yg = y[:, g * LANES:(g + 1) * LANES]
        if use_gain:
            yg = yg * inv[g] * gain[:, g * LANES:(g + 1) * LANES]
        if use_rope and rope_groups(g):
            yg = _rope_group(yg, tabs, half)
        if scale != 1.0:
            yg = yg * scale
        out_refs[0][:, g * LANES:(g + 1) * LANES] = yg.astype(out_refs[0].dtype)


def _epi_plain(accs, aux_refs, out_refs):
    out_refs[0][...] = accs[0].astype(out_refs[0].dtype)


def _epi_sigmoid(accs, aux_refs, out_refs):
    out_refs[0][...] = _sigmoid(accs[0]).astype(out_refs[0].dtype)


def _epi_residual(accs, aux_refs, out_refs):
    x_ref, g_ref = aux_refs
    out_refs[0][...] = x_ref[...] + g_ref[...] * accs[0]


def _epi_swiglu(accs, aux_refs, out_refs):
    gate, up = accs
    out_refs[0][...] = (gate * _sigmoid(gate) * up).astype(out_refs[0].dtype)


def _epi_mla_kv(accs, aux_refs, out_refs):
    kpe_ref, g_nope_ref, g_pe_ref, c_ref, s1_ref, s2_ref = aux_refs
    k_ref, v_ref = out_refs
    y = accs[0]
    kpe = kpe_ref[...]
    ss_pe = jnp.sum(kpe * kpe, axis=1, keepdims=True)
    tabs = (c_ref[...], s1_ref[...], s2_ref[...])
    g_nope = g_nope_ref[...]
    g_pe = g_pe_ref[...]
    for h in range(y.shape[1] // (2 * LANES)):
        kn = y[:, 2 * h * LANES:(2 * h + 1) * LANES]
        v = y[:, (2 * h + 1) * LANES:(2 * h + 2) * LANES]
        r = lax.rsqrt((jnp.sum(kn * kn, axis=1, keepdims=True) + ss_pe) * (1.0 / MLA_HEAD) + RMS_EPS)
        k_ref[:, 2 * h * LANES:(2 * h + 1) * LANES] = (kn * r * g_nope).astype(k_ref.dtype)
        pe = _rope_group(kpe * r * g_pe, tabs, MLA_ROPE // 2)
        k_ref[:, (2 * h + 1) * LANES:(2 * h + 2) * LANES] = pe.astype(k_ref.dtype)
        v_ref[:, h * LANES:(h + 1) * LANES] = v.astype(v_ref.dtype)


def _row_tab_aux(tabs, tm):
    return [(t, (tm, LANES), lambda n, m: (m, 0)) for t in tabs]


def _proj(a, w, layer, col_off, width, *, tn, out_dtype=BF16, gain=None, tabs=None, gw=LANES, real=LANES,
          half=0, rope_groups=lambda g: True, scale=1.0, epi=None, tm=TM, a_blk=0):
    s = a.shape[0]
    tm = min(tm, s)
    assert col_off % tn == 0 and width % tn == 0
    aux = []
    if gain is not None:
        aux.append((gain.reshape(1, width).astype(F32), (1, tn), lambda n, m: (0, n)))
    if tabs is not None:
        aux.extend(_row_tab_aux(tabs, tm))
    if epi is None:
        epi = functools.partial(_epi_heads, gw=gw, real=real, half=half, rope_groups=rope_groups, scale=scale,
                                use_gain=gain is not None, use_rope=tabs is not None)
    outs = [((s, width), out_dtype, (tm, tn), lambda n, m: (m, n))]
    return _mm(a, [(w, layer, col_off // tn)], aux, outs, epi, tm=tm, tn=tn, n_tiles=width // tn, a_blk=a_blk,
               name="proj")[0]


def _residual_mm(a, w, layer, x, gate, *, tn=TN, tm=TM):
    s, d = x.shape
    tm = min(tm, s)
    aux = [(x, (tm, tn), lambda n, m: (m, n)), (gate.reshape(1, d), (1, tn), lambda n, m: (0, n))]
    outs = [((s, d), F32, (tm, tn), lambda n, m: (m, n))]
    return _mm(a, [(w, layer, 0)], aux, outs, _epi_residual, tm=tm, tn=tn, n_tiles=d // tn, name="residual_mm")[0]


def _ada_body(c_ref, w_ref, b_ref, o_ref):
    c = c_ref[...]
    cond = c * _sigmoid(c)
    o_ref[...] = jnp.dot(cond.astype(BF16), w_ref[...].astype(BF16), preferred_element_type=F32) + b_ref[...]


def _ada_mod(c, ada_w, ada_b):
    depth, d, n = ada_w.shape
    tn = 1536
    c8 = jnp.broadcast_to(c, (8, d))
    out = pl.pallas_call(
        _ada_body,
        grid=(depth, n // tn),
        in_specs=[pl.BlockSpec((8, d), lambda l, j: (0, 0)),
                  pl.BlockSpec((None, d, tn), lambda l, j: (l, 0, j)),
                  pl.BlockSpec((None, 1, tn), lambda l, j: (l, 0, j))],
        out_specs=pl.BlockSpec((None, 8, tn), lambda l, j: (l, 0, j)),
        out_shape=jax.ShapeDtypeStruct((depth, 8, n), F32),
        compiler_params=_cparams("arbitrary", "arbitrary"),
        name="ada_mod",
    )(c8, ada_w, ada_b.reshape(depth, 1, n))
    return out[:, 0, :]


def _normmod_body(x_ref, g_ref, sc_ref, sh_ref, o_ref):
    x = x_ref[...]
    y = x * lax.rsqrt(jnp.mean(x * x, axis=1, keepdims=True) + RMS_EPS) * g_ref[...]
    o_ref[...] = (y * (1.0 + sc_ref[...]) + sh_ref[...]).astype(o_ref.dtype)


def _normmod(x, g, sc, sh, tm=TM):
    s, d = x.shape
    tm = min(tm, s)
    vec = pl.BlockSpec((1, d), lambda m: (0, 0))
    return pl.pallas_call(
        _normmod_body,
        grid=(s // tm,),
        in_specs=[pl.BlockSpec((tm, d), lambda m: (m, 0)), vec, vec, vec],
        out_specs=pl.BlockSpec((tm, d), lambda m: (m, 0)),
        out_shape=jax.ShapeDtypeStruct((s, d), BF16),
        compiler_params=_cparams("arbitrary"),
        name="normmod",
    )(x, g.reshape(1, d), sc.reshape(1, d), sh.reshape(1, d))


def _softmax_scratch(n_maps, tq, tk):
    return [pltpu.VMEM((n_maps, tq, 2 * HEAD_DIM), F32), pltpu.VMEM((n_maps, tq, LANES), F32),
            pltpu.VMEM((n_maps, 2, tq, tk), F32), pltpu.VMEM((n_maps, 2, 2, tq, LANES), F32)]


def _softmax_pipeline(qs, kv_cols, k_ref, v_ref, scratch, *, tq, tk, first_chunk, first_keep, n_rest, bias_fn):
    acc_ref, m_ref, s_ref, st_ref = scratch
    ones = jnp.ones((tk, LANES), BF16)
    n_grp = tk // LANES
    maps = range(len(qs))

    def chunk_start(c):
        return pl.multiple_of(c * tk, tk)

    def stage1(c, slot, keep):
        start = chunk_start(c)
        for t in maps:
            s = _dot_nt(qs[t], k_ref[pl.ds(start, tk), kv_cols[t][0]])
            if bias_fn is not None:
                s = s + bias_fn(t, start)
            if keep is not None:
                s = jnp.where(keep, s, NEG)
            m_run = m_ref[t]
            m_new = jnp.maximum(m_run, jnp.max(s, axis=1, keepdims=True))
            s_ref[t, slot] = s
            st_ref[t, slot, 0] = m_new
            st_ref[t, slot, 1] = jnp.exp2(m_run - m_new)
            m_ref[t] = m_new

    def stage2(c, slot):
        start = chunk_start(c)
        for t in maps:
            v1 = jnp.concatenate([v_ref[pl.ds(start, tk), kv_cols[t][1]], ones], axis=1)
            m_new = st_ref[t, slot, 0]
            alpha = st_ref[t, slot, 1]
            p = jnp.concatenate([jnp.exp2(s_ref[t, slot, :, g * LANES:(g + 1) * LANES] - m_new)
                                 for g in range(n_grp)], axis=1)
            pv = jnp.dot(p.astype(BF16), v1, preferred_element_type=F32)
            acc_ref[t] = jnp.concatenate([alpha, alpha], axis=1) * acc_ref[t] + pv

    acc_ref[...] = jnp.zeros(acc_ref.shape, F32)
    m_ref[...] = jnp.full(m_ref.shape, NEG, F32)
    stage1(first_chunk, 0, first_keep)

    def two_chunks(c0):
        stage1(c0, 1, None)
        stage2(jnp.where(c0 == 0, first_chunk, c0 - 1), 0)
        stage1(c0 + 1, 0, None)
        stage2(c0, 1)

    def body(t4, carry):
        two_chunks(4 * t4)
        two_chunks(4 * t4 + 2)
        return carry

    n_quads = n_rest // 4
    lax.fori_loop(0, n_quads, body, 0)
    n_pairs = n_rest // 2

    @pl.when(n_pairs > 2 * n_quads)
    def _pair_tail():
        two_chunks(4 * n_quads)

    pending = jnp.where(n_pairs == 0, first_chunk, 2 * n_pairs - 1)

    @pl.when(n_rest % 2 == 1)
    def _odd_tail():
        stage1(n_rest - 1, 1, None)
        stage2(pending, 0)
        stage2(n_rest - 1, 1)

    @pl.when(n_rest % 2 == 0)
    def _even_tail():
        stage2(pending, 0)


def _causal_keep(tq, tk, col_shift):
    row = lax.broadcasted_iota(jnp.int32, (tq, 1), 0)
    col = lax.broadcasted_iota(jnp.int32, (1, tk), 1)
    return row >= col + col_shift


def _dense_attn_body(*refs, mode, tq, tk, lam_init):
    if mode == "mla":
        q_ref, k_ref, v_ref, o_ref = refs[:4]
        scratch = refs[4:]
    elif mode == "fox":
        q_ref, k_ref, v_ref, cum_ref, cumt_ref, gate_ref, o_ref = refs[:7]
        scratch = refs[7:]
    else:
        q_ref, k_ref, v_ref, lam_ref, subg_ref, o_ref = refs[:6]
        scratch = refs[6:]
    hg = pl.program_id(0)
    i = pl.program_id(1)
    acc_ref = scratch[0]
    dq = q_ref.shape[1] // HEADS_PER_STEP
    heads = range(HEADS_PER_STEP)
    lane = lax.broadcasted_iota(jnp.int32, (1, LANES), 1)

    def head_cols(hh, width=HEAD_DIM):
        return slice(hh * width, (hh + 1) * width)

    qs, kv_cols = [], []
    for hh in heads:
        q = q_ref[:, head_cols(hh, dq)]
        if mode == "diff":
            zero = jnp.zeros_like(q)
            qs += [jnp.where(lane < DIFF_DIM, q, zero), jnp.where(lane >= DIFF_DIM, q, zero)]
            kv_cols += [(head_cols(hh, dq), head_cols(hh))] * 2
        else:
            qs.append(q)
            kv_cols.append((head_cols(hh, dq), head_cols(hh)))
    if mode == "fox":
        cum = cum_ref[...]
        cqs = [jnp.sum(jnp.where(lane == hg * HEADS_PER_STEP + hh, cum, 0.0), axis=1, keepdims=True) * LOG2E
               for hh in heads]

        def bias_fn(t, start):
            return cqs[t] - cumt_ref[t, :, pl.ds(start, tk)] * LOG2E
    else:
        bias_fn = None

    _softmax_pipeline(qs, kv_cols, k_ref, v_ref, scratch, tq=tq, tk=tk, first_chunk=i,
                      first_keep=_causal_keep(tq, tk, 0), n_rest=i, bias_fn=bias_fn)

    def normalised(t):
        acc = acc_ref[t]
        return acc[:, :HEAD_DIM] / acc[:, HEAD_DIM:]

    for hh in heads:
        if mode == "diff":
            lp = lam_ref[...]
            lam = (jnp.exp(jnp.sum(lp[0:1] * lp[1:2], axis=1, keepdims=True))
                   - jnp.exp(jnp.sum(lp[2:3] * lp[3:4], axis=1, keepdims=True)) + lam_init)
            o = normalised(2 * hh) - lam * normalised(2 * hh + 1)
            o = o * lax.rsqrt(jnp.mean(o * o, axis=1, keepdims=True) + RMS_EPS) * subg_ref[...]
            o = o * (1.0 - lam_init)
        else:
            o = normalised(hh)
            if mode == "fox":
                o = o * gate_ref[:, head_cols(hh)]
        o_ref[:, head_cols(hh)] = o.astype(o_ref.dtype)


def _dense_attention(q, k, v, *, mode, dq, extra=(), lam_init=0.0, tq=512):
    s = q.shape[0]
    tq = min(tq, s)
    tk = tq
    g = HEADS_PER_STEP
    in_specs = [pl.BlockSpec((tq, g * dq), lambda h, i: (i, h)),
                pl.BlockSpec((s, g * dq), lambda h, i: (0, h)),
                pl.BlockSpec((s, g * HEAD_DIM), lambda h, i: (0, h))]
    operands = [q, k, v]
    if mode == "fox":
        cum, cumt, gate = extra
        in_specs += [pl.BlockSpec((tq, LANES), lambda h, i: (i, 0)),
                     pl.BlockSpec((g, 1, s), lambda h, i: (h, 0, 0)),
                     pl.BlockSpec((tq, g * HEAD_DIM), lambda h, i: (i, h))]
        operands += [cum, cumt, gate]
    elif mode == "diff":
        lam_params, subln_g = extra
        in_specs += [pl.BlockSpec((4, DIFF_DIM), lambda h, i: (0, 0)),
                     pl.BlockSpec((1, HEAD_DIM), lambda h, i: (0, 0))]
        operands += [lam_params, subln_g]
    n_maps = g * (2 if mode == "diff" else 1)
    body = functools.partial(_dense_attn_body, mode=mode, tq=tq, tk=tk, lam_init=lam_init)
    return pl.pallas_call(
        body,
        grid=(N_HEADS // g, s // tq),
        in_specs=in_specs,
        out_specs=pl.BlockSpec((tq, g * HEAD_DIM), lambda h, i: (i, h)),
        out_shape=jax.ShapeDtypeStruct((s, N_HEADS * HEAD_DIM), BF16),
        scratch_shapes=_softmax_scratch(n_maps, tq, tk),
        compiler_params=_cparams("arbitrary", "arbitrary"),
        name="attn_" + mode,
    )(*operands)


def _dsa_attn_body(q_ref, k_ref, v_ref, bias_ref, o_ref, *scratch, tq, tk):
    i = pl.program_id(0)

    def bias_fn(t, start):
        return bias_ref[:, pl.ds(start, tk)].astype(F32)

    cols = [slice(hh * HEAD_DIM, (hh + 1) * HEAD_DIM) for hh in range(HEADS_PER_STEP)]
    _softmax_pipeline([q_ref[:, c] for c in cols], [(c, c) for c in cols], k_ref, v_ref, scratch, tq=tq, tk=tk,
                      first_chunk=i, first_keep=None, n_rest=i, bias_fn=bias_fn)
    for t, c in enumerate(cols):
        acc = scratch[0][t]
        o_ref[:, c] = (acc[:, :HEAD_DIM] / acc[:, HEAD_DIM:]).astype(o_ref.dtype)


def _dsa_attention(q, k, v, bias, *, tq=512):
    s = q.shape[0]
    tq = min(tq, s)
    tk = tq
    body = functools.partial(_dsa_attn_body, tq=tq, tk=tk)
    return pl.pallas_call(
        body,
        grid=(s // tq, N_HEADS // HEADS_PER_STEP),
        in_specs=[pl.BlockSpec((tq, HEADS_PER_STEP * HEAD_DIM), lambda i, h: (i, h)),
                  pl.BlockSpec((s, HEADS_PER_STEP * HEAD_DIM), lambda i, h: (0, h)),
                  pl.BlockSpec((s, HEADS_PER_STEP * HEAD_DIM), lambda i, h: (0, h)),
                  pl.BlockSpec((tq, s), lambda i, h: (i, 0))],
        out_specs=pl.BlockSpec((tq, HEADS_PER_STEP * HEAD_DIM), lambda i, h: (i, h)),
        out_shape=jax.ShapeDtypeStruct((s, N_HEADS * HEAD_DIM), BF16),
        scratch_shapes=_softmax_scratch(HEADS_PER_STEP, tq, tk),
        compiler_params=_cparams("arbitrary", "arbitrary"),
        name="attn_dsa",
    )(q, k, v, bias)


def _sortable_key(score):
    bits = lax.bitcast_convert_type(score, jnp.int32)
    return bits ^ ((bits >> 31) & 0x7FFFFFFF)


def _indexer_body(iq_ref, ik_ref, iw_ref, bias_ref, keys_ref, qz_ref, wb_ref, thr_ref, need_ref, *, tq, n_sel):
    i = pl.program_id(0)
    s_len = ik_ref.shape[0]
    n_chunks = s_len // tq
    lane = lax.broadcasted_iota(jnp.int32, (1, LANES), 1)
    iw = iw_ref[...]
    for p in range(IDX_HEADS // 2):
        pair = iq_ref[:, p * LANES:(p + 1) * LANES]
        zero = jnp.zeros_like(pair)
        qz_ref[2 * p] = jnp.where(lane < IDX_DIM, pair, zero)
        qz_ref[2 * p + 1] = jnp.where(lane >= IDX_DIM, pair, zero)
    for hh in range(IDX_HEADS):
        wb_ref[hh] = jnp.broadcast_to(iw[:, hh:hh + 1], (tq, LANES))

    def score_chunk(start):
        ik = ik_ref[pl.ds(start, tq), :]
        sc = jnp.zeros((tq, tq), F32)
        for hh in range(IDX_HEADS):
            rel = jnp.maximum(_dot_nt(qz_ref[hh], ik), 0.0)
            sc = sc + jnp.tile(wb_ref[hh], (1, tq // LANES)) * rel
        return sc

    def full_body(j, carry):
        start = pl.multiple_of(j * tq, tq)
        keys_ref[:, pl.ds(start, tq)] = _sortable_key(score_chunk(start))
        return carry

    lax.fori_loop(0, i, full_body, 0)
    diag = pl.multiple_of(i * tq, tq)
    sc = jnp.where(_causal_keep(tq, tq, 0), score_chunk(diag), -jnp.inf)
    keys_ref[:, pl.ds(diag, tq)] = _sortable_key(sc)

    keys_ref[:, pl.ds(pl.multiple_of((i + 1) * tq, tq), tq)] = jnp.full((tq, tq), KEY_NEG_INF, jnp.int32)
    strips = [pl.ds(r0, BISECT_ROWS) for r0 in range(0, tq, BISECT_ROWS)]
    n_trips = (i + 2) // 2
    lanes_per_trip = 2 * tq // LANES

    def count_keys(rows, trial, strict):
        def count_body(j, cnt):
            start = pl.multiple_of(j * (2 * tq), 2 * tq)
            for g in range(lanes_per_trip):
                kk = keys_ref[rows, pl.ds(start + g * LANES, LANES)]
                cnt = cnt + ((kk > trial) if strict else (kk >= trial)).astype(jnp.int32)
            return cnt

        return lax.fori_loop(0, n_trips, count_body, jnp.zeros((BISECT_ROWS, LANES), jnp.int32))

    def key_bounds(rows):
        def max_body(j, gm):
            start = pl.multiple_of(j * (2 * tq), 2 * tq)
            gm = list(gm)
            for g in range(lanes_per_trip):
                gm[g % 2] = jnp.maximum(gm[g % 2], keys_ref[rows, pl.ds(start + g * LANES, LANES)])
            return tuple(gm)

        init = jnp.full((BISECT_ROWS, LANES), KEY_NEG_INF, jnp.int32)
        g0, g1 = lax.fori_loop(0, n_trips, max_body, (init, init))
        lo = jnp.min(jnp.minimum(g0, g1), axis=1, keepdims=True)
        hi = jnp.max(jnp.maximum(g0, g1), axis=1, keepdims=True)
        lo = jnp.maximum(lo, KEY_NEG_INF + 1) + jnp.zeros((BISECT_ROWS, LANES), jnp.int32)
        return lo, jnp.maximum(hi, lo) + 1

    def open_rows(lo, hi):
        width = hi - lo
        return ((width != 0) & (width != 1)).astype(jnp.int32)

    def halve(bounds):
        mids = [lo + lax.shift_right_logical(hi - lo, 1) for lo, hi in bounds]
        counts = [count_keys(rows, mid, False) for rows, mid in zip(strips, mids)]
        out = []
        for cnt, mid, (lo, hi) in zip(counts, mids, bounds):
            total = jnp.sum(cnt, axis=1, keepdims=True)
            enough = total >= n_sel
            out.append((jnp.where(enough, mid, lo), jnp.where(total == n_sel, mid + 1, jnp.where(enough, hi, mid))))
        return out

    def flat(bounds):
        return tuple(b for pair in bounds for b in pair)

    def pairs(flat_bounds):
        return [(flat_bounds[2 * t], flat_bounds[2 * t + 1]) for t in range(len(strips))]

    bounds = lax.fori_loop(0, BISECT_UNCHECKED, lambda _, fb: flat(halve(pairs(fb))),
                           flat([key_bounds(rows) for rows in strips]))

    def still_open(state):
        it, fb = state[0], state[1:]
        n_open = functools.reduce(jnp.maximum, [open_rows(lo, hi) for lo, hi in pairs(fb)])
        return (it < 32) & (jnp.max(n_open) > 0)

    bounds = lax.while_loop(still_open, lambda state: (state[0] + 1,) + flat(halve(pairs(state[1:]))),
                            (jnp.int32(0),) + bounds)[1:]

    surplus = jnp.zeros((BISECT_ROWS, LANES), jnp.int32)
    for rows, (thr, _) in zip(strips, pairs(bounds)):
        above = jnp.sum(count_keys(rows, thr, True), axis=1, keepdims=True)
        tied = jnp.sum(count_keys(rows, thr, False), axis=1, keepdims=True) - above
        need = n_sel - above
        thr_ref[rows, :] = thr
        need_ref[rows, :] = need.astype(F32) + jnp.zeros((BISECT_ROWS, LANES), F32)
        surplus = jnp.maximum(surplus, tied - need)
    any_surplus = jnp.max(surplus) > 0
    n_grp = tq // LANES

    @pl.when(jnp.logical_not(any_surplus))
    def _write_no_ties():
        def write_body(j, carry):
            start = pl.multiple_of(j * tq, tq)
            thr = thr_ref[...]
            for g in range(n_grp):
                cols = pl.ds(start + g * LANES, LANES)
                bias_ref[:, cols] = jnp.where(keys_ref[:, cols] >= thr, 0.0, NEG).astype(bias_ref.dtype)
            return carry

        lax.fori_loop(0, i + 1, write_body, 0)

    @pl.when(any_surplus)
    def _write_with_ties():
        ca = lax.broadcasted_iota(jnp.int32, (tq, tq), 0)
        cb = lax.broadcasted_iota(jnp.int32, (tq, tq), 1)
        prefix_ones = (ca <= cb).astype(BF16)
        all_ones = jnp.ones((tq, LANES), BF16)

        def write_body(j, seen):
            start = pl.multiple_of(j * tq, tq)
            kk = keys_ref[:, pl.ds(start, tq)]
            thr = jnp.concatenate([thr_ref[...]] * n_grp, axis=1)
            eq = kk == thr
            eq_b = jnp.where(eq, 1.0, 0.0).astype(BF16)
            rank = jnp.dot(eq_b, prefix_ones, preferred_element_type=F32) + jnp.concatenate([seen] * n_grp, axis=1)
            take = (kk > thr) | (eq & (rank <= jnp.concatenate([need_ref[...]] * n_grp, axis=1)))
            bias_ref[:, pl.ds(start, tq)] = jnp.where(take, 0.0, NEG).astype(bias_ref.dtype)
            return seen + jnp.dot(eq_b, all_ones, preferred_element_type=F32)

        lax.fori_loop(0, i + 1, write_body, jnp.zeros((tq, LANES), F32))

    def fill_body(j, carry):
        bias_ref[:, pl.ds(pl.multiple_of(j * tq, tq), tq)] = jnp.full((tq, tq), NEG, bias_ref.dtype)
        return carry

    lax.fori_loop(i + 1, n_chunks, fill_body, 0)


def _dsa_select_bias(iq, ik2, iw, n_sel, tq=256):
    s = iq.shape[0]
    tq = min(tq, s)
    body = functools.partial(_indexer_body, tq=tq, n_sel=n_sel)
    return pl.pallas_call(
        body,
        grid=(s // tq,),
        in_specs=[pl.BlockSpec((tq, IDX_HEADS * IDX_DIM), lambda i: (i, 0)),
                  pl.BlockSpec((s, LANES), lambda i: (0, 0)),
                  pl.BlockSpec((tq, LANES), lambda i: (i, 0))],
        out_specs=pl.BlockSpec((tq, s), lambda i: (i, 0)),
        out_shape=jax.ShapeDtypeStruct((s, s), BF16),
        scratch_shapes=[pltpu.VMEM((tq, s + tq), jnp.int32),
                        pltpu.VMEM((IDX_HEADS, tq, LANES), BF16),
                        pltpu.VMEM((IDX_HEADS, tq, LANES), F32),
                        pltpu.VMEM((tq, LANES), jnp.int32),
                        pltpu.VMEM((tq, LANES), F32)],
        compiler_params=_cparams("arbitrary"),
        name="dsa_select",
    )(iq, ik2, iw)


def _fox_cum_body(f_ref, b_ref, o_ref, carry_ref, *, tc):
    @pl.when(pl.program_id(0) == 0)
    def _():
        carry_ref[...] = jnp.zeros(carry_ref.shape, F32)

    z = f_ref[...] + b_ref[...]
    lf = jnp.minimum(z, 0.0) - jnp.log(1.0 + jnp.exp(-jnp.abs(z)))
    row = lax.broadcasted_iota(jnp.int32, (tc, tc), 0)
    col = lax.broadcasted_iota(jnp.int32, (tc, tc), 1)
    tri = (row >= col).astype(BF16)
    hi = lf.astype(BF16)
    r1 = lf - hi.astype(F32)
    mid = r1.astype(BF16)
    lo = (r1 - mid.astype(F32)).astype(BF16)
    cs = (jnp.dot(tri, hi, preferred_element_type=F32) + jnp.dot(tri, mid, preferred_element_type=F32)
          + jnp.dot(tri, lo, preferred_element_type=F32)) + carry_ref[...]
    o_ref[...] = cs
    carry_ref[...] = cs[tc - 1:tc, :]


def _fox_cumsum(f_raw, b_pad, tc=512):
    s = f_raw.shape[0]
    tc = min(tc, s)
    return pl.pallas_call(
        functools.partial(_fox_cum_body, tc=tc),
        grid=(s // tc,),
        in_specs=[pl.BlockSpec((tc, LANES), lambda m: (m, 0)), pl.BlockSpec((1, LANES), lambda m: (0, 0))],
        out_specs=pl.BlockSpec((tc, LANES), lambda m: (m, 0)),
        out_shape=jax.ShapeDtypeStruct((s, LANES), F32),
        scratch_shapes=[pltpu.VMEM((1, LANES), F32)],
        compiler_params=_cparams("arbitrary"),
        name="fox_cumsum",
    )(f_raw, b_pad)


def _rope_tables(positions, rot_dim, period):
    half = rot_dim // 2
    inv_freq = ROPE_THETA ** (-jnp.arange(0, rot_dim, 2, dtype=F32) / rot_dim)
    ang = positions.astype(F32)[:, None] * inv_freq
    cos, sin = jnp.cos(ang), jnp.sin(ang)
    lp = jnp.arange(LANES) % period
    in_x1 = lp < half
    in_x2 = (lp >= half) & (lp < 2 * half)
    idx = jnp.where(in_x1, lp, jnp.clip(lp - half, 0, half - 1))
    cos_l, sin_l = cos[:, idx], sin[:, idx]
    c = jnp.where(in_x1 | in_x2, cos_l, 1.0)
    s1 = jnp.where(in_x2, sin_l, 0.0)
    s2 = jnp.where(in_x1, -sin_l, 0.0)
    return c, s1, s2


def _pad_cols(w, width):
    return jnp.pad(w, ((0, 0), (0, width - w.shape[1])))


def _mla_mixer(h, tabs, w_in, q_a_g, kv_a_g, w_q_b, w_kv_b, q_g, k_g):
    s = h.shape[0]
    nope = HEAD_DIM
    scale = MLA_HEAD ** -0.5 * LOG2E
    gains = jnp.concatenate([q_a_g, kv_a_g])
    c_norm = _proj(h, w_in, 0, 0, 2 * MLA_LORA, tn=MLA_LORA, gain=gains, gw=MLA_LORA, real=MLA_LORA)
    kpe = _proj(h, _pad_cols(w_in[0][:, 2 * MLA_LORA:], LANES)[None], 0, 0, LANES, tn=LANES, out_dtype=F32,
                epi=_epi_plain)
    wq = w_q_b[0].reshape(MLA_LORA, N_HEADS, MLA_HEAD)
    wq = jnp.concatenate([wq[:, :, MLA_ROPE:], wq[:, :, :MLA_ROPE],
                          jnp.zeros((MLA_LORA, N_HEADS, 2 * LANES - MLA_HEAD), F32)], axis=2)
    wq = wq.reshape(1, MLA_LORA, N_HEADS * 2 * LANES)
    qg = jnp.concatenate([q_g[MLA_ROPE:], q_g[:MLA_ROPE], jnp.zeros((2 * LANES - MLA_HEAD,), F32)])
    q = _proj(c_norm, wq, 0, 0, N_HEADS * 2 * LANES, tn=TN_WIDE, gain=jnp.tile(qg, N_HEADS), tabs=tabs,
              gw=2 * LANES, real=MLA_HEAD, half=MLA_ROPE // 2, rope_groups=lambda g: g % 2 == 1, scale=scale)
    tm, tn = min(TM, s), TN
    aux = [(kpe, (tm, LANES), lambda n, m: (m, 0)),
           (k_g[MLA_ROPE:].reshape(1, nope), (1, nope), lambda n, m: (0, 0)),
           (jnp.pad(k_g[:MLA_ROPE], (0, LANES - MLA_ROPE)).reshape(1, LANES), (1, LANES), lambda n, m: (0, 0))]
    aux += _row_tab_aux(tabs, tm)
    outs = [((s, N_HEADS * 2 * LANES), BF16, (tm, tn), lambda n, m: (m, n)),
            ((s, N_HEADS * HEAD_DIM), BF16, (tm, tn // 2), lambda n, m: (m, n))]
    k, v = _mm(c_norm, [(w_kv_b, 0, 0)], aux, outs, _epi_mla_kv, tm=tm, tn=tn,
               n_tiles=N_HEADS * 2 * LANES // tn, a_blk=1, name="mla_kv")
    return _dense_attention(q, k, v, mode="mla", dq=2 * LANES)


def _dsa_mixer(h, tabs_head, tabs_idx, w_in, q_g, k_g, idx_k_g):
    s = h.shape[0]
    hd = N_HEADS * HEAD_DIM
    ih = IDX_HEADS * IDX_DIM
    half = HEAD_DIM // 8
    q = _proj(h, w_in, 0, 0, hd, tn=TN_WIDE, gain=jnp.tile(q_g, N_HEADS), tabs=tabs_head, half=half,
              scale=HEAD_DIM ** -0.5 * LOG2E)
    k = _proj(h, w_in, 0, hd, hd, tn=TN_WIDE, gain=jnp.tile(k_g, N_HEADS), tabs=tabs_head, half=half)
    v = _proj(h, w_in, 0, 2 * hd, hd, tn=TN_WIDE, epi=_epi_plain)
    iq = _proj(h, w_in, 0, 3 * hd, ih, tn=TN_WIDE, tabs=tabs_idx, half=IDX_DIM // 8)
    w_ik = w_in[0][:, 3 * hd + ih:3 * hd + ih + IDX_DIM]
    ik2 = _proj(h, jnp.concatenate([w_ik, w_ik], axis=1)[None], 0, 0, LANES, tn=LANES,
                gain=jnp.tile(idx_k_g, 2), tabs=tabs_idx, half=IDX_DIM // 8)
    w_iw = _pad_cols(w_in[0][:, 3 * hd + ih + IDX_DIM:], LANES)[None]
    iw = _proj(h, w_iw, 0, 0, LANES, tn=LANES, out_dtype=F32,
               epi=functools.partial(_epi_heads, gw=LANES, real=LANES, half=0, rope_groups=None,
                                     scale=IDX_HEADS ** -0.5 * IDX_DIM ** -0.5, use_gain=False, use_rope=False))
    bias = _dsa_select_bias(iq, ik2, iw, min(IDX_TOPK, s // 4))
    return _dsa_attention(q, k, v, bias)


def _diff_mixer(h, tabs, layer_idx, w_in, q_g, k_g, lam_params, subln_g):
    w = N_HEADS * 2 * DIFF_DIM
    half = DIFF_DIM // 8
    q = _proj(h, w_in, 0, 0, w, tn=TN_WIDE, gain=jnp.tile(q_g, 2 * N_HEADS), tabs=tabs, gw=DIFF_DIM, real=DIFF_DIM,
              half=half, scale=DIFF_DIM ** -0.5 * LOG2E)
    k = _proj(h, w_in, 0, w, w, tn=TN_WIDE, gain=jnp.tile(k_g, 2 * N_HEADS), tabs=tabs, gw=DIFF_DIM, real=DIFF_DIM,
              half=half)
    v = _proj(h, w_in, 0, 2 * w, w, tn=TN_WIDE, epi=_epi_plain)
    lam_init = 0.8 - 0.6 * math.exp(-0.3 * layer_idx)
    return _dense_attention(q, k, v, mode="diff", dq=HEAD_DIM, extra=(lam_params, subln_g.reshape(1, HEAD_DIM)),
                            lam_init=lam_init)


def _fox_mixer(h, w_in, b_f, q_g, k_g):
    s = h.shape[0]
    hd = N_HEADS * HEAD_DIM
    q = _proj(h, w_in, 0, 0, hd, tn=TN_WIDE, gain=jnp.tile(q_g, N_HEADS), scale=HEAD_DIM ** -0.5 * LOG2E)
    k = _proj(h, w_in, 0, hd, hd, tn=TN_WIDE, gain=jnp.tile(k_g, N_HEADS))
    v = _proj(h, w_in, 0, 2 * hd, hd, tn=TN_WIDE, epi=_epi_plain)
    w_f = _pad_cols(w_in[0][:, 3 * hd:3 * hd + N_HEADS], LANES)[None]
    f_raw = _proj(h, w_f, 0, 0, LANES, tn=LANES, out_dtype=F32, epi=_epi_plain)
    gate = _proj(h, w_in[:, :, 3 * hd + N_HEADS:], 0, 0, hd, tn=TN_WIDE, out_dtype=F32, epi=_epi_sigmoid)
    cum = _fox_cumsum(f_raw, jnp.pad(b_f, (0, LANES - N_HEADS)).reshape(1, LANES))
    cumt = cum[:, :N_HEADS].T.reshape(N_HEADS, 1, s)
    return _dense_attention(q, k, v, mode="fox", dq=HEAD_DIM, extra=(cum, cumt, gate))


def _swiglu_ffn(h, x, gate, w_gate_up, w_down, layer):
    s = h.shape[0]
    tm, tn = min(2 * TM, s), TN
    outs = [((s, FFN_HIDDEN), BF16, (tm, tn), lambda n, m: (m, n))]
    act = _mm(h, [(w_gate_up, layer, 0), (w_gate_up, layer, FFN_HIDDEN // tn)], [], outs, _epi_swiglu,
              tm=tm, tn=tn, n_tiles=FFN_HIDDEN // tn, name="ffn_gate_up")[0]
    return _residual_mm(act, w_down, layer, x, gate, tm=TM_DEEP)


def kernel(x, c, positions, ln_mix_g, ln_ffn_g, ada_w, ada_b, ffn_w_gate_up, ffn_w_down, mla_w_in, mla_q_a_g, mla_kv_a_g, mla_w_q_b, mla_w_kv_b, mla_q_g, mla_k_g, mla_w_out, dsa_w_in, dsa_q_g, dsa_k_g, dsa_idx_k_g, dsa_w_out, diff_w_in, diff_q_g, diff_k_g, diff_lambda_q1, diff_lambda_k1, diff_lambda_q2, diff_lambda_k2, diff_subln_g, diff_w_out, fox_w_in, fox_b_f, fox_q_g, fox_k_g, fox_w_out):
    batch, s, d = x.shape
    assert batch == 1 and d == D_MODEL
    depth = ada_w.shape[0]
    pos = positions[0]
    tabs_head = _rope_tables(pos, HEAD_DIM // 4, LANES)
    tabs_small = _rope_tables(pos, IDX_DIM // 4, IDX_DIM)
    tabs_mla = _rope_tables(pos, MLA_ROPE, LANES)
    mod = _ada_mod(c, ada_w, ada_b)
    xs = x[0]
    for i in range(depth):
        sh1, sc1, g1, sh2, sc2, g2 = [mod[i, t * d:(t + 1) * d] for t in range(6)]
        h = _normmod(xs, ln_mix_g[i], sc1, sh1)
        kind, j = i % 4, i // 4
        if kind == 0:
            o = _mla_mixer(h, tabs_mla, mla_w_in[j:j + 1], mla_q_a_g[j], mla_kv_a_g[j], mla_w_q_b[j:j + 1],
                           mla_w_kv_b[j:j + 1], mla_q_g[j], mla_k_g[j])
            w_out = mla_w_out
        elif kind == 1:
            o = _dsa_mixer(h, tabs_head, tabs_small, dsa_w_in[j:j + 1], dsa_q_g[j], dsa_k_g[j], dsa_idx_k_g[j])
            w_out = dsa_w_out
        elif kind == 2:
            lam_params = jnp.stack([diff_lambda_q1[j], diff_lambda_k1[j], diff_lambda_q2[j], diff_lambda_k2[j]])
            o = _diff_mixer(h, tabs_small, i, diff_w_in[j:j + 1], diff_q_g[j], diff_k_g[j], lam_params,
                            diff_subln_g[j])
            w_out = diff_w_out
        else:
            o = _fox_mixer(h, fox_w_in[j:j + 1], fox_b_f[j], fox_q_g[j], fox_k_g[j])
            w_out = fox_w_out
        xs = _residual_mm(o, w_out, j, xs, g1, tn=TN_WIDE)
        h = _normmod(xs, ln_ffn_g[i], sc2, sh2)
        xs = _swiglu_ffn(h, xs, g2, ffn_w_gate_up, ffn_w_down, i)
    return xs[None]
```

```python
import functools
import math

import jax
import jax.numpy as jnp
from jax import lax
from jax.experimental import pallas as pl
from jax.experimental.pallas import tpu as pltpu

F32 = jnp.float32
BF16 = jnp.bfloat16

D_MODEL = 2048
N_HEADS = 16
HEAD_DIM = 128
ROPE_THETA = 500000.0
RMS_EPS = 1e-6
FFN_HIDDEN = 5632
MLA_LORA = 512
MLA_ROPE = 64
MLA_HEAD = 192
IDX_HEADS = 16
IDX_DIM = 64
IDX_TOPK = 256
DIFF_DIM = 64

LANES = 128
LOG2E = 1.4426950408889634
NEG = -1e30
VMEM_LIMIT_BYTES = 56 * 1024 * 1024
TM = 1024
TM_LONG = 2048
TM_DEEP = 512
SUB_ROWS = 256
TN = 512
TN_WIDE = 1024
HEADS_PER_STEP = 2
BISECT_ROWS = 128
BISECT_UNCHECKED = 8
INT_MIN = -2 ** 31
KEY_NEG_INF = (0xFF800000 ^ 0x7FFFFFFF) - 2 ** 32


def _cparams(*sem):
    return pltpu.CompilerParams(dimension_semantics=sem, vmem_limit_bytes=VMEM_LIMIT_BYTES)


def _dot_nt(a, b):
    return lax.dot_general(a, b, (((1,), (1,)), ((), ())), preferred_element_type=F32)


def _sigmoid(z):
    return 1.0 / (1.0 + jnp.exp(-z))


def _mm_body(*refs, n_w, n_aux, n_out, epi, sub):
    a_ref = refs[0]
    w_refs = refs[1:1 + n_w]
    aux_refs = refs[1 + n_w:1 + n_w + n_aux]
    out_refs = refs[1 + n_w + n_aux:1 + n_w + n_aux + n_out]
    wb_refs = refs[1 + n_w + n_aux + n_out:]
    tm = a_ref.shape[0]

    @pl.when(pl.program_id(1) == 0)
    def _cast_weights():
        for w_ref, wb_ref in zip(w_refs, wb_refs):
            wb_ref[...] = w_ref[...].astype(BF16)

    for r in range(tm // sub):
        rows = pl.ds(r * sub, sub)
        accs = [jnp.dot(a_ref[rows, :], wb_ref[...], preferred_element_type=F32) for wb_ref in wb_refs]
        epi(accs, [ref.at[rows] if ref.shape[0] == tm else ref for ref in aux_refs],
            [ref.at[rows] for ref in out_refs])


def _mm(a, ws, aux, outs, epi, *, tm, tn, n_tiles, a_blk=0, name="mm", sub=SUB_ROWS):
    s = a.shape[0]
    k = ws[0][0].shape[1]
    tm = min(tm, s)
    in_specs = [pl.BlockSpec((tm, k), lambda n, m: (m, a_blk))]
    operands = [a]
    for w, layer, off in ws:
        assert w.shape[1] == k
        in_specs.append(pl.BlockSpec((None, k, tn), lambda n, m, layer=layer, off=off: (layer, 0, n + off)))
        operands.append(w)
    for arr, block, imap in aux:
        in_specs.append(pl.BlockSpec(block, imap))
        operands.append(arr)
    out_shape = [jax.ShapeDtypeStruct(shape, dtype) for shape, dtype, _, _ in outs]
    out_specs = [pl.BlockSpec(block, imap) for _, _, block, imap in outs]
    body = functools.partial(_mm_body, n_w=len(ws), n_aux=len(aux), n_out=len(outs), epi=epi, sub=min(sub, tm))
    res = pl.pallas_call(
        body,
        grid=(n_tiles, s // tm),
        in_specs=in_specs,
        out_specs=out_specs,
        out_shape=out_shape,
        scratch_shapes=[pltpu.VMEM((k, tn), BF16) for _ in ws],
        compiler_params=_cparams("arbitrary", "arbitrary"),
        name=name,
    )(*operands)
    return res


def _rope_group(y, tabs, half):
    c, s1, s2 = tabs
    return y * c + pltpu.roll(y, half, 1) * s1 + pltpu.roll(y, LANES - half, 1) * s2


def _group_inv_rms(y, gw, real):
    tn = y.shape[1]
    sq = y * y
    n_groups = tn // LANES
    if gw == DIFF_DIM:
        lane = lax.broadcasted_iota(jnp.int32, (1, LANES), 1)
        lo = lane < DIFF_DIM
        out = []
        for g in range(n_groups):
            sg = sq[:, g * LANES:(g + 1) * LANES]
            s_lo = jnp.sum(jnp.where(lo, sg, 0.0), axis=1, keepdims=True)
            s_hi = jnp.sum(jnp.where(lo, 0.0, sg), axis=1, keepdims=True)
            out.append(lax.rsqrt(jnp.where(lo, s_lo, s_hi) * (1.0 / real) + RMS_EPS))
        return out
    per = gw // LANES
    out = []
    for h in range(n_groups // per):
        folded = sq[:, h * gw:h * gw + LANES]
        for t in range(1, per):
            folded = folded + sq[:, h * gw + t * LANES:h * gw + (t + 1) * LANES]
        r = lax.rsqrt(jnp.sum(folded, axis=1, keepdims=True) * (1.0 / real) + RMS_EPS)
        out.extend([r] * per)
    return out


def _epi_heads(accs, aux_refs, out_refs, *, gw, real, half, rope_groups, scale, use_gain, use_rope):
    y = accs[0]
    tn = y.shape[1]
    idx = 0
    gain = None
    if use_gain:
        gain = aux_refs[idx][...]
        idx += 1
        inv = _group_inv_rms(y, gw, real)
    if use_rope:
        tabs = tuple(aux_refs[idx + t][...] for t in range(3))
    for g in range(tn // LANES):
        yg = y[:, g * LANES:(g + 1) * LANES]
        if use_gain:
            yg = yg * inv[g] * gain[:, g * LANES:(g + 1) * LANES]
        if use_rope and rope_groups(g):
            yg = _rope_group(yg, tabs, half)
        if scale != 1.0:
            yg = yg * scale
        out_refs[0][:, g * LANES:(g + 1) * LANES] = yg.astype(out_refs[0].dtype)


def _epi_plain(accs, aux_refs, out_refs):
    out_refs[0][...] = accs[0].astype(out_refs[0].dtype)


def _epi_sigmoid(accs, aux_refs, out_refs):
    out_refs[0][...] = _sigmoid(accs[0]).astype(out_refs[0].dtype)


def _epi_residual(accs, aux_refs, out_refs):
    x_ref, g_ref = aux_refs
    out_refs[0][...] = x_ref[...] + g_ref[...] * accs[0]


def _epi_swiglu(accs, aux_refs, out_refs):
    gate, up = accs
    out_refs[0][...] = (gate * _sigmoid(gate) * up).astype(out_refs[0].dtype)


def _epi_mla_kv(accs, aux_refs, out_refs):
    kpe_ref, g_nope_ref, g_pe_ref, c_ref, s1_ref, s2_ref = aux_refs
    k_ref, v_ref = out_refs
    y = accs[0]
    kpe = kpe_ref[...]
    ss_pe = jnp.sum(kpe * kpe, axis=1, keepdims=True)
    tabs = (c_ref[...], s1_ref[...], s2_ref[...])
    g_nope = g_nope_ref[...]
    pe_roped = _rope_group(kpe * g_pe_ref[...], tabs, MLA_ROPE // 2)
    for h in range(y.shape[1] // (2 * LANES)):
        kn = y[:, 2 * h * LANES:(2 * h + 1) * LANES]
        v = y[:, (2 * h + 1) * LANES:(2 * h + 2) * LANES]
        r = lax.rsqrt((jnp.sum(kn * kn, axis=1, keepdims=True) + ss_pe) * (1.0 / MLA_HEAD) + RMS_EPS)
        k_ref[:, 2 * h * LANES:(2 * h + 1) * LANES] = (kn * r * g_nope).astype(k_ref.dtype)
        k_ref[:, (2 * h + 1) * LANES:(2 * h + 2) * LANES] = (pe_roped * r).astype(k_ref.dtype)
        v_ref[:, h * LANES:(h + 1) * LANES] = v.astype(v_ref.dtype)


def _row_tab_aux(tabs, tm):
    return [(t, (tm, LANES), lambda n, m: (m, 0)) for t in tabs]


def _proj(a, w, layer, col_off, width, *, tn, out_dtype=BF16, gain=None, tabs=None, gw=LANES, real=LANES,
          half=0, rope_groups=lambda g: True, scale=1.0, epi=None, tm=TM_LONG, a_blk=0):
    s = a.shape[0]
    tm = min(tm, s)
    assert col_off % tn == 0 and width % tn == 0
    aux = []
    if gain is not None:
        aux.append((gain.reshape(1, width).astype(F32), (1, tn), lambda n, m: (0, n)))
    if tabs is not None:
        aux.extend(_row_tab_aux(tabs, tm))
    if epi is None:
        epi = functools.partial(_epi_heads, gw=gw, real=real, half=half, rope_groups=rope_groups, scale=scale,
                                use_gain=gain is not None, use_rope=tabs is not None)
    outs = [((s, width), out_dtype, (tm, tn), lambda n, m: (m, n))]
    return _mm(a, [(w, layer, col_off // tn)], aux, outs, epi, tm=tm, tn=tn, n_tiles=width // tn, a_blk=a_blk,
               name="proj")[0]


def _residual_mm(a, w, layer, x, gate, *, tn=TN, tm=TM):
    s, d = x.shape
    tm = min(tm, s)
    aux = [(x, (tm, tn), lambda n, m: (m, n)), (gate.reshape(1, d), (1, tn), lambda n, m: (0, n))]
    outs = [((s, d), F32, (tm, tn), lambda n, m: (m, n))]
    return _mm(a, [(w, layer, 0)], aux, outs, _epi_residual, tm=tm, tn=tn, n_tiles=d // tn, name="residual_mm")[0]


def _ada_body(c_ref, w_ref, b_ref, o_ref):
    c = c_ref[...]
    cond = c * _sigmoid(c)
    o_ref[...] = jnp.dot(cond.astype(BF16), w_ref[...].astype(BF16), preferred_element_type=F32) + b_ref[...]


def _ada_mod(c, ada_w, ada_b):
    depth, d, n = ada_w.shape
    tn = 1536
    c8 = jnp.broadcast_to(c, (8, d))
    out = pl.pallas_call(
        _ada_body,
        grid=(depth, n // tn),
        in_specs=[pl.BlockSpec((8, d), lambda l, j: (0, 0)),
                  pl.BlockSpec((None, d, tn), lambda l, j: (l, 0, j)),
                  pl.BlockSpec((None, 1, tn), lambda l, j: (l, 0, j))],
        out_specs=pl.BlockSpec((None, 8, tn), lambda l, j: (l, 0, j)),
        out_shape=jax.ShapeDtypeStruct((depth, 8, n), F32),
        compiler_params=_cparams("arbitrary", "arbitrary"),
        name="ada_mod",
    )(c8, ada_w, ada_b.reshape(depth, 1, n))
    return out[:, 0, :]


def _normmod_body(x_ref, g_ref, sc_ref, sh_ref, o_ref):
    x = x_ref[...]
    y = x * lax.rsqrt(jnp.mean(x * x, axis=1, keepdims=True) + RMS_EPS) * g_ref[...]
    o_ref[...] = (y * (1.0 + sc_ref[...]) + sh_ref[...]).astype(o_ref.dtype)


def _normmod(x, g, sc, sh, tm=TM):
    s, d = x.shape
    tm = min(tm, s)
    vec = pl.BlockSpec((1, d), lambda m: (0, 0))
    return pl.pallas_call(
        _normmod_body,
        grid=(s // tm,),
        in_specs=[pl.BlockSpec((tm, d), lambda m: (m, 0)), vec, vec, vec],
        out_specs=pl.BlockSpec((tm, d), lambda m: (m, 0)),
        out_shape=jax.ShapeDtypeStruct((s, d), BF16),
        compiler_params=_cparams("arbitrary"),
        name="normmod",
    )(x, g.reshape(1, d), sc.reshape(1, d), sh.reshape(1, d))


def _softmax_scratch(n_maps, tq, tk):
    return [pltpu.VMEM((n_maps, tq, 2 * HEAD_DIM), F32), pltpu.VMEM((n_maps, tq, LANES), F32),
            pltpu.VMEM((n_maps, 2, tq, tk), F32), pltpu.VMEM((n_maps, 2, 2, tq, LANES), F32)]


def _softmax_pipeline(qs, kv_cols, k_ref, v_ref, scratch, *, tq, tk, first_chunk, first_keep, n_rest, bias_fn):
    acc_ref, m_ref, s_ref, st_ref = scratch
    ones = jnp.ones((tk, LANES), BF16)
    n_grp = tk // LANES
    maps = range(len(qs))

    def chunk_start(c):
        return pl.multiple_of(c * tk, tk)

    def stage1(c, slot, keep):
        start = chunk_start(c)
        for t in maps:
            s = _dot_nt(qs[t], k_ref[pl.ds(start, tk), kv_cols[t][0]])
            if bias_fn is not None:
                s = s + bias_fn(t, start)
            if keep is not None:
                s = jnp.where(keep, s, NEG)
            m_run = m_ref[t]
            m_new = jnp.maximum(m_run, jnp.max(s, axis=1, keepdims=True))
            s_ref[t, slot] = s
            st_ref[t, slot, 0] = m_new
            st_ref[t, slot, 1] = jnp.exp2(m_run - m_new)
            m_ref[t] = m_new

    def stage2(c, slot):
        start = chunk_start(c)
        for t in maps:
            v1 = jnp.concatenate([v_ref[pl.ds(start, tk), kv_cols[t][1]], ones], axis=1)
            m_new = st_ref[t, slot, 0]
            alpha = st_ref[t, slot, 1]
            p = jnp.concatenate([jnp.exp2(s_ref[t, slot, :, g * LANES:(g + 1) * LANES] - m_new)
                                 for g in range(n_grp)], axis=1)
            pv = jnp.dot(p.astype(BF16), v1, preferred_element_type=F32)
            acc_ref[t] = jnp.concatenate([alpha, alpha], axis=1) * acc_ref[t] + pv

    acc_ref[...] = jnp.zeros(acc_ref.shape, F32)
    m_ref[...] = jnp.full(m_ref.shape, NEG, F32)
    stage1(first_chunk, 0, first_keep)

    def two_chunks(c0):
        stage1(c0, 1, None)
        stage2(jnp.where(c0 == 0, first_chunk, c0 - 1), 0)
        stage1(c0 + 1, 0, None)
        stage2(c0, 1)

    def body(t4, carry):
        two_chunks(4 * t4)
        two_chunks(4 * t4 + 2)
        return carry

    n_quads = n_rest // 4
    lax.fori_loop(0, n_quads, body, 0)
    n_pairs = n_rest // 2

    @pl.when(n_pairs > 2 * n_quads)
    def _pair_tail():
        two_chunks(4 * n_quads)

    pending = jnp.where(n_pairs == 0, first_chunk, 2 * n_pairs - 1)

    @pl.when(n_rest % 2 == 1)
    def _odd_tail():
        stage1(n_rest - 1, 1, None)
        stage2(pending, 0)
        stage2(n_rest - 1, 1)

    @pl.when(n_rest % 2 == 0)
    def _even_tail():
        stage2(pending, 0)


def _causal_keep(tq, tk, col_shift):
    row = lax.broadcasted_iota(jnp.int32, (tq, 1), 0)
    col = lax.broadcasted_iota(jnp.int32, (1, tk), 1)
    return row >= col + col_shift


def _dense_attn_body(*refs, mode, tq, tk, lam_init):
    if mode == "mla":
        q_ref, k_ref, v_ref, o_ref = refs[:4]
        scratch = refs[4:]
    elif mode == "fox":
        q_ref, k_ref, v_ref, cum_ref, cumt_ref, gate_ref, o_ref = refs[:7]
        scratch = refs[7:]
    else:
        q_ref, k_ref, v_ref, lam_ref, subg_ref, o_ref = refs[:6]
        scratch = refs[6:]
    hg = pl.program_id(0)
    i = pl.program_id(1)
    acc_ref = scratch[0]
    dq = q_ref.shape[1] // HEADS_PER_STEP
    heads = range(HEADS_PER_STEP)
    lane = lax.broadcasted_iota(jnp.int32, (1, LANES), 1)

    def head_cols(hh, width=HEAD_DIM):
        return slice(hh * width, (hh + 1) * width)

    qs, kv_cols = [], []
    for hh in heads:
        q = q_ref[:, head_cols(hh, dq)]
        if mode == "diff":
            zero = jnp.zeros_like(q)
            qs += [jnp.where(lane < DIFF_DIM, q, zero), jnp.where(lane >= DIFF_DIM, q, zero)]
            kv_cols += [(head_cols(hh, dq), head_cols(hh))] * 2
        else:
            qs.append(q)
            kv_cols.append((head_cols(hh, dq), head_cols(hh)))
    if mode == "fox":
        cum = cum_ref[...]
        cqs = [jnp.sum(jnp.where(lane == hg * HEADS_PER_STEP + hh, cum, 0.0), axis=1, keepdims=True) * LOG2E
               for hh in heads]

        def bias_fn(t, start):
            return cqs[t] - cumt_ref[t, :, pl.ds(start, tk)] * LOG2E
    else:
        bias_fn = None

    _softmax_pipeline(qs, kv_cols, k_ref, v_ref, scratch, tq=tq, tk=tk, first_chunk=i,
                      first_keep=_causal_keep(tq, tk, 0), n_rest=i, bias_fn=bias_fn)

    def normalised(t):
        acc = acc_ref[t]
        return acc[:, :HEAD_DIM] / acc[:, HEAD_DIM:]

    for hh in heads:
        if mode == "diff":
            lp = lam_ref[...]
            lam = (jnp.exp(jnp.sum(lp[0:1] * lp[1:2], axis=1, keepdims=True))
                   - jnp.exp(jnp.sum(lp[2:3] * lp[3:4], axis=1, keepdims=True)) + lam_init)
            o = normalised(2 * hh) - lam * normalised(2 * hh + 1)
            o = o * lax.rsqrt(jnp.mean(o * o, axis=1, keepdims=True) + RMS_EPS) * subg_ref[...]
            o = o * (1.0 - lam_init)
        else:
            o = normalised(hh)
            if mode == "fox":
                o = o * gate_ref[:, head_cols(hh)]
        o_ref[:, head_cols(hh)] = o.astype(o_ref.dtype)


def _dense_attention(q, k, v, *, mode, dq, extra=(), lam_init=0.0, tq=512):
    s = q.shape[0]
    tq = min(tq, s)
    tk = tq
    g = HEADS_PER_STEP
    in_specs = [pl.BlockSpec((tq, g * dq), lambda h, i: (i, h)),
                pl.BlockSpec((s, g * dq), lambda h, i: (0, h)),
                pl.BlockSpec((s, g * HEAD_DIM), lambda h, i: (0, h))]
    operands = [q, k, v]
    if mode == "fox":
        cum, cumt, gate = extra
        in_specs += [pl.BlockSpec((tq, LANES), lambda h, i: (i, 0)),
                     pl.BlockSpec((g, 1, s), lambda h, i: (h, 0, 0)),
                     pl.BlockSpec((tq, g * HEAD_DIM), lambda h, i: (i, h))]
        operands += [cum, cumt, gate]
    elif mode == "diff":
        lam_params, subln_g = extra
        in_specs += [pl.BlockSpec((4, DIFF_DIM), lambda h, i: (0, 0)),
                     pl.BlockSpec((1, HEAD_DIM), lambda h, i: (0, 0))]
        operands += [lam_params, subln_g]
    n_maps = g * (2 if mode == "diff" else 1)
    body = functools.partial(_dense_attn_body, mode=mode, tq=tq, tk=tk, lam_init=lam_init)
    return pl.pallas_call(
        body,
        grid=(N_HEADS // g, s // tq),
        in_specs=in_specs,
        out_specs=pl.BlockSpec((tq, g * HEAD_DIM), lambda h, i: (i, h)),
        out_shape=jax.ShapeDtypeStruct((s, N_HEADS * HEAD_DIM), BF16),
        scratch_shapes=_softmax_scratch(n_maps, tq, tk),
        compiler_params=_cparams("arbitrary", "arbitrary"),
        name="attn_" + mode,
    )(*operands)


def _dsa_attn_body(q_ref, k_ref, v_ref, bias_ref, o_ref, *scratch, tq, tk):
    i = pl.program_id(0)

    def bias_fn(t, start):
        return bias_ref[:, pl.ds(start, tk)].astype(F32)

    cols = [slice(hh * HEAD_DIM, (hh + 1) * HEAD_DIM) for hh in range(HEADS_PER_STEP)]
    _softmax_pipeline([q_ref[:, c] for c in cols], [(c, c) for c in cols], k_ref, v_ref, scratch, tq=tq, tk=tk,
                      first_chunk=i, first_keep=None, n_rest=i, bias_fn=bias_fn)
    for t, c in enumerate(cols):
        acc = scratch[0][t]
        o_ref[:, c] = (acc[:, :HEAD_DIM] / acc[:, HEAD_DIM:]).astype(o_ref.dtype)


def _dsa_attention(q, k, v, bias, *, tq=512):
    s = q.shape[0]
    tq = min(tq, s)
    tk = tq
    body = functools.partial(_dsa_attn_body, tq=tq, tk=tk)
    return pl.pallas_call(
        body,
        grid=(s // tq, N_HEADS // HEADS_PER_STEP),
        in_specs=[pl.BlockSpec((tq, HEADS_PER_STEP * HEAD_DIM), lambda i, h: (i, h)),
                  pl.BlockSpec((s, HEADS_PER_STEP * HEAD_DIM), lambda i, h: (0, h)),
                  pl.BlockSpec((s, HEADS_PER_STEP * HEAD_DIM), lambda i, h: (0, h)),
                  pl.BlockSpec((tq, s), lambda i, h: (i, 0))],
        out_specs=pl.BlockSpec((tq, HEADS_PER_STEP * HEAD_DIM), lambda i, h: (i, h)),
        out_shape=jax.ShapeDtypeStruct((s, N_HEADS * HEAD_DIM), BF16),
        scratch_shapes=_softmax_scratch(HEADS_PER_STEP, tq, tk),
        compiler_params=_cparams("arbitrary", "arbitrary"),
        name="attn_dsa",
    )(q, k, v, bias)


def _sortable_key(score):
    bits = lax.bitcast_convert_type(score, jnp.int32)
    return bits ^ ((bits >> 31) & 0x7FFFFFFF)


def _indexer_body(iq_ref, ik_ref, iw_ref, bias_ref, keys_ref, qz_ref, wb_ref, thr_ref, need_ref, *, tq, n_sel):
    i = pl.program_id(0)
    s_len = ik_ref.shape[0]
    n_chunks = s_len // tq
    lane = lax.broadcasted_iota(jnp.int32, (1, LANES), 1)
    iw = iw_ref[...]
    for p in range(IDX_HEADS // 2):
        pair = iq_ref[:, p * LANES:(p + 1) * LANES]
        zero = jnp.zeros_like(pair)
        qz_ref[2 * p] = jnp.where(lane < IDX_DIM, pair, zero)
        qz_ref[2 * p + 1] = jnp.where(lane >= IDX_DIM, pair, zero)
    for hh in range(IDX_HEADS):
        wb_ref[hh] = jnp.broadcast_to(iw[:, hh:hh + 1], (tq, LANES))

    def score_chunk(start):
        ik = ik_ref[pl.ds(start, tq), :]
        sc = jnp.zeros((tq, tq), F32)
        for hh in range(IDX_HEADS):
            rel = jnp.maximum(_dot_nt(qz_ref[hh], ik), 0.0)
            sc = sc + jnp.tile(wb_ref[hh], (1, tq // LANES)) * rel
        return sc

    def full_body(j, carry):
        start = pl.multiple_of(j * tq, tq)
        keys_ref[:, pl.ds(start, tq)] = _sortable_key(score_chunk(start))
        return carry

    lax.fori_loop(0, i, full_body, 0)
    diag = pl.multiple_of(i * tq, tq)
    sc = jnp.where(_causal_keep(tq, tq, 0), score_chunk(diag), -jnp.inf)
    keys_ref[:, pl.ds(diag, tq)] = _sortable_key(sc)

    keys_ref[:, pl.ds(pl.multiple_of((i + 1) * tq, tq), tq)] = jnp.full((tq, tq), KEY_NEG_INF, jnp.int32)
    strips = [pl.ds(r0, BISECT_ROWS) for r0 in range(0, tq, BISECT_ROWS)]
    n_trips = (i + 2) // 2
    lanes_per_trip = 2 * tq // LANES

    def count_keys(rows, trial, strict):
        def count_body(j, cnt):
            start = pl.multiple_of(j * (2 * tq), 2 * tq)
            for g in range(lanes_per_trip):
                kk = keys_ref[rows, pl.ds(start + g * LANES, LANES)]
                cnt = cnt + ((kk > trial) if strict else (kk >= trial)).astype(jnp.int32)
            return cnt

        return lax.fori_loop(0, n_trips, count_body, jnp.zeros((BISECT_ROWS, LANES), jnp.int32))

    def key_bounds(rows):
        def max_body(j, gm):
            start = pl.multiple_of(j * (2 * tq), 2 * tq)
            gm = list(gm)
            for g in range(lanes_per_trip):
                gm[g % 2] = jnp.maximum(gm[g % 2], keys_ref[rows, pl.ds(start + g * LANES, LANES)])
            return tuple(gm)

        init = jnp.full((BISECT_ROWS, LANES), KEY_NEG_INF, jnp.int32)
        g0, g1 = lax.fori_loop(0, n_trips, max_body, (init, init))
        lo = jnp.min(jnp.minimum(g0, g1), axis=1, keepdims=True)
        hi = jnp.max(jnp.maximum(g0, g1), axis=1, keepdims=True)
        lo = jnp.maximum(lo, KEY_NEG_INF + 1) + jnp.zeros((BISECT_ROWS, LANES), jnp.int32)
        return lo, jnp.maximum(hi, lo) + 1

    def open_rows(lo, hi):
        width = hi - lo
        return ((width != 0) & (width != 1)).astype(jnp.int32)

    def halve(bounds):
        mids = [lo + lax.shift_right_logical(hi - lo, 1) for lo, hi in bounds]
        counts = [count_keys(rows, mid, False) for rows, mid in zip(strips, mids)]
        out = []
        for cnt, mid, (lo, hi) in zip(counts, mids, bounds):
            total = jnp.sum(cnt, axis=1, keepdims=True)
            enough = total >= n_sel
            out.append((jnp.where(enough, mid, lo), jnp.where(total == n_sel, mid + 1, jnp.where(enough, hi, mid))))
        return out

    def flat(bounds):
        return tuple(b for pair in bounds for b in pair)

    def pairs(flat_bounds):
        return [(flat_bounds[2 * t], flat_bounds[2 * t + 1]) for t in range(len(strips))]

    bounds = lax.fori_loop(0, BISECT_UNCHECKED, lambda _, fb: flat(halve(pairs(fb))),
                           flat([key_bounds(rows) for rows in strips]))

    def still_open(state):
        it, fb = state[0], state[1:]
        n_open = functools.reduce(jnp.maximum, [open_rows(lo, hi) for lo, hi in pairs(fb)])
        return (it < 32) & (jnp.max(n_open) > 0)

    bounds = lax.while_loop(still_open, lambda state: (state[0] + 1,) + flat(halve(pairs(state[1:]))),
                            (jnp.int32(0),) + bounds)[1:]

    surplus = jnp.zeros((BISECT_ROWS, LANES), jnp.int32)
    for rows, (thr, _) in zip(strips, pairs(bounds)):
        above = jnp.sum(count_keys(rows, thr, True), axis=1, keepdims=True)
        tied = jnp.sum(count_keys(rows, thr, False), axis=1, keepdims=True) - above
        need = n_sel - above
        thr_ref[rows, :] = thr
        need_ref[rows, :] = need.astype(F32) + jnp.zeros((BISECT_ROWS, LANES), F32)
        surplus = jnp.maximum(surplus, tied - need)
    any_surplus = jnp.max(surplus) > 0
    n_grp = tq // LANES

    @pl.when(jnp.logical_not(any_surplus))
    def _write_no_ties():
        def write_body(j, carry):
            start = pl.multiple_of(j * tq, tq)
            thr = thr_ref[...]
            for g in range(n_grp):
                cols = pl.ds(start + g * LANES, LANES)
                bias_ref[:, cols] = jnp.where(keys_ref[:, cols] >= thr, 0.0, NEG).astype(bias_ref.dtype)
            return carry

        lax.fori_loop(0, i + 1, write_body, 0)

    @pl.when(any_surplus)
    def _write_with_ties():
        ca = lax.broadcasted_iota(jnp.int32, (tq, tq), 0)
        cb = lax.broadcasted_iota(jnp.int32, (tq, tq), 1)
        prefix_ones = (ca <= cb).astype(BF16)
        all_ones = jnp.ones((tq, LANES), BF16)

        def write_body(j, seen):
            start = pl.multiple_of(j * tq, tq)
            kk = keys_ref[:, pl.ds(start, tq)]
            thr = jnp.concatenate([thr_ref[...]] * n_grp, axis=1)
            eq = kk == thr
            eq_b = jnp.where(eq, 1.0, 0.0).astype(BF16)
            rank = jnp.dot(eq_b, prefix_ones, preferred_element_type=F32) + jnp.concatenate([seen] * n_grp, axis=1)
            take = (kk > thr) | (eq & (rank <= jnp.concatenate([need_ref[...]] * n_grp, axis=1)))
            bias_ref[:, pl.ds(start, tq)] = jnp.where(take, 0.0, NEG).astype(bias_ref.dtype)
            return seen + jnp.dot(eq_b, all_ones, preferred_element_type=F32)

        lax.fori_loop(0, i + 1, write_body, jnp.zeros((tq, LANES), F32))

    def fill_body(j, carry):
        bias_ref[:, pl.ds(pl.multiple_of(j * tq, tq), tq)] = jnp.full((tq, tq), NEG, bias_ref.dtype)
        return carry

    lax.fori_loop(i + 1, n_chunks, fill_body, 0)


def _dsa_select_bias(iq, ik2, iw, n_sel, tq=256):
    s = iq.shape[0]
    tq = min(tq, s)
    body = functools.partial(_indexer_body, tq=tq, n_sel=n_sel)
    return pl.pallas_call(
        body,
        grid=(s // tq,),
        in_specs=[pl.BlockSpec((tq, IDX_HEADS * IDX_DIM), lambda i: (i, 0)),
                  pl.BlockSpec((s, LANES), lambda i: (0, 0)),
                  pl.BlockSpec((tq, LANES), lambda i: (i, 0))],
        out_specs=pl.BlockSpec((tq, s), lambda i: (i, 0)),
        out_shape=jax.ShapeDtypeStruct((s, s), BF16),
        scratch_shapes=[pltpu.VMEM((tq, s + tq), jnp.int32),
                        pltpu.VMEM((IDX_HEADS, tq, LANES), BF16),
                        pltpu.VMEM((IDX_HEADS, tq, LANES), F32),
                        pltpu.VMEM((tq, LANES), jnp.int32),
                        pltpu.VMEM((tq, LANES), F32)],
        compiler_params=_cparams("arbitrary"),
        name="dsa_select",
    )(iq, ik2, iw)


def _fox_cum_body(f_ref, b_ref, o_ref, carry_ref, *, tc):
    @pl.when(pl.program_id(0) == 0)
    def _():
        carry_ref[...] = jnp.zeros(carry_ref.shape, F32)

    z = f_ref[...] + b_ref[...]
    lf = jnp.minimum(z, 0.0) - jnp.log(1.0 + jnp.exp(-jnp.abs(z)))
    row = lax.broadcasted_iota(jnp.int32, (tc, tc), 0)
    col = lax.broadcasted_iota(jnp.int32, (tc, tc), 1)
    tri = (row >= col).astype(BF16)
    hi = lf.astype(BF16)
    r1 = lf - hi.astype(F32)
    mid = r1.astype(BF16)
    lo = (r1 - mid.astype(F32)).astype(BF16)
    cs = (jnp.dot(tri, hi, preferred_element_type=F32) + jnp.dot(tri, mid, preferred_element_type=F32)
          + jnp.dot(tri, lo, preferred_element_type=F32)) + carry_ref[...]
    o_ref[...] = cs
    carry_ref[...] = cs[tc - 1:tc, :]


def _fox_cumsum(f_raw, b_pad, tc=512):
    s = f_raw.shape[0]
    tc = min(tc, s)
    return pl.pallas_call(
        functools.partial(_fox_cum_body, tc=tc),
        grid=(s // tc,),
        in_specs=[pl.BlockSpec((tc, LANES), lambda m: (m, 0)), pl.BlockSpec((1, LANES), lambda m: (0, 0))],
        out_specs=pl.BlockSpec((tc, LANES), lambda m: (m, 0)),
        out_shape=jax.ShapeDtypeStruct((s, LANES), F32),
        scratch_shapes=[pltpu.VMEM((1, LANES), F32)],
        compiler_params=_cparams("arbitrary"),
        name="fox_cumsum",
    )(f_raw, b_pad)


def _rope_tables(positions, rot_dim, period):
    half = rot_dim // 2
    inv_freq = ROPE_THETA ** (-jnp.arange(0, rot_dim, 2, dtype=F32) / rot_dim)
    ang = positions.astype(F32)[:, None] * inv_freq
    cos, sin = jnp.cos(ang), jnp.sin(ang)
    lp = jnp.arange(LANES) % period
    in_x1 = lp < half
    in_x2 = (lp >= half) & (lp < 2 * half)
    idx = jnp.where(in_x1, lp, jnp.clip(lp - half, 0, half - 1))
    cos_l, sin_l = cos[:, idx], sin[:, idx]
    c = jnp.where(in_x1 | in_x2, cos_l, 1.0)
    s1 = jnp.where(in_x2, sin_l, 0.0)
    s2 = jnp.where(in_x1, -sin_l, 0.0)
    return c, s1, s2


def _pad_cols(w, width):
    return jnp.pad(w, ((0, 0), (0, width - w.shape[1])))


def _mla_mixer(h, tabs, w_in, q_a_g, kv_a_g, w_q_b, w_kv_b, q_g, k_g):
    s = h.shape[0]
    nope = HEAD_DIM
    scale = MLA_HEAD ** -0.5 * LOG2E
    gains = jnp.concatenate([q_a_g, kv_a_g])
    c_norm = _proj(h, w_in, 0, 0, 2 * MLA_LORA, tn=MLA_LORA, gain=gains, gw=MLA_LORA, real=MLA_LORA)
    kpe = _proj(h, _pad_cols(w_in[0][:, 2 * MLA_LORA:], LANES)[None], 0, 0, LANES, tn=LANES, out_dtype=F32,
                epi=_epi_plain)
    wq = w_q_b[0].reshape(MLA_LORA, N_HEADS, MLA_HEAD)
    wq = jnp.concatenate([wq[:, :, MLA_ROPE:], wq[:, :, :MLA_ROPE],
                          jnp.zeros((MLA_LORA, N_HEADS, 2 * LANES - MLA_HEAD), F32)], axis=2)
    wq = wq.reshape(1, MLA_LORA, N_HEADS * 2 * LANES)
    qg = jnp.concatenate([q_g[MLA_ROPE:], q_g[:MLA_ROPE], jnp.zeros((2 * LANES - MLA_HEAD,), F32)])
    q = _proj(c_norm, wq, 0, 0, N_HEADS * 2 * LANES, tn=TN_WIDE, gain=jnp.tile(qg, N_HEADS), tabs=tabs,
              gw=2 * LANES, real=MLA_HEAD, half=MLA_ROPE // 2, rope_groups=lambda g: g % 2 == 1, scale=scale)
    tm, tn = min(TM, s), TN
    aux = [(kpe, (tm, LANES), lambda n, m: (m, 0)),
           (k_g[MLA_ROPE:].reshape(1, nope), (1, nope), lambda n, m: (0, 0)),
           (jnp.pad(k_g[:MLA_ROPE], (0, LANES - MLA_ROPE)).reshape(1, LANES), (1, LANES), lambda n, m: (0, 0))]
    aux += _row_tab_aux(tabs, tm)
    outs = [((s, N_HEADS * 2 * LANES), BF16, (tm, tn), lambda n, m: (m, n)),
            ((s, N_HEADS * HEAD_DIM), BF16, (tm, tn // 2), lambda n, m: (m, n))]
    k, v = _mm(c_norm, [(w_kv_b, 0, 0)], aux, outs, _epi_mla_kv, tm=tm, tn=tn,
               n_tiles=N_HEADS * 2 * LANES // tn, a_blk=1, name="mla_kv")
    return _dense_attention(q, k, v, mode="mla", dq=2 * LANES)


def _dsa_mixer(h, tabs_head, tabs_idx, w_in, q_g, k_g, idx_k_g):
    s = h.shape[0]
    hd = N_HEADS * HEAD_DIM
    ih = IDX_HEADS * IDX_DIM
    half = HEAD_DIM // 8
    q = _proj(h, w_in, 0, 0, hd, tn=TN_WIDE, gain=jnp.tile(q_g, N_HEADS), tabs=tabs_head, half=half,
              scale=HEAD_DIM ** -0.5 * LOG2E)
    k = _proj(h, w_in, 0, hd, hd, tn=TN_WIDE, gain=jnp.tile(k_g, N_HEADS), tabs=tabs_head, half=half)
    v = _proj(h, w_in, 0, 2 * hd, hd, tn=TN_WIDE, epi=_epi_plain)
    iq = _proj(h, w_in, 0, 3 * hd, ih, tn=TN_WIDE, tabs=tabs_idx, half=IDX_DIM // 8)
    w_ik = w_in[0][:, 3 * hd + ih:3 * hd + ih + IDX_DIM]
    ik2 = _proj(h, jnp.concatenate([w_ik, w_ik], axis=1)[None], 0, 0, LANES, tn=LANES,
                gain=jnp.tile(idx_k_g, 2), tabs=tabs_idx, half=IDX_DIM // 8)
    w_iw = _pad_cols(w_in[0][:, 3 * hd + ih + IDX_DIM:], LANES)[None]
    iw = _proj(h, w_iw, 0, 0, LANES, tn=LANES, out_dtype=F32,
               epi=functools.partial(_epi_heads, gw=LANES, real=LANES, half=0, rope_groups=None,
                                     scale=IDX_HEADS ** -0.5 * IDX_DIM ** -0.5, use_gain=False, use_rope=False))
    bias = _dsa_select_bias(iq, ik2, iw, min(IDX_TOPK, s // 4))
    return _dsa_attention(q, k, v, bias)


def _diff_mixer(h, tabs, layer_idx, w_in, q_g, k_g, lam_params, subln_g):
    w = N_HEADS * 2 * DIFF_DIM
    half = DIFF_DIM // 8
    q = _proj(h, w_in, 0, 0, w, tn=TN_WIDE, gain=jnp.tile(q_g, 2 * N_HEADS), tabs=tabs, gw=DIFF_DIM, real=DIFF_DIM,
              half=half, scale=DIFF_DIM ** -0.5 * LOG2E)
    k = _proj(h, w_in, 0, w, w, tn=TN_WIDE, gain=jnp.tile(k_g, 2 * N_HEADS), tabs=tabs, gw=DIFF_DIM, real=DIFF_DIM,
              half=half)
    v = _proj(h, w_in, 0, 2 * w, w, tn=TN_WIDE, epi=_epi_plain)
    lam_init = 0.8 - 0.6 * math.exp(-0.3 * layer_idx)
    return _dense_attention(q, k, v, mode="diff", dq=HEAD_DIM, extra=(lam_params, subln_g.reshape(1, HEAD_DIM)),
                            lam_init=lam_init)


def _fox_mixer(h, w_in, b_f, q_g, k_g):
    s = h.shape[0]
    hd = N_HEADS * HEAD_DIM
    q = _proj(h, w_in, 0, 0, hd, tn=TN_WIDE, gain=jnp.tile(q_g, N_HEADS), scale=HEAD_DIM ** -0.5 * LOG2E)
    k = _proj(h, w_in, 0, hd, hd, tn=TN_WIDE, gain=jnp.tile(k_g, N_HEADS))
    v = _proj(h, w_in, 0, 2 * hd, hd, tn=TN_WIDE, epi=_epi_plain)
    w_f = _pad_cols(w_in[0][:, 3 * hd:3 * hd + N_HEADS], LANES)[None]
    f_raw = _proj(h, w_f, 0, 0, LANES, tn=LANES, out_dtype=F32, epi=_epi_plain)
    gate = _proj(h, w_in[:, :, 3 * hd + N_HEADS:], 0, 0, hd, tn=TN_WIDE, out_dtype=F32, epi=_epi_sigmoid, tm=TM)
    cum = _fox_cumsum(f_raw, jnp.pad(b_f, (0, LANES - N_HEADS)).reshape(1, LANES))
    cumt = cum[:, :N_HEADS].T.reshape(N_HEADS, 1, s)
    return _dense_attention(q, k, v, mode="fox", dq=HEAD_DIM, extra=(cum, cumt, gate))


def _swiglu_ffn(h, x, gate, w_gate_up, w_down, layer):
    s = h.shape[0]
    tm, tn = min(TM_LONG, s), TN
    outs = [((s, FFN_HIDDEN), BF16, (tm, tn), lambda n, m: (m, n))]
    act = _mm(h, [(w_gate_up, layer, 0), (w_gate_up, layer, FFN_HIDDEN // tn)], [], outs, _epi_swiglu,
              tm=tm, tn=tn, n_tiles=FFN_HIDDEN // tn, name="ffn_gate_up")[0]
    return _residual_mm(act, w_down, layer, x, gate, tm=TM_DEEP)


def kernel(x, c, positions, ln_mix_g, ln_ffn_g, ada_w, ada_b, ffn_w_gate_up, ffn_w_down, mla_w_in, mla_q_a_g, mla_kv_a_g, mla_w_q_b, mla_w_kv_b, mla_q_g, mla_k_g, mla_w_out, dsa_w_in, dsa_q_g, dsa_k_g, dsa_idx_k_g, dsa_w_out, diff_w_in, diff_q_g, diff_k_g, diff_lambda_q1, diff_lambda_k1, diff_lambda_q2, diff_lambda_k2, diff_subln_g, diff_w_out, fox_w_in, fox_b_f, fox_q_g, fox_k_g, fox_w_out):
    batch, s, d = x.shape
    assert batch == 1 and d == D_MODEL
    depth = ada_w.shape[0]
    pos = positions[0]
    tabs_head = _rope_tables(pos, HEAD_DIM // 4, LANES)
    tabs_small = _rope_tables(pos, IDX_DIM // 4, IDX_DIM)
    tabs_mla = _rope_tables(pos, MLA_ROPE, LANES)
    mod = _ada_mod(c, ada_w, ada_b)
    xs = x[0]
    for i in range(depth):
        sh1, sc1, g1, sh2, sc2, g2 = [mod[i, t * d:(t + 1) * d] for t in range(6)]
        h = _normmod(xs, ln_mix_g[i], sc1, sh1)
        kind, j = i % 4, i // 4
        if kind == 0:
            o = _mla_mixer(h, tabs_mla, mla_w_in[j:j + 1], mla_q_a_g[j], mla_kv_a_g[j], mla_w_q_b[j:j + 1],
                           mla_w_kv_b[j:j + 1], mla_q_g[j], mla_k_g[j])
            w_out = mla_w_out
        elif kind == 1:
            o = _dsa_mixer(h, tabs_head, tabs_small, dsa_w_in[j:j + 1], dsa_q_g[j], dsa_k_g[j], dsa_idx_k_g[j])
            w_out = dsa_w_out
        elif kind == 2:
            lam_params = jnp.stack([diff_lambda_q1[j], diff_lambda_k1[j], diff_lambda_q2[j], diff_lambda_k2[j]])
            o = _diff_mixer(h, tabs_small, i, diff_w_in[j:j + 1], diff_q_g[j], diff_k_g[j], lam_params,
                            diff_subln_g[j])
            w_out = diff_w_out
        else:
            o = _fox_mixer(h, fox_w_in[j:j + 1], fox_b_f[j], fox_q_g[j], fox_k_g[j])
            w_out = fox_w_out
        xs = _residual_mm(o, w_out, j, xs, g1, tn=TN_WIDE)
        h = _normmod(xs, ln_ffn_g[i], sc2, sh2)
        xs = _swiglu_ffn(h, xs, g2, ffn_w_gate_up, ffn_w_down, i)
    return xs[None]
```

```python
import functools
import math

import jax
import jax.numpy as jnp
from jax import lax
from jax.experimental import pallas as pl
from jax.experimental.pallas import tpu as pltpu

F32 = jnp.float32
BF16 = jnp.bfloat16

D_MODEL = 2048
N_HEADS = 16
HEAD_DIM = 128
ROPE_THETA = 500000.0
RMS_EPS = 1e-6
FFN_HIDDEN = 5632
MLA_LORA = 512
MLA_ROPE = 64
MLA_HEAD = 192
IDX_HEADS = 16
IDX_DIM = 64
IDX_TOPK = 256
DIFF_DIM = 64

LANES = 128
LOG2E = 1.4426950408889634
NEG = -1e30
VMEM_LIMIT_BYTES = 56 * 1024 * 1024
TM = 1024
TM_LONG = 2048
TM_DEEP = 512
SUB_ROWS = 256
TN = 512
TN_WIDE = 1024
TN_ADA = 1536
TQ_ATTN = 512
TQ_SELECT = 256
T_CUMSUM = 512
HEADS_PER_STEP = 2
BISECT_ROWS = 128
BISECT_UNCHECKED = 8
INT_MIN = -2 ** 31
KEY_NEG_INF = (0xFF800000 ^ 0x7FFFFFFF) - 2 ** 32


def _cparams(*sem):
    return pltpu.CompilerParams(dimension_semantics=sem, vmem_limit_bytes=VMEM_LIMIT_BYTES)


def _dot_nt(a, b):
    return lax.dot_general(a, b, (((1,), (1,)), ((), ())), preferred_element_type=F32)


def _sigmoid(z):
    return 1.0 / (1.0 + jnp.exp(-z))


def _mm_body(*refs, n_w, n_aux, n_out, epi, sub):
    a_ref = refs[0]
    w_refs = refs[1:1 + n_w]
    aux_refs = refs[1 + n_w:1 + n_w + n_aux]
    out_refs = refs[1 + n_w + n_aux:1 + n_w + n_aux + n_out]
    wb_refs = refs[1 + n_w + n_aux + n_out:]
    tm = a_ref.shape[0]

    @pl.when(pl.program_id(1) == 0)
    def _cast_weights():
        for w_ref, wb_ref in zip(w_refs, wb_refs):
            wb_ref[...] = w_ref[...].astype(BF16)

    for r in range(tm // sub):
        rows = pl.ds(r * sub, sub)
        accs = [jnp.dot(a_ref[rows, :], wb_ref[...], preferred_element_type=F32) for wb_ref in wb_refs]
        epi(accs, [ref.at[rows] if ref.shape[0] == tm else ref for ref in aux_refs],
            [ref.at[rows] for ref in out_refs])


def _mm(a, ws, aux, outs, epi, *, tm, tn, n_tiles, a_blk=0, name="mm", sub=SUB_ROWS):
    s = a.shape[0]
    k = ws[0][0].shape[1]
    tm = min(tm, s)
    in_specs = [pl.BlockSpec((tm, k), lambda n, m: (m, a_blk))]
    operands = [a]
    for w, layer, off in ws:
        assert w.shape[1] == k
        in_specs.append(pl.BlockSpec((None, k, tn), lambda n, m, layer=layer, off=off: (layer, 0, n + off)))
        operands.append(w)
    for arr, block, imap in aux:
        in_specs.append(pl.BlockSpec(block, imap))
        operands.append(arr)
    out_shape = [jax.ShapeDtypeStruct(shape, dtype) for shape, dtype, _, _ in outs]
    out_specs = [pl.BlockSpec(block, imap) for _, _, block, imap in outs]
    body = functools.partial(_mm_body, n_w=len(ws), n_aux=len(aux), n_out=len(outs), epi=epi, sub=min(sub, tm))
    res = pl.pallas_call(
        body,
        grid=(n_tiles, s // tm),
        in_specs=in_specs,
        out_specs=out_specs,
        out_shape=out_shape,
        scratch_shapes=[pltpu.VMEM((k, tn), BF16) for _ in ws],
        compiler_params=_cparams("arbitrary", "arbitrary"),
        name=name,
    )(*operands)
    return res


def _rope_group(y, tabs, half):
    c, s1, s2 = tabs
    return y * c + pltpu.roll(y, half, 1) * s1 + pltpu.roll(y, LANES - half, 1) * s2


def _group_inv_rms(y, gw, real):
    tn = y.shape[1]
    sq = y * y
    n_groups = tn // LANES
    if gw == DIFF_DIM:
        lane = lax.broadcasted_iota(jnp.int32, (1, LANES), 1)
        lo = lane < DIFF_DIM
        out = []
        for g in range(n_groups):
            sg = sq[:, g * LANES:(g + 1) * LANES]
            s_lo = jnp.sum(jnp.where(lo, sg, 0.0), axis=1, keepdims=True)
            s_hi = jnp.sum(jnp.where(lo, 0.0, sg), axis=1, keepdims=True)
            out.append(lax.rsqrt(jnp.where(lo, s_lo, s_hi) * (1.0 / real) + RMS_EPS))
        return out
    per = gw // LANES
    out = []
    for h in range(n_groups // per):
        folded = sq[:, h * gw:h * gw + LANES]
        for t in range(1, per):
            folded = folded + sq[:, h * gw + t * LANES:h * gw + (t + 1) * LANES]
        r = lax.rsqrt(jnp.sum(folded, axis=1, keepdims=True) * (1.0 / real) + RMS_EPS)
        out.extend([r] * per)
    return out


def _epi_heads(accs, aux_refs, out_refs, *, gw, real, half, rope_groups, scale, use_gain, use_rope):
    y = accs[0]
    tn = y.shape[1]
    idx = 0
    gain = None
    if use_gain:
        gain = aux_refs[idx][...]
        idx += 1
        inv = _group_inv_rms(y, gw, real)
    if use_rope:
        tabs = tuple(aux_refs[idx + t][...] for t in range(3))
    for g in range(tn // LANES):
        yg = y[:, g * LANES:(g + 1) * LANES]
        if use_gain:
            yg = yg * inv[g] * gain[:, g * LANES:(g + 1) * LANES]
        if use_rope and rope_groups(g):
            yg = _rope_group(yg, tabs, half)
        if scale != 1.0:
            yg = yg * scale
        out_refs[0][:, g * LANES:(g + 1) * LANES] = yg.astype(out_refs[0].dtype)


def _epi_plain(accs, aux_refs, out_refs):
    out_refs[0][...] = accs[0].astype(out_refs[0].dtype)


def _epi_sigmoid(accs, aux_refs, out_refs):
    out_refs[0][...] = _sigmoid(accs[0]).astype(out_refs[0].dtype)


def _epi_residual(accs, aux_refs, out_refs):
    x_ref, g_ref = aux_refs
    out_refs[0][...] = x_ref[...] + g_ref[...] * accs[0]


def _epi_swiglu(accs, aux_refs, out_refs):
    gate, up = accs
    out_refs[0][...] = (gate * _sigmoid(gate) * up).astype(out_refs[0].dtype)


def _epi_mla_kv(accs, aux_refs, out_refs):
    kpe_ref, g_nope_ref, g_pe_ref, c_ref, s1_ref, s2_ref = aux_refs
    k_ref, v_ref = out_refs
    y = accs[0]
    kpe = kpe_ref[...]
    ss_pe = jnp.sum(kpe * kpe, axis=1, keepdims=True)
    tabs = (c_ref[...], s1_ref[...], s2_ref[...])
    g_nope = g_nope_ref[...]
    pe_roped = _rope_group(kpe * g_pe_ref[...], tabs, MLA_ROPE // 2)
    for h in range(y.shape[1] // (2 * LANES)):
        kn = y[:, 2 * h * LANES:(2 * h + 1) * LANES]
        v = y[:, (2 * h + 1) * LANES:(2 * h + 2) * LANES]
        r = lax.rsqrt((jnp.sum(kn * kn, axis=1, keepdims=True) + ss_pe) * (1.0 / MLA_HEAD) + RMS_EPS)
        k_ref[:, 2 * h * LANES:(2 * h + 1) * LANES] = (kn * r * g_nope).astype(k_ref.dtype)
        k_ref[:, (2 * h + 1) * LANES:(2 * h + 2) * LANES] = (pe_roped * r).astype(k_ref.dtype)
        v_ref[:, h * LANES:(h + 1) * LANES] = v.astype(v_ref.dtype)


def _row_tab_aux(tabs, tm):
    return [(t, (tm, LANES), lambda n, m: (m, 0)) for t in tabs]


def _proj(a, w, layer, col_off, width, *, tn, out_dtype=BF16, gain=None, tabs=None, gw=LANES, real=LANES,
          half=0, rope_groups=lambda g: True, scale=1.0, epi=None, tm=TM_LONG, a_blk=0):
    s = a.shape[0]
    tm = min(tm, s)
    assert col_off % tn == 0 and width % tn == 0
    aux = []
    if gain is not None:
        aux.append((gain.reshape(1, width).astype(F32), (1, tn), lambda n, m: (0, n)))
    if tabs is not None:
        aux.extend(_row_tab_aux(tabs, tm))
    if epi is None:
        epi = functools.partial(_epi_heads, gw=gw, real=real, half=half, rope_groups=rope_groups, scale=scale,
                                use_gain=gain is not None, use_rope=tabs is not None)
    outs = [((s, width), out_dtype, (tm, tn), lambda n, m: (m, n))]
    return _mm(a, [(w, layer, col_off // tn)], aux, outs, epi, tm=tm, tn=tn, n_tiles=width // tn, a_blk=a_blk,
               name="proj")[0]


def _residual_mm(a, w, layer, x, gate, *, tn=TN, tm=TM):
    s, d = x.shape
    tm = min(tm, s)
    aux = [(x, (tm, tn), lambda n, m: (m, n)), (gate.reshape(1, d), (1, tn), lambda n, m: (0, n))]
    outs = [((s, d), F32, (tm, tn), lambda n, m: (m, n))]
    return _mm(a, [(w, layer, 0)], aux, outs, _epi_residual, tm=tm, tn=tn, n_tiles=d // tn, name="residual_mm")[0]


def _ada_body(c_ref, w_ref, b_ref, o_ref):
    c = c_ref[...]
    cond = c * _sigmoid(c)
    o_ref[...] = jnp.dot(cond.astype(BF16), w_ref[...].astype(BF16), preferred_element_type=F32) + b_ref[...]


def _ada_mod(c, ada_w, ada_b):
    depth, d, n = ada_w.shape
    tn = TN_ADA
    c8 = jnp.broadcast_to(c, (8, d))
    out = pl.pallas_call(
        _ada_body,
        grid=(depth, n // tn),
        in_specs=[pl.BlockSpec((8, d), lambda l, j: (0, 0)),
                  pl.BlockSpec((None, d, tn), lambda l, j: (l, 0, j)),
                  pl.BlockSpec((None, 1, tn), lambda l, j: (l, 0, j))],
        out_specs=pl.BlockSpec((None, 8, tn), lambda l, j: (l, 0, j)),
        out_shape=jax.ShapeDtypeStruct((depth, 8, n), F32),
        compiler_params=_cparams("arbitrary", "arbitrary"),
        name="ada_mod",
    )(c8, ada_w, ada_b.reshape(depth, 1, n))
    return out[:, 0, :]


def _normmod_body(x_ref, g_ref, sc_ref, sh_ref, o_ref):
    x = x_ref[...]
    y = x * lax.rsqrt(jnp.mean(x * x, axis=1, keepdims=True) + RMS_EPS) * g_ref[...]
    o_ref[...] = (y * (1.0 + sc_ref[...]) + sh_ref[...]).astype(o_ref.dtype)


def _normmod(x, g, sc, sh, tm=TM):
    s, d = x.shape
    tm = min(tm, s)
    vec = pl.BlockSpec((1, d), lambda m: (0, 0))
    return pl.pallas_call(
        _normmod_body,
        grid=(s // tm,),
        in_specs=[pl.BlockSpec((tm, d), lambda m: (m, 0)), vec, vec, vec],
        out_specs=pl.BlockSpec((tm, d), lambda m: (m, 0)),
        out_shape=jax.ShapeDtypeStruct((s, d), BF16),
        compiler_params=_cparams("arbitrary"),
        name="normmod",
    )(x, g.reshape(1, d), sc.reshape(1, d), sh.reshape(1, d))


def _softmax_scratch(n_maps, tq, tk):
    return [pltpu.VMEM((n_maps, tq, 2 * HEAD_DIM), F32), pltpu.VMEM((n_maps, tq, LANES), F32),
            pltpu.VMEM((n_maps, 2, tq, tk), F32), pltpu.VMEM((n_maps, 2, 2, tq, LANES), F32)]


def _softmax_pipeline(qs, kv_cols, k_ref, v_ref, scratch, *, tq, tk, first_chunk, first_keep, n_rest, bias_fn):
    acc_ref, m_ref, s_ref, st_ref = scratch
    ones = jnp.ones((tk, LANES), BF16)
    n_grp = tk // LANES
    maps = range(len(qs))

    def chunk_start(c):
        return pl.multiple_of(c * tk, tk)

    def stage1(c, slot, keep):
        start = chunk_start(c)
        for t in maps:
            s = _dot_nt(qs[t], k_ref[pl.ds(start, tk), kv_cols[t][0]])
            if bias_fn is not None:
                s = s + bias_fn(t, start)
            if keep is not None:
                s = jnp.where(keep, s, NEG)
            m_run = m_ref[t]
            m_new = jnp.maximum(m_run, jnp.max(s, axis=1, keepdims=True))
            s_ref[t, slot] = s
            st_ref[t, slot, 0] = m_new
            st_ref[t, slot, 1] = jnp.exp2(m_run - m_new)
            m_ref[t] = m_new

    def stage2(c, slot):
        start = chunk_start(c)
        for t in maps:
            v1 = jnp.concatenate([v_ref[pl.ds(start, tk), kv_cols[t][1]], ones], axis=1)
            m_new = st_ref[t, slot, 0]
            alpha = st_ref[t, slot, 1]
            p = jnp.concatenate([jnp.exp2(s_ref[t, slot, :, g * LANES:(g + 1) * LANES] - m_new)
                                 for g in range(n_grp)], axis=1)
            pv = jnp.dot(p.astype(BF16), v1, preferred_element_type=F32)
            acc_ref[t] = jnp.concatenate([alpha, alpha], axis=1) * acc_ref[t] + pv

    acc_ref[...] = jnp.zeros(acc_ref.shape, F32)
    m_ref[...] = jnp.full(m_ref.shape, NEG, F32)
    stage1(first_chunk, 0, first_keep)

    def two_chunks(c0):
        stage1(c0, 1, None)
        stage2(jnp.where(c0 == 0, first_chunk, c0 - 1), 0)
        stage1(c0 + 1, 0, None)
        stage2(c0, 1)

    def body(t4, carry):
        two_chunks(4 * t4)
        two_chunks(4 * t4 + 2)
        return carry

    n_quads = n_rest // 4
    lax.fori_loop(0, n_quads, body, 0)
    n_pairs = n_rest // 2

    @pl.when(n_pairs > 2 * n_quads)
    def _pair_tail():
        two_chunks(4 * n_quads)

    pending = jnp.where(n_pairs == 0, first_chunk, 2 * n_pairs - 1)

    @pl.when(n_rest % 2 == 1)
    def _odd_tail():
        stage1(n_rest - 1, 1, None)
        stage2(pending, 0)
        stage2(n_rest - 1, 1)

    @pl.when(n_rest % 2 == 0)
    def _even_tail():
        stage2(pending, 0)


def _causal_keep(tq, tk, col_shift):
    row = lax.broadcasted_iota(jnp.int32, (tq, 1), 0)
    col = lax.broadcasted_iota(jnp.int32, (1, tk), 1)
    return row >= col + col_shift


def _dense_attn_body(*refs, mode, tq, tk, lam_init):
    if mode == "mla":
        q_ref, k_ref, v_ref, o_ref = refs[:4]
        scratch = refs[4:]
    elif mode == "fox":
        q_ref, k_ref, v_ref, cum_ref, cumt_ref, gate_ref, o_ref = refs[:7]
        scratch = refs[7:]
    else:
        q_ref, k_ref, v_ref, lam_ref, subg_ref, o_ref = refs[:6]
        scratch = refs[6:]
    hg = pl.program_id(0)
    i = pl.program_id(1)
    acc_ref = scratch[0]
    dq = q_ref.shape[1] // HEADS_PER_STEP
    heads = range(HEADS_PER_STEP)
    lane = lax.broadcasted_iota(jnp.int32, (1, LANES), 1)

    def head_cols(hh, width=HEAD_DIM):
        return slice(hh * width, (hh + 1) * width)

    qs, kv_cols = [], []
    for hh in heads:
        q = q_ref[:, head_cols(hh, dq)]
        if mode == "diff":
            zero = jnp.zeros_like(q)
            qs += [jnp.where(lane < DIFF_DIM, q, zero), jnp.where(lane >= DIFF_DIM, q, zero)]
            kv_cols += [(head_cols(hh, dq), head_cols(hh))] * 2
        else:
            qs.append(q)
            kv_cols.append((head_cols(hh, dq), head_cols(hh)))
    if mode == "fox":
        cum = cum_ref[...]
        cqs = [jnp.sum(jnp.where(lane == hg * HEADS_PER_STEP + hh, cum, 0.0), axis=1, keepdims=True) * LOG2E
               for hh in heads]

        def bias_fn(t, start):
            return cqs[t] - cumt_ref[t, :, pl.ds(start, tk)] * LOG2E
    else:
        bias_fn = None

    _softmax_pipeline(qs, kv_cols, k_ref, v_ref, scratch, tq=tq, tk=tk, first_chunk=i,
                      first_keep=_causal_keep(tq, tk, 0), n_rest=i, bias_fn=bias_fn)

    def normalised(t):
        acc = acc_ref[t]
        return acc[:, :HEAD_DIM] / acc[:, HEAD_DIM:]

    for hh in heads:
        if mode == "diff":
            lp = lam_ref[...]
            lam = (jnp.exp(jnp.sum(lp[0:1] * lp[1:2], axis=1, keepdims=True))
                   - jnp.exp(jnp.sum(lp[2:3] * lp[3:4], axis=1, keepdims=True)) + lam_init)
            o = normalised(2 * hh) - lam * normalised(2 * hh + 1)
            o = o * lax.rsqrt(jnp.mean(o * o, axis=1, keepdims=True) + RMS_EPS) * subg_ref[...]
            o = o * (1.0 - lam_init)
        else:
            o = normalised(hh)
            if mode == "fox":
                o = o * gate_ref[:, head_cols(hh)]
        o_ref[:, head_cols(hh)] = o.astype(o_ref.dtype)


def _dense_attention(q, k, v, *, mode, dq, extra=(), lam_init=0.0, tq=TQ_ATTN):
    s = q.shape[0]
    tq = min(tq, s)
    tk = tq
    assert s % tq == 0 and N_HEADS % HEADS_PER_STEP == 0
    g = HEADS_PER_STEP
    in_specs = [pl.BlockSpec((tq, g * dq), lambda h, i: (i, h)),
                pl.BlockSpec((s, g * dq), lambda h, i: (0, h)),
                pl.BlockSpec((s, g * HEAD_DIM), lambda h, i: (0, h))]
    operands = [q, k, v]
    if mode == "fox":
        cum, cumt, gate = extra
        in_specs += [pl.BlockSpec((tq, LANES), lambda h, i: (i, 0)),
                     pl.BlockSpec((g, 1, s), lambda h, i: (h, 0, 0)),
                     pl.BlockSpec((tq, g * HEAD_DIM), lambda h, i: (i, h))]
        operands += [cum, cumt, gate]
    elif mode == "diff":
        lam_params, subln_g = extra
        in_specs += [pl.BlockSpec((4, DIFF_DIM), lambda h, i: (0, 0)),
                     pl.BlockSpec((1, HEAD_DIM), lambda h, i: (0, 0))]
        operands += [lam_params, subln_g]
    n_maps = g * (2 if mode == "diff" else 1)
    body = functools.partial(_dense_attn_body, mode=mode, tq=tq, tk=tk, lam_init=lam_init)
    return pl.pallas_call(
        body,
        grid=(N_HEADS // g, s // tq),
        in_specs=in_specs,
        out_specs=pl.BlockSpec((tq, g * HEAD_DIM), lambda h, i: (i, h)),
        out_shape=jax.ShapeDtypeStruct((s, N_HEADS * HEAD_DIM), BF16),
        scratch_shapes=_softmax_scratch(n_maps, tq, tk),
        compiler_params=_cparams("arbitrary", "arbitrary"),
        name="attn_" + mode,
    )(*operands)


def _dsa_attn_body(q_ref, k_ref, v_ref, bias_ref, o_ref, *scratch, tq, tk):
    i = pl.program_id(0)

    def bias_fn(t, start):
        return bias_ref[:, pl.ds(start, tk)].astype(F32)

    cols = [slice(hh * HEAD_DIM, (hh + 1) * HEAD_DIM) for hh in range(HEADS_PER_STEP)]
    _softmax_pipeline([q_ref[:, c] for c in cols], [(c, c) for c in cols], k_ref, v_ref, scratch, tq=tq, tk=tk,
                      first_chunk=i, first_keep=None, n_rest=i, bias_fn=bias_fn)
    for t, c in enumerate(cols):
        acc = scratch[0][t]
        o_ref[:, c] = (acc[:, :HEAD_DIM] / acc[:, HEAD_DIM:]).astype(o_ref.dtype)


def _dsa_attention(q, k, v, bias, *, tq=TQ_ATTN):
    s = q.shape[0]
    tq = min(tq, s)
    tk = tq
    assert s % tq == 0 and N_HEADS % HEADS_PER_STEP == 0
    body = functools.partial(_dsa_attn_body, tq=tq, tk=tk)
    return pl.pallas_call(
        body,
        grid=(s // tq, N_HEADS // HEADS_PER_STEP),
        in_specs=[pl.BlockSpec((tq, HEADS_PER_STEP * HEAD_DIM), lambda i, h: (i, h)),
                  pl.BlockSpec((s, HEADS_PER_STEP * HEAD_DIM), lambda i, h: (0, h)),
                  pl.BlockSpec((s, HEADS_PER_STEP * HEAD_DIM), lambda i, h: (0, h)),
                  pl.BlockSpec((tq, s), lambda i, h: (i, 0))],
        out_specs=pl.BlockSpec((tq, HEADS_PER_STEP * HEAD_DIM), lambda i, h: (i, h)),
        out_shape=jax.ShapeDtypeStruct((s, N_HEADS * HEAD_DIM), BF16),
        scratch_shapes=_softmax_scratch(HEADS_PER_STEP, tq, tk),
        compiler_params=_cparams("arbitrary", "arbitrary"),
        name="attn_dsa",
    )(q, k, v, bias)


def _sortable_key(score):
    bits = lax.bitcast_convert_type(score + 0.0, jnp.int32)
    return bits ^ ((bits >> 31) & 0x7FFFFFFF)


def _indexer_body(iq_ref, ik_ref, iw_ref, bias_ref, keys_ref, qz_ref, wb_ref, thr_ref, need_ref, *, tq, n_sel):
    i = pl.program_id(0)
    s_len = ik_ref.shape[0]
    n_chunks = s_len // tq
    lane = lax.broadcasted_iota(jnp.int32, (1, LANES), 1)
    iw = iw_ref[...]
    for p in range(IDX_HEADS // 2):
        pair = iq_ref[:, p * LANES:(p + 1) * LANES]
        zero = jnp.zeros_like(pair)
        qz_ref[2 * p] = jnp.where(lane < IDX_DIM, pair, zero)
        qz_ref[2 * p + 1] = jnp.where(lane >= IDX_DIM, pair, zero)
    for hh in range(IDX_HEADS):
        wb_ref[hh] = jnp.broadcast_to(iw[:, hh:hh + 1], (tq, LANES))

    def score_chunk(start):
        ik = ik_ref[pl.ds(start, tq), :]
        sc = jnp.zeros((tq, tq), F32)
        for hh in range(IDX_HEADS):
            rel = jnp.maximum(_dot_nt(qz_ref[hh], ik), 0.0)
            sc = sc + jnp.tile(wb_ref[hh], (1, tq // LANES)) * rel
        return sc

    def full_body(j, carry):
        start = pl.multiple_of(j * tq, tq)
        keys_ref[:, pl.ds(start, tq)] = _sortable_key(score_chunk(start))
        return carry

    lax.fori_loop(0, i, full_body, 0)
    diag = pl.multiple_of(i * tq, tq)
    sc = jnp.where(_causal_keep(tq, tq, 0), score_chunk(diag), -jnp.inf)
    keys_ref[:, pl.ds(diag, tq)] = _sortable_key(sc)

    keys_ref[:, pl.ds(pl.multiple_of((i + 1) * tq, tq), tq)] = jnp.full((tq, tq), KEY_NEG_INF, jnp.int32)
    strips = [pl.ds(r0, BISECT_ROWS) for r0 in range(0, tq, BISECT_ROWS)]
    n_trips = (i + 2) // 2
    lanes_per_trip = 2 * tq // LANES

    def count_keys(rows, trial, strict):
        def count_body(j, cnt):
            start = pl.multiple_of(j * (2 * tq), 2 * tq)
            for g in range(lanes_per_trip):
                kk = keys_ref[rows, pl.ds(start + g * LANES, LANES)]
                cnt = cnt + ((kk > trial) if strict else (kk >= trial)).astype(jnp.int32)
            return cnt

        return lax.fori_loop(0, n_trips, count_body, jnp.zeros((BISECT_ROWS, LANES), jnp.int32))

    def key_bounds(rows):
        def max_body(j, gm):
            start = pl.multiple_of(j * (2 * tq), 2 * tq)
            gm = list(gm)
            for g in range(lanes_per_trip):
                gm[g % 2] = jnp.maximum(gm[g % 2], keys_ref[rows, pl.ds(start + g * LANES, LANES)])
            return tuple(gm)

        init = jnp.full((BISECT_ROWS, LANES), KEY_NEG_INF, jnp.int32)
        g0, g1 = lax.fori_loop(0, n_trips, max_body, (init, init))
        lo = jnp.min(jnp.minimum(g0, g1), axis=1, keepdims=True)
        hi = jnp.max(jnp.maximum(g0, g1), axis=1, keepdims=True)
        lo = jnp.maximum(lo, KEY_NEG_INF + 1) + jnp.zeros((BISECT_ROWS, LANES), jnp.int32)
        return lo, jnp.maximum(hi, lo) + 1

    def open_rows(lo, hi):
        width = hi - lo
        return ((width != 0) & (width != 1)).astype(jnp.int32)

    def halve(bounds):
        mids = [lo + lax.shift_right_logical(hi - lo, 1) for lo, hi in bounds]
        counts = [count_keys(rows, mid, False) for rows, mid in zip(strips, mids)]
        out = []
        for cnt, mid, (lo, hi) in zip(counts, mids, bounds):
            total = jnp.sum(cnt, axis=1, keepdims=True)
            enough = total >= n_sel
            out.append((jnp.where(enough, mid, lo), jnp.where(total == n_sel, mid + 1, jnp.where(enough, hi, mid))))
        return out

    def flat(bounds):
        return tuple(b for pair in bounds for b in pair)

    def pairs(flat_bounds):
        return [(flat_bounds[2 * t], flat_bounds[2 * t + 1]) for t in range(len(strips))]

    bounds = lax.fori_loop(0, BISECT_UNCHECKED, lambda _, fb: flat(halve(pairs(fb))),
                           flat([key_bounds(rows) for rows in strips]))

    def still_open(state):
        it, fb = state[0], state[1:]
        n_open = functools.reduce(jnp.maximum, [open_rows(lo, hi) for lo, hi in pairs(fb)])
        return (it < 32) & (jnp.max(n_open) > 0)

    bounds = lax.while_loop(still_open, lambda state: (state[0] + 1,) + flat(halve(pairs(state[1:]))),
                            (jnp.int32(0),) + bounds)[1:]

    surplus = jnp.zeros((BISECT_ROWS, LANES), jnp.int32)
    for rows, (thr, _) in zip(strips, pairs(bounds)):
        above = jnp.sum(count_keys(rows, thr, True), axis=1, keepdims=True)
        tied = jnp.sum(count_keys(rows, thr, False), axis=1, keepdims=True) - above
        need = n_sel - above
        thr_ref[rows, :] = thr
        need_ref[rows, :] = need.astype(F32) + jnp.zeros((BISECT_ROWS, LANES), F32)
        surplus = jnp.maximum(surplus, tied - need)
    any_surplus = jnp.max(surplus) > 0
    n_grp = tq // LANES

    @pl.when(jnp.logical_not(any_surplus))
    def _write_no_ties():
        def write_body(j, carry):
            start = pl.multiple_of(j * tq, tq)
            thr = thr_ref[...]
            for g in range(n_grp):
                cols = pl.ds(start + g * LANES, LANES)
                bias_ref[:, cols] = jnp.where(keys_ref[:, cols] >= thr, 0.0, NEG).astype(bias_ref.dtype)
            return carry

        lax.fori_loop(0, i + 1, write_body, 0)

    @pl.when(any_surplus)
    def _write_with_ties():
        ca = lax.broadcasted_iota(jnp.int32, (tq, tq), 0)
        cb = lax.broadcasted_iota(jnp.int32, (tq, tq), 1)
        prefix_ones = (ca <= cb).astype(BF16)
        all_ones = jnp.ones((tq, LANES), BF16)

        def write_body(j, seen):
            start = pl.multiple_of(j * tq, tq)
            kk = keys_ref[:, pl.ds(start, tq)]
            thr = jnp.concatenate([thr_ref[...]] * n_grp, axis=1)
            eq = kk == thr
            eq_b = jnp.where(eq, 1.0, 0.0).astype(BF16)
            rank = jnp.dot(eq_b, prefix_ones, preferred_element_type=F32) + jnp.concatenate([seen] * n_grp, axis=1)
            take = (kk > thr) | (eq & (rank <= jnp.concatenate([need_ref[...]] * n_grp, axis=1)))
            bias_ref[:, pl.ds(start, tq)] = jnp.where(take, 0.0, NEG).astype(bias_ref.dtype)
            return seen + jnp.dot(eq_b, all_ones, preferred_element_type=F32)

        lax.fori_loop(0, i + 1, write_body, jnp.zeros((tq, LANES), F32))

    def fill_body(j, carry):
        bias_ref[:, pl.ds(pl.multiple_of(j * tq, tq), tq)] = jnp.full((tq, tq), NEG, bias_ref.dtype)
        return carry

    lax.fori_loop(i + 1, n_chunks, fill_body, 0)


def _dsa_select_bias(iq, ik2, iw, n_sel, tq=TQ_SELECT):
    s = iq.shape[0]
    tq = min(tq, s)
    assert s % tq == 0 and tq % BISECT_ROWS == 0
    body = functools.partial(_indexer_body, tq=tq, n_sel=n_sel)
    return pl.pallas_call(
        body,
        grid=(s // tq,),
        in_specs=[pl.BlockSpec((tq, IDX_HEADS * IDX_DIM), lambda i: (i, 0)),
                  pl.BlockSpec((s, LANES), lambda i: (0, 0)),
                  pl.BlockSpec((tq, LANES), lambda i: (i, 0))],
        out_specs=pl.BlockSpec((tq, s), lambda i: (i, 0)),
        out_shape=jax.ShapeDtypeStruct((s, s), BF16),
        scratch_shapes=[pltpu.VMEM((tq, s + tq), jnp.int32),
                        pltpu.VMEM((IDX_HEADS, tq, LANES), BF16),
                        pltpu.VMEM((IDX_HEADS, tq, LANES), F32),
                        pltpu.VMEM((tq, LANES), jnp.int32),
                        pltpu.VMEM((tq, LANES), F32)],
        compiler_params=_cparams("arbitrary"),
        name="dsa_select",
    )(iq, ik2, iw)


def _fox_cum_body(f_ref, b_ref, o_ref, carry_ref, *, tc):
    @pl.when(pl.program_id(0) == 0)
    def _():
        carry_ref[...] = jnp.zeros(carry_ref.shape, F32)

    z = f_ref[...] + b_ref[...]
    lf = jnp.minimum(z, 0.0) - jnp.log(1.0 + jnp.exp(-jnp.abs(z)))
    row = lax.broadcasted_iota(jnp.int32, (tc, tc), 0)
    col = lax.broadcasted_iota(jnp.int32, (tc, tc), 1)
    tri = (row >= col).astype(BF16)
    hi = lf.astype(BF16)
    r1 = lf - hi.astype(F32)
    mid = r1.astype(BF16)
    lo = (r1 - mid.astype(F32)).astype(BF16)
    cs = (jnp.dot(tri, hi, preferred_element_type=F32) + jnp.dot(tri, mid, preferred_element_type=F32)
          + jnp.dot(tri, lo, preferred_element_type=F32)) + carry_ref[...]
    o_ref[...] = cs
    carry_ref[...] = cs[tc - 1:tc, :]


def _fox_cumsum(f_raw, b_pad, tc=T_CUMSUM):
    s = f_raw.shape[0]
    tc = min(tc, s)
    return pl.pallas_call(
        functools.partial(_fox_cum_body, tc=tc),
        grid=(s // tc,),
        in_specs=[pl.BlockSpec((tc, LANES), lambda m: (m, 0)), pl.BlockSpec((1, LANES), lambda m: (0, 0))],
        out_specs=pl.BlockSpec((tc, LANES), lambda m: (m, 0)),
        out_shape=jax.ShapeDtypeStruct((s, LANES), F32),
        scratch_shapes=[pltpu.VMEM((1, LANES), F32)],
        compiler_params=_cparams("arbitrary"),
        name="fox_cumsum",
    )(f_raw, b_pad)


def _rope_tables(positions, rot_dim, period):
    half = rot_dim // 2
    inv_freq = ROPE_THETA ** (-jnp.arange(0, rot_dim, 2, dtype=F32) / rot_dim)
    ang = positions.astype(F32)[:, None] * inv_freq
    cos, sin = jnp.cos(ang), jnp.sin(ang)
    lp = jnp.arange(LANES) % period
    in_x1 = lp < half
    in_x2 = (lp >= half) & (lp < 2 * half)
    idx = jnp.where(in_x1, lp, jnp.clip(lp - half, 0, half - 1))
    cos_l, sin_l = cos[:, idx], sin[:, idx]
    c = jnp.where(in_x1 | in_x2, cos_l, 1.0)
    s1 = jnp.where(in_x2, sin_l, 0.0)
    s2 = jnp.where(in_x1, -sin_l, 0.0)
    return c, s1, s2


def _pad_cols(w, width):
    return jnp.pad(w, ((0, 0), (0, width - w.shape[1])))


def _mla_mixer(h, tabs, w_in, q_a_g, kv_a_g, w_q_b, w_kv_b, q_g, k_g):
    s = h.shape[0]
    nope = HEAD_DIM
    scale = MLA_HEAD ** -0.5 * LOG2E
    gains = jnp.concatenate([q_a_g, kv_a_g])
    c_norm = _proj(h, w_in, 0, 0, 2 * MLA_LORA, tn=MLA_LORA, gain=gains, gw=MLA_LORA, real=MLA_LORA)
    kpe = _proj(h, _pad_cols(w_in[0][:, 2 * MLA_LORA:], LANES)[None], 0, 0, LANES, tn=LANES, out_dtype=F32,
                epi=_epi_plain)
    wq = w_q_b[0].reshape(MLA_LORA, N_HEADS, MLA_HEAD)
    wq = jnp.concatenate([wq[:, :, MLA_ROPE:], wq[:, :, :MLA_ROPE],
                          jnp.zeros((MLA_LORA, N_HEADS, 2 * LANES - MLA_HEAD), F32)], axis=2)
    wq = wq.reshape(1, MLA_LORA, N_HEADS * 2 * LANES)
    qg = jnp.concatenate([q_g[MLA_ROPE:], q_g[:MLA_ROPE], jnp.zeros((2 * LANES - MLA_HEAD,), F32)])
    q = _proj(c_norm, wq, 0, 0, N_HEADS * 2 * LANES, tn=TN_WIDE, gain=jnp.tile(qg, N_HEADS), tabs=tabs,
              gw=2 * LANES, real=MLA_HEAD, half=MLA_ROPE // 2, rope_groups=lambda g: g % 2 == 1, scale=scale)
    tm, tn = min(TM, s), TN
    aux = [(kpe, (tm, LANES), lambda n, m: (m, 0)),
           (k_g[MLA_ROPE:].reshape(1, nope), (1, nope), lambda n, m: (0, 0)),
           (jnp.pad(k_g[:MLA_ROPE], (0, LANES - MLA_ROPE)).reshape(1, LANES), (1, LANES), lambda n, m: (0, 0))]
    aux += _row_tab_aux(tabs, tm)
    outs = [((s, N_HEADS * 2 * LANES), BF16, (tm, tn), lambda n, m: (m, n)),
            ((s, N_HEADS * HEAD_DIM), BF16, (tm, tn // 2), lambda n, m: (m, n))]
    k, v = _mm(c_norm, [(w_kv_b, 0, 0)], aux, outs, _epi_mla_kv, tm=tm, tn=tn,
               n_tiles=N_HEADS * 2 * LANES // tn, a_blk=1, name="mla_kv")
    return _dense_attention(q, k, v, mode="mla", dq=2 * LANES)


def _dsa_mixer(h, tabs_head, tabs_idx, w_in, q_g, k_g, idx_k_g):
    s = h.shape[0]
    hd = N_HEADS * HEAD_DIM
    ih = IDX_HEADS * IDX_DIM
    half = HEAD_DIM // 8
    q = _proj(h, w_in, 0, 0, hd, tn=TN_WIDE, gain=jnp.tile(q_g, N_HEADS), tabs=tabs_head, half=half,
              scale=HEAD_DIM ** -0.5 * LOG2E)
    k = _proj(h, w_in, 0, hd, hd, tn=TN_WIDE, gain=jnp.tile(k_g, N_HEADS), tabs=tabs_head, half=half)
    v = _proj(h, w_in, 0, 2 * hd, hd, tn=TN_WIDE, epi=_epi_plain)
    iq = _proj(h, w_in, 0, 3 * hd, ih, tn=TN_WIDE, tabs=tabs_idx, half=IDX_DIM // 8)
    w_ik = w_in[0][:, 3 * hd + ih:3 * hd + ih + IDX_DIM]
    ik2 = _proj(h, jnp.concatenate([w_ik, w_ik], axis=1)[None], 0, 0, LANES, tn=LANES,
                gain=jnp.tile(idx_k_g, 2), tabs=tabs_idx, half=IDX_DIM // 8)
    w_iw = _pad_cols(w_in[0][:, 3 * hd + ih + IDX_DIM:], LANES)[None]
    iw = _proj(h, w_iw, 0, 0, LANES, tn=LANES, out_dtype=F32,
               epi=functools.partial(_epi_heads, gw=LANES, real=LANES, half=0, rope_groups=None,
                                     scale=IDX_HEADS ** -0.5 * IDX_DIM ** -0.5, use_gain=False, use_rope=False))
    bias = _dsa_select_bias(iq, ik2, iw, min(IDX_TOPK, s // 4))
    return _dsa_attention(q, k, v, bias)


def _diff_mixer(h, tabs, layer_idx, w_in, q_g, k_g, lam_params, subln_g):
    w = N_HEADS * 2 * DIFF_DIM
    half = DIFF_DIM // 8
    q = _proj(h, w_in, 0, 0, w, tn=TN_WIDE, gain=jnp.tile(q_g, 2 * N_HEADS), tabs=tabs, gw=DIFF_DIM, real=DIFF_DIM,
              half=half, scale=DIFF_DIM ** -0.5 * LOG2E)
    k = _proj(h, w_in, 0, w, w, tn=TN_WIDE, gain=jnp.tile(k_g, 2 * N_HEADS), tabs=tabs, gw=DIFF_DIM, real=DIFF_DIM,
              half=half)
    v = _proj(h, w_in, 0, 2 * w, w, tn=TN_WIDE, epi=_epi_plain)
    lam_init = 0.8 - 0.6 * math.exp(-0.3 * layer_idx)
    return _dense_attention(q, k, v, mode="diff", dq=HEAD_DIM, extra=(lam_params, subln_g.reshape(1, HEAD_DIM)),
                            lam_init=lam_init)


def _fox_mixer(h, w_in, b_f, q_g, k_g):
    s = h.shape[0]
    hd = N_HEADS * HEAD_DIM
    q = _proj(h, w_in, 0, 0, hd, tn=TN_WIDE, gain=jnp.tile(q_g, N_HEADS), scale=HEAD_DIM ** -0.5 * LOG2E)
    k = _proj(h, w_in, 0, hd, hd, tn=TN_WIDE, gain=jnp.tile(k_g, N_HEADS))
    v = _proj(h, w_in, 0, 2 * hd, hd, tn=TN_WIDE, epi=_epi_plain)
    w_f = _pad_cols(w_in[0][:, 3 * hd:3 * hd + N_HEADS], LANES)[None]
    f_raw = _proj(h, w_f, 0, 0, LANES, tn=LANES, out_dtype=F32, epi=_epi_plain)
    gate = _proj(h, w_in[:, :, 3 * hd + N_HEADS:], 0, 0, hd, tn=TN_WIDE, out_dtype=F32, epi=_epi_sigmoid, tm=TM)
    cum = _fox_cumsum(f_raw, jnp.pad(b_f, (0, LANES - N_HEADS)).reshape(1, LANES))
    cumt = cum[:, :N_HEADS].T.reshape(N_HEADS, 1, s)
    return _dense_attention(q, k, v, mode="fox", dq=HEAD_DIM, extra=(cum, cumt, gate))


def _swiglu_ffn(h, x, gate, w_gate_up, w_down, layer):
    s = h.shape[0]
    tm, tn = min(TM_LONG, s), TN
    outs = [((s, FFN_HIDDEN), BF16, (tm, tn), lambda n, m: (m, n))]
    act = _mm(h, [(w_gate_up, layer, 0), (w_gate_up, layer, FFN_HIDDEN // tn)], [], outs, _epi_swiglu,
              tm=tm, tn=tn, n_tiles=FFN_HIDDEN // tn, name="ffn_gate_up")[0]
    return _residual_mm(act, w_down, layer, x, gate, tm=TM_DEEP)


def kernel(x, c, positions, ln_mix_g, ln_ffn_g, ada_w, ada_b, ffn_w_gate_up, ffn_w_down, mla_w_in, mla_q_a_g, mla_kv_a_g, mla_w_q_b, mla_w_kv_b, mla_q_g, mla_k_g, mla_w_out, dsa_w_in, dsa_q_g, dsa_k_g, dsa_idx_k_g, dsa_w_out, diff_w_in, diff_q_g, diff_k_g, diff_lambda_q1, diff_lambda_k1, diff_lambda_q2, diff_lambda_k2, diff_subln_g, diff_w_out, fox_w_in, fox_b_f, fox_q_g, fox_k_g, fox_w_out):
    batch, s, d = x.shape
    assert batch == 1 and d == D_MODEL
    depth = ada_w.shape[0]
    pos = positions[0]
    tabs_head = _rope_tables(pos, HEAD_DIM // 4, LANES)
    tabs_small = _rope_tables(pos, IDX_DIM // 4, IDX_DIM)
    tabs_mla = _rope_tables(pos, MLA_ROPE, LANES)
    mod = _ada_mod(c, ada_w, ada_b)
    xs = x[0]
    for i in range(depth):
        sh1, sc1, g1, sh2, sc2, g2 = [mod[i, t * d:(t + 1) * d] for t in range(6)]
        h = _normmod(xs, ln_mix_g[i], sc1, sh1)
        kind, j = i % 4, i // 4
        if kind == 0:
            o = _mla_mixer(h, tabs_mla, mla_w_in[j:j + 1], mla_q_a_g[j], mla_kv_a_g[j], mla_w_q_b[j:j + 1],
                           mla_w_kv_b[j:j + 1], mla_q_g[j], mla_k_g[j])
            w_out = mla_w_out
        elif kind == 1:
            o = _dsa_mixer(h, tabs_head, tabs_small, dsa_w_in[j:j + 1], dsa_q_g[j], dsa_k_g[j], dsa_idx_k_g[j])
            w_out = dsa_w_out
        elif kind == 2:
            lam_params = jnp.stack([diff_lambda_q1[j], diff_lambda_k1[j], diff_lambda_q2[j], diff_lambda_k2[j]])
            o = _diff_mixer(h, tabs_small, i, diff_w_in[j:j + 1], diff_q_g[j], diff_k_g[j], lam_params,
                            diff_subln_g[j])
            w_out = diff_w_out
        else:
            o = _fox_mixer(h, fox_w_in[j:j + 1], fox_b_f[j], fox_q_g[j], fox_k_g[j])
            w_out = fox_w_out
        xs = _residual_mm(o, w_out, j, xs, g1, tn=TN_WIDE)
        h = _normmod(xs, ln_ffn_g[i], sc2, sh2)
        xs = _swiglu_ffn(h, xs, g2, ffn_w_gate_up, ffn_w_down, i)
    return xs[None]
```

```python
import functools
import math

import jax
import jax.numpy as jnp
from jax import lax
from jax.experimental import pallas as pl
from jax.experimental.pallas import tpu as pltpu

F32 = jnp.float32
BF16 = jnp.bfloat16

D_MODEL = 2048
N_HEADS = 16
HEAD_DIM = 128
ROPE_THETA = 500000.0
RMS_EPS = 1e-6
FFN_HIDDEN = 5632
MLA_LORA = 512
MLA_ROPE = 64
MLA_HEAD = 192
IDX_HEADS = 16
IDX_DIM = 64
IDX_TOPK = 256
DIFF_DIM = 64

LANES = 128
LOG2E = 1.4426950408889634
NEG = -1e30
VMEM_LIMIT_BYTES = 56 * 1024 * 1024
TM = 1024
TM_LONG = 2048
TM_DEEP = 512
SUB_ROWS = 256
TN = 512
TN_WIDE = 1024
TN_ADA = 1536
TQ_ATTN = 512
TQ_SELECT = 256
T_CUMSUM = 512
HEADS_PER_STEP = 2
BISECT_ROWS = 128
BISECT_UNCHECKED = 8
INT_MIN = -2 ** 31
KEY_NEG_INF = (0xFF800000 ^ 0x7FFFFFFF) - 2 ** 32


def _cparams(*sem):
    return pltpu.CompilerParams(dimension_semantics=sem, vmem_limit_bytes=VMEM_LIMIT_BYTES)


def _dot_nt(a, b):
    return lax.dot_general(a, b, (((1,), (1,)), ((), ())), preferred_element_type=F32)


def _sigmoid(z):
    return 1.0 / (1.0 + jnp.exp(-z))


def _mm_body(*refs, n_w, n_aux, n_out, epi, sub, lane_shift):
    a_ref = refs[0]
    w_refs = refs[1:1 + n_w]
    aux_refs = refs[1 + n_w:1 + n_w + n_aux]
    out_refs = refs[1 + n_w + n_aux:1 + n_w + n_aux + n_out]
    wb_refs = refs[1 + n_w + n_aux + n_out:]
    tm = a_ref.shape[0]

    @pl.when(pl.program_id(1) == 0)
    def _cast_weights():
        if lane_shift:
            w0, w1 = w_refs
            wb_refs[0][...] = jnp.concatenate([w0[:, lane_shift:], w1[:, :lane_shift]], axis=1).astype(BF16)
            return
        for w_ref, wb_ref in zip(w_refs, wb_refs):
            wb_ref[...] = w_ref[...].astype(BF16)

    for r in range(tm // sub):
        rows = pl.ds(r * sub, sub)
        accs = [jnp.dot(a_ref[rows, :], wb_ref[...], preferred_element_type=F32) for wb_ref in wb_refs]
        epi(accs, [ref.at[rows] if ref.shape[0] == tm else ref for ref in aux_refs],
            [ref.at[rows] for ref in out_refs])


def _mm(a, ws, aux, outs, epi, *, tm, tn, n_tiles, a_blk=0, name="mm", sub=SUB_ROWS, lane_shift=0):
    s = a.shape[0]
    k = ws[0][0].shape[1]
    tm = min(tm, s)
    in_specs = [pl.BlockSpec((tm, k), lambda n, m: (m, a_blk))]
    operands = [a]
    for w, layer, off in ws:
        assert w.shape[1] == k
        in_specs.append(pl.BlockSpec((None, k, tn), lambda n, m, layer=layer, off=off: (layer, 0, n + off)))
        operands.append(w)
    for arr, block, imap in aux:
        in_specs.append(pl.BlockSpec(block, imap))
        operands.append(arr)
    out_shape = [jax.ShapeDtypeStruct(shape, dtype) for shape, dtype, _, _ in outs]
    out_specs = [pl.BlockSpec(block, imap) for _, _, block, imap in outs]
    body = functools.partial(_mm_body, n_w=len(ws), n_aux=len(aux), n_out=len(outs), epi=epi, sub=min(sub, tm),
                             lane_shift=lane_shift)
    res = pl.pallas_call(
        body,
        grid=(n_tiles, s // tm),
        in_specs=in_specs,
        out_specs=out_specs,
        out_shape=out_shape,
        scratch_shapes=[pltpu.VMEM((k, tn), BF16) for _ in (ws[:1] if lane_shift else ws)],
        compiler_params=_cparams("arbitrary", "arbitrary"),
        name=name,
    )(*operands)
    return res


def _rope_group(y, tabs, half):
    c, s1, s2 = tabs
    return y * c + pltpu.roll(y, half, 1) * s1 + pltpu.roll(y, LANES - half, 1) * s2


def _group_inv_rms(y, gw, real):
    tn = y.shape[1]
    sq = y * y
    n_groups = tn // LANES
    if gw == DIFF_DIM:
        lane = lax.broadcasted_iota(jnp.int32, (1, LANES), 1)
        lo = lane < DIFF_DIM
        out = []
        for g in range(n_groups):
            sg = sq[:, g * LANES:(g + 1) * LANES]
            s_lo = jnp.sum(jnp.where(lo, sg, 0.0), axis=1, keepdims=True)
            s_hi = jnp.sum(jnp.where(lo, 0.0, sg), axis=1, keepdims=True)
            out.append(lax.rsqrt(jnp.where(lo, s_lo, s_hi) * (1.0 / real) + RMS_EPS))
        return out
    per = gw // LANES
    out = []
    for h in range(n_groups // per):
        folded = sq[:, h * gw:h * gw + LANES]
        for t in range(1, per):
            folded = folded + sq[:, h * gw + t * LANES:h * gw + (t + 1) * LANES]
        r = lax.rsqrt(jnp.sum(folded, axis=1, keepdims=True) * (1.0 / real) + RMS_EPS)
        out.extend([r] * per)
    return out


def _epi_heads(accs, aux_refs, out_refs, *, gw, real, half, rope_groups, scale, use_gain, use_rope):
    y = accs[0]
    tn = y.shape[1]
    idx = 0
    gain = None
    if use_gain:
        gain = aux_refs[idx][...]
        idx += 1
        inv = _group_inv_rms(y, gw, real)
    if use_rope:
        tabs = tuple(aux_refs[idx + t][...] for t in range(3))
    for g in range(tn // LANES):
        yg = y[:, g * LANES:(g + 1) * LANES]
        if use_gain:
            yg = yg * inv[g] * gain[:, g * LANES:(g + 1) * LANES]
        if use_rope and rope_groups(g):
            yg = _rope_group(yg, tabs, half)
        if scale != 1.0:
            yg = yg * scale
        out_refs[0][:, g * LANES:(g + 1) * LANES] = yg.astype(out_refs[0].dtype)


def _epi_plain(accs, aux_refs, out_refs):
    out_refs[0][...] = accs[0].astype(out_refs[0].dtype)


def _epi_sigmoid(accs, aux_refs, out_refs):
    out_refs[0][...] = _sigmoid(accs[0]).astype(out_refs[0].dtype)


def _epi_residual(accs, aux_refs, out_refs):
    x_ref, g_ref = aux_refs
    out_refs[0][...] = x_ref[...] + g_ref[...] * accs[0]


def _epi_swiglu(accs, aux_refs, out_refs):
    gate, up = accs
    out_refs[0][...] = (gate * _sigmoid(gate) * up).astype(out_refs[0].dtype)


def _epi_mla_kv(accs, aux_refs, out_refs):
    kpe_ref, g_nope_ref, g_pe_ref, c_ref, s1_ref, s2_ref = aux_refs
    k_ref, v_ref = out_refs
    y = accs[0]
    kpe = kpe_ref[...]
    ss_pe = jnp.sum(kpe * kpe, axis=1, keepdims=True)
    tabs = (c_ref[...], s1_ref[...], s2_ref[...])
    g_nope = g_nope_ref[...]
    pe_roped = _rope_group(kpe * g_pe_ref[...], tabs, MLA_ROPE // 2)
    for h in range(y.shape[1] // (2 * LANES)):
        kn = y[:, 2 * h * LANES:(2 * h + 1) * LANES]
        v = y[:, (2 * h + 1) * LANES:(2 * h + 2) * LANES]
        r = lax.rsqrt((jnp.sum(kn * kn, axis=1, keepdims=True) + ss_pe) * (1.0 / MLA_HEAD) + RMS_EPS)
        k_ref[:, 2 * h * LANES:(2 * h + 1) * LANES] = (kn * r * g_nope).astype(k_ref.dtype)
        k_ref[:, (2 * h + 1) * LANES:(2 * h + 2) * LANES] = (pe_roped * r).astype(k_ref.dtype)
        v_ref[:, h * LANES:(h + 1) * LANES] = v.astype(v_ref.dtype)


def _epi_index_key_and_weights(accs, aux_refs, out_refs):
    g_ref, c_ref, s1_ref, s2_ref = aux_refs
    ik2_ref, iw_ref = out_refs
    y = accs[0]
    lane = lax.broadcasted_iota(jnp.int32, (1, LANES), 1)
    yk = jnp.where(lane < IDX_DIM, y, 0.0)
    inv = lax.rsqrt(jnp.sum(yk * yk, axis=1, keepdims=True) * (1.0 / IDX_DIM) + RMS_EPS)
    key = _rope_group(yk * inv * g_ref[...], (c_ref[...], s1_ref[...], s2_ref[...]), IDX_DIM // 8)
    ik2_ref[...] = (key + pltpu.roll(key, IDX_DIM, 1)).astype(ik2_ref.dtype)
    weights = pltpu.roll(y, LANES - IDX_DIM, 1) * (IDX_HEADS ** -0.5 * IDX_DIM ** -0.5)
    iw_ref[...] = jnp.where(lane < IDX_HEADS, weights, 0.0)


def _epi_keep_lanes(accs, aux_refs, out_refs, *, n_keep):
    lane = lax.broadcasted_iota(jnp.int32, (1, LANES), 1)
    out_refs[0][...] = jnp.where(lane < n_keep, accs[0], 0.0).astype(out_refs[0].dtype)


def _row_tab_aux(tabs, tm):
    return [(t, (tm, LANES), lambda n, m: (m, 0)) for t in tabs]


def _proj(a, w, layer, col_off, width, *, tn, out_dtype=BF16, gain=None, tabs=None, gw=LANES, real=LANES,
          half=0, rope_groups=lambda g: True, scale=1.0, epi=None, tm=TM_LONG, a_blk=0):
    s = a.shape[0]
    tm = min(tm, s)
    assert col_off % tn == 0 and width % tn == 0
    aux = []
    if gain is not None:
        aux.append((gain.reshape(1, width).astype(F32), (1, tn), lambda n, m: (0, n)))
    if tabs is not None:
        aux.extend(_row_tab_aux(tabs, tm))
    if epi is None:
        epi = functools.partial(_epi_heads, gw=gw, real=real, half=half, rope_groups=rope_groups, scale=scale,
                                use_gain=gain is not None, use_rope=tabs is not None)
    outs = [((s, width), out_dtype, (tm, tn), lambda n, m: (m, n))]
    return _mm(a, [(w, layer, col_off // tn)], aux, outs, epi, tm=tm, tn=tn, n_tiles=width // tn, a_blk=a_blk,
               name="proj")[0]


def _residual_mm(a, w, layer, x, gate, *, tn=TN, tm=TM):
    s, d = x.shape
    tm = min(tm, s)
    aux = [(x, (tm, tn), lambda n, m: (m, n)), (gate.reshape(1, d), (1, tn), lambda n, m: (0, n))]
    outs = [((s, d), F32, (tm, tn), lambda n, m: (m, n))]
    return _mm(a, [(w, layer, 0)], aux, outs, _epi_residual, tm=tm, tn=tn, n_tiles=d // tn, name="residual_mm")[0]


def _ada_body(c_ref, w_ref, b_ref, o_ref):
    c = c_ref[...]
    cond = c * _sigmoid(c)
    o_ref[...] = jnp.dot(cond.astype(BF16), w_ref[...].astype(BF16), preferred_element_type=F32) + b_ref[...]


def _ada_mod(c, ada_w, ada_b):
    depth, d, n = ada_w.shape
    tn = TN_ADA
    c8 = jnp.broadcast_to(c, (8, d))
    out = pl.pallas_call(
        _ada_body,
        grid=(depth, n // tn),
        in_specs=[pl.BlockSpec((8, d), lambda l, j: (0, 0)),
                  pl.BlockSpec((None, d, tn), lambda l, j: (l, 0, j)),
                  pl.BlockSpec((None, 1, tn), lambda l, j: (l, 0, j))],
        out_specs=pl.BlockSpec((None, 8, tn), lambda l, j: (l, 0, j)),
        out_shape=jax.ShapeDtypeStruct((depth, 8, n), F32),
        compiler_params=_cparams("arbitrary", "arbitrary"),
        name="ada_mod",
    )(c8, ada_w, ada_b.reshape(depth, 1, n))
    return out[:, 0, :]


def _normmod_body(x_ref, g_ref, sc_ref, sh_ref, o_ref):
    x = x_ref[...]
    y = x * lax.rsqrt(jnp.mean(x * x, axis=1, keepdims=True) + RMS_EPS) * g_ref[...]
    o_ref[...] = (y * (1.0 + sc_ref[...]) + sh_ref[...]).astype(o_ref.dtype)


def _normmod(x, g, sc, sh, tm=TM):
    s, d = x.shape
    tm = min(tm, s)
    vec = pl.BlockSpec((1, d), lambda m: (0, 0))
    return pl.pallas_call(
        _normmod_body,
        grid=(s // tm,),
        in_specs=[pl.BlockSpec((tm, d), lambda m: (m, 0)), vec, vec, vec],
        out_specs=pl.BlockSpec((tm, d), lambda m: (m, 0)),
        out_shape=jax.ShapeDtypeStruct((s, d), BF16),
        compiler_params=_cparams("arbitrary"),
        name="normmod",
    )(x, g.reshape(1, d), sc.reshape(1, d), sh.reshape(1, d))


def _softmax_scratch(n_maps, tq, tk):
    return [pltpu.VMEM((n_maps, tq, 2 * HEAD_DIM), F32), pltpu.VMEM((n_maps, tq, LANES), F32),
            pltpu.VMEM((n_maps, 2, tq, tk), F32), pltpu.VMEM((n_maps, 2, 2, tq, LANES), F32)]


def _softmax_pipeline(qs, kv_cols, k_ref, v_ref, scratch, *, tq, tk, first_chunk, first_keep, n_rest, bias_fn):
    acc_ref, m_ref, s_ref, st_ref = scratch
    ones = jnp.ones((tk, LANES), BF16)
    n_grp = tk // LANES
    maps = range(len(qs))

    def chunk_start(c):
        return pl.multiple_of(c * tk, tk)

    def stage1(c, slot, keep):
        start = chunk_start(c)
        for t in maps:
            s = _dot_nt(qs[t], k_ref[pl.ds(start, tk), kv_cols[t][0]])
            if bias_fn is not None:
                s = s + bias_fn(t, start)
            if keep is not None:
                s = jnp.where(keep, s, NEG)
            m_run = m_ref[t]
            m_new = jnp.maximum(m_run, jnp.max(s, axis=1, keepdims=True))
            s_ref[t, slot] = s
            st_ref[t, slot, 0] = m_new
            st_ref[t, slot, 1] = jnp.exp2(m_run - m_new)
            m_ref[t] = m_new

    def stage2(c, slot):
        start = chunk_start(c)
        for t in maps:
            v1 = jnp.concatenate([v_ref[pl.ds(start, tk), kv_cols[t][1]], ones], axis=1)
            m_new = st_ref[t, slot, 0]
            alpha = st_ref[t, slot, 1]
            p = jnp.concatenate([jnp.exp2(s_ref[t, slot, :, g * LANES:(g + 1) * LANES] - m_new)
                                 for g in range(n_grp)], axis=1)
            pv = jnp.dot(p.astype(BF16), v1, preferred_element_type=F32)
            acc_ref[t] = jnp.concatenate([alpha, alpha], axis=1) * acc_ref[t] + pv

    acc_ref[...] = jnp.zeros(acc_ref.shape, F32)
    m_ref[...] = jnp.full(m_ref.shape, NEG, F32)
    stage1(first_chunk, 0, first_keep)

    def two_chunks(c0):
        stage1(c0, 1, None)
        stage2(jnp.where(c0 == 0, first_chunk, c0 - 1), 0)
        stage1(c0 + 1, 0, None)
        stage2(c0, 1)

    def body(t4, carry):
        two_chunks(4 * t4)
        two_chunks(4 * t4 + 2)
        return carry

    n_quads = n_rest // 4
    lax.fori_loop(0, n_quads, body, 0)
    n_pairs = n_rest // 2

    @pl.when(n_pairs > 2 * n_quads)
    def _pair_tail():
        two_chunks(4 * n_quads)

    pending = jnp.where(n_pairs == 0, first_chunk, 2 * n_pairs - 1)

    @pl.when(n_rest % 2 == 1)
    def _odd_tail():
        stage1(n_rest - 1, 1, None)
        stage2(pending, 0)
        stage2(n_rest - 1, 1)

    @pl.when(n_rest % 2 == 0)
    def _even_tail():
        stage2(pending, 0)


def _causal_keep(tq, tk, col_shift):
    row = lax.broadcasted_iota(jnp.int32, (tq, 1), 0)
    col = lax.broadcasted_iota(jnp.int32, (1, tk), 1)
    return row >= col + col_shift


def _dense_attn_body(*refs, mode, tq, tk, lam_init):
    if mode == "mla":
        q_ref, k_ref, v_ref, o_ref = refs[:4]
        scratch = refs[4:]
    elif mode == "fox":
        q_ref, k_ref, v_ref, cum_ref, cumt_ref, gate_ref, o_ref = refs[:7]
        scratch = refs[7:]
    else:
        q_ref, k_ref, v_ref, lam_ref, subg_ref, o_ref = refs[:6]
        scratch = refs[6:]
    hg = pl.program_id(0)
    i = pl.program_id(1)
    acc_ref = scratch[0]
    dq = q_ref.shape[1] // HEADS_PER_STEP
    heads = range(HEADS_PER_STEP)
    lane = lax.broadcasted_iota(jnp.int32, (1, LANES), 1)

    def head_cols(hh, width=HEAD_DIM):
        return slice(hh * width, (hh + 1) * width)

    qs, kv_cols = [], []
    for hh in heads:
        q = q_ref[:, head_cols(hh, dq)]
        if mode == "diff":
            zero = jnp.zeros_like(q)
            qs += [jnp.where(lane < DIFF_DIM, q, zero), jnp.where(lane >= DIFF_DIM, q, zero)]
            kv_cols += [(head_cols(hh, dq), head_cols(hh))] * 2
        else:
            qs.append(q)
            kv_cols.append((head_cols(hh, dq), head_cols(hh)))
    if mode == "fox":
        cum = cum_ref[...]
        cqs = [jnp.sum(jnp.where(lane == hg * HEADS_PER_STEP + hh, cum, 0.0), axis=1, keepdims=True) * LOG2E
               for hh in heads]

        def bias_fn(t, start):
            return cqs[t] - cumt_ref[t, :, pl.ds(start, tk)] * LOG2E
    else:
        bias_fn = None

    _softmax_pipeline(qs, kv_cols, k_ref, v_ref, scratch, tq=tq, tk=tk, first_chunk=i,
                      first_keep=_causal_keep(tq, tk, 0), n_rest=i, bias_fn=bias_fn)

    def normalised(t):
        acc = acc_ref[t]
        return acc[:, :HEAD_DIM] / acc[:, HEAD_DIM:]

    for hh in heads:
        if mode == "diff":
            lp = lam_ref[...]
            lam = (jnp.exp(jnp.sum(lp[0:1] * lp[1:2], axis=1, keepdims=True))
                   - jnp.exp(jnp.sum(lp[2:3] * lp[3:4], axis=1, keepdims=True)) + lam_init)
            o = normalised(2 * hh) - lam * normalised(2 * hh + 1)
            o = o * lax.rsqrt(jnp.mean(o * o, axis=1, keepdims=True) + RMS_EPS) * subg_ref[...]
            o = o * (1.0 - lam_init)
        else:
            o = normalised(hh)
            if mode == "fox":
                o = o * gate_ref[:, head_cols(hh)]
        o_ref[:, head_cols(hh)] = o.astype(o_ref.dtype)


def _dense_attention(q, k, v, *, mode, dq, extra=(), lam_init=0.0, tq=TQ_ATTN):
    s = q.shape[0]
    tq = min(tq, s)
    tk = tq
    assert s % tq == 0 and N_HEADS % HEADS_PER_STEP == 0
    g = HEADS_PER_STEP
    in_specs = [pl.BlockSpec((tq, g * dq), lambda h, i: (i, h)),
                pl.BlockSpec((s, g * dq), lambda h, i: (0, h)),
                pl.BlockSpec((s, g * HEAD_DIM), lambda h, i: (0, h))]
    operands = [q, k, v]
    if mode == "fox":
        cum, cumt, gate = extra
        in_specs += [pl.BlockSpec((tq, LANES), lambda h, i: (i, 0)),
                     pl.BlockSpec((g, 1, s), lambda h, i: (h, 0, 0)),
                     pl.BlockSpec((tq, g * HEAD_DIM), lambda h, i: (i, h))]
        operands += [cum, cumt, gate]
    elif mode == "diff":
        lam_params, subln_g = extra
        in_specs += [pl.BlockSpec((4, DIFF_DIM), lambda h, i: (0, 0)),
                     pl.BlockSpec((1, HEAD_DIM), lambda h, i: (0, 0))]
        operands += [lam_params, subln_g]
    n_maps = g * (2 if mode == "diff" else 1)
    body = functools.partial(_dense_attn_body, mode=mode, tq=tq, tk=tk, lam_init=lam_init)
    return pl.pallas_call(
        body,
        grid=(N_HEADS // g, s // tq),
        in_specs=in_specs,
        out_specs=pl.BlockSpec((tq, g * HEAD_DIM), lambda h, i: (i, h)),
        out_shape=jax.ShapeDtypeStruct((s, N_HEADS * HEAD_DIM), BF16),
        scratch_shapes=_softmax_scratch(n_maps, tq, tk),
        compiler_params=_cparams("arbitrary", "arbitrary"),
        name="attn_" + mode,
    )(*operands)


def _dsa_attn_body(q_ref, k_ref, v_ref, bias_ref, o_ref, *scratch, tq, tk):
    i = pl.program_id(0)

    def bias_fn(t, start):
        return bias_ref[:, pl.ds(start, tk)].astype(F32)

    cols = [slice(hh * HEAD_DIM, (hh + 1) * HEAD_DIM) for hh in range(HEADS_PER_STEP)]
    _softmax_pipeline([q_ref[:, c] for c in cols], [(c, c) for c in cols], k_ref, v_ref, scratch, tq=tq, tk=tk,
                      first_chunk=i, first_keep=None, n_rest=i, bias_fn=bias_fn)
    for t, c in enumerate(cols):
        acc = scratch[0][t]
        o_ref[:, c] = (acc[:, :HEAD_DIM] / acc[:, HEAD_DIM:]).astype(o_ref.dtype)


def _dsa_attention(q, k, v, bias, *, tq=TQ_ATTN):
    s = q.shape[0]
    tq = min(tq, s)
    tk = tq
    assert s % tq == 0 and N_HEADS % HEADS_PER_STEP == 0
    body = functools.partial(_dsa_attn_body, tq=tq, tk=tk)
    return pl.pallas_call(
        body,
        grid=(s // tq, N_HEADS // HEADS_PER_STEP),
        in_specs=[pl.BlockSpec((tq, HEADS_PER_STEP * HEAD_DIM), lambda i, h: (i, h)),
                  pl.BlockSpec((s, HEADS_PER_STEP * HEAD_DIM), lambda i, h: (0, h)),
                  pl.BlockSpec((s, HEADS_PER_STEP * HEAD_DIM), lambda i, h: (0, h)),
                  pl.BlockSpec((tq, s), lambda i, h: (i, 0))],
        out_specs=pl.BlockSpec((tq, HEADS_PER_STEP * HEAD_DIM), lambda i, h: (i, h)),
        out_shape=jax.ShapeDtypeStruct((s, N_HEADS * HEAD_DIM), BF16),
        scratch_shapes=_softmax_scratch(HEADS_PER_STEP, tq, tk),
        compiler_params=_cparams("arbitrary", "arbitrary"),
        name="attn_dsa",
    )(q, k, v, bias)


def _sortable_key(score):
    bits = lax.bitcast_convert_type(score + 0.0, jnp.int32)
    return bits ^ ((bits >> 31) & 0x7FFFFFFF)


def _indexer_body(iq_ref, ik_ref, iw_ref, bias_ref, keys_ref, qz_ref, wb_ref, thr_ref, need_ref, *, tq, n_sel):
    i = pl.program_id(0)
    s_len = ik_ref.shape[0]
    n_chunks = s_len // tq
    lane = lax.broadcasted_iota(jnp.int32, (1, LANES), 1)
    iw = iw_ref[...]
    for p in range(IDX_HEADS // 2):
        pair = iq_ref[:, p * LANES:(p + 1) * LANES]
        zero = jnp.zeros_like(pair)
        qz_ref[2 * p] = jnp.where(lane < IDX_DIM, pair, zero)
        qz_ref[2 * p + 1] = jnp.where(lane >= IDX_DIM, pair, zero)
    for hh in range(IDX_HEADS):
        wb_ref[hh] = jnp.broadcast_to(iw[:, hh:hh + 1], (tq, LANES))

    def score_chunk(start):
        ik = ik_ref[pl.ds(start, tq), :]
        sc = jnp.zeros((tq, tq), F32)
        for hh in range(IDX_HEADS):
            rel = jnp.maximum(_dot_nt(qz_ref[hh], ik), 0.0)
            sc = sc + jnp.tile(wb_ref[hh], (1, tq // LANES)) * rel
        return sc

    def full_body(j, carry):
        start = pl.multiple_of(j * tq, tq)
        keys_ref[:, pl.ds(start, tq)] = _sortable_key(score_chunk(start))
        return carry

    lax.fori_loop(0, i, full_body, 0)
    diag = pl.multiple_of(i * tq, tq)
    sc = jnp.where(_causal_keep(tq, tq, 0), score_chunk(diag), -jnp.inf)
    keys_ref[:, pl.ds(diag, tq)] = _sortable_key(sc)

    keys_ref[:, pl.ds(pl.multiple_of((i + 1) * tq, tq), tq)] = jnp.full((tq, tq), KEY_NEG_INF, jnp.int32)
    strips = [pl.ds(r0, BISECT_ROWS) for r0 in range(0, tq, BISECT_ROWS)]
    n_trips = (i + 2) // 2
    lanes_per_trip = 2 * tq // LANES

    def count_keys(rows, trial, strict):
        def count_body(j, cnt):
            start = pl.multiple_of(j * (2 * tq), 2 * tq)
            for g in range(lanes_per_trip):
                kk = keys_ref[rows, pl.ds(start + g * LANES, LANES)]
                cnt = cnt + ((kk > trial) if strict else (kk >= trial)).astype(jnp.int32)
            return cnt

        return lax.fori_loop(0, n_trips, count_body, jnp.zeros((BISECT_ROWS, LANES), jnp.int32))

    def key_bounds(rows):
        def max_body(j, gm):
            start = pl.multiple_of(j * (2 * tq), 2 * tq)
            gm = list(gm)
            for g in range(lanes_per_trip):
                gm[g % 2] = jnp.maximum(gm[g % 2], keys_ref[rows, pl.ds(start + g * LANES, LANES)])
            return tuple(gm)

        init = jnp.full((BISECT_ROWS, LANES), KEY_NEG_INF, jnp.int32)
        g0, g1 = lax.fori_loop(0, n_trips, max_body, (init, init))
        lo = jnp.min(jnp.minimum(g0, g1), axis=1, keepdims=True)
        hi = jnp.max(jnp.maximum(g0, g1), axis=1, keepdims=True)
        lo = jnp.maximum(lo, KEY_NEG_INF + 1) + jnp.zeros((BISECT_ROWS, LANES), jnp.int32)
        return lo, jnp.maximum(hi, lo) + 1

    def open_rows(lo, hi):
        width = hi - lo
        return ((width != 0) & (width != 1)).astype(jnp.int32)

    def halve(bounds):
        mids = [lo + lax.shift_right_logical(hi - lo, 1) for lo, hi in bounds]
        counts = [count_keys(rows, mid, False) for rows, mid in zip(strips, mids)]
        out = []
        for cnt, mid, (lo, hi) in zip(counts, mids, bounds):
            total = jnp.sum(cnt, axis=1, keepdims=True)
            enough = total >= n_sel
            out.append((jnp.where(enough, mid, lo), jnp.where(total == n_sel, mid + 1, jnp.where(enough, hi, mid))))
        return out

    def flat(bounds):
        return tuple(b for pair in bounds for b in pair)

    def pairs(flat_bounds):
        return [(flat_bounds[2 * t], flat_bounds[2 * t + 1]) for t in range(len(strips))]

    bounds = lax.fori_loop(0, BISECT_UNCHECKED, lambda _, fb: flat(halve(pairs(fb))),
                           flat([key_bounds(rows) for rows in strips]))

    def still_open(state):
        it, fb = state[0], state[1:]
        n_open = functools.reduce(jnp.maximum, [open_rows(lo, hi) for lo, hi in pairs(fb)])
        return (it < 32) & (jnp.max(n_open) > 0)

    bounds = lax.while_loop(still_open, lambda state: (state[0] + 1,) + flat(halve(pairs(state[1:]))),
                            (jnp.int32(0),) + bounds)[1:]

    surplus = jnp.zeros((BISECT_ROWS, LANES), jnp.int32)
    for rows, (thr, _) in zip(strips, pairs(bounds)):
        above = jnp.sum(count_keys(rows, thr, True), axis=1, keepdims=True)
        tied = jnp.sum(count_keys(rows, thr, False), axis=1, keepdims=True) - above
        need = n_sel - above
        thr_ref[rows, :] = thr
        need_ref[rows, :] = need.astype(F32) + jnp.zeros((BISECT_ROWS, LANES), F32)
        surplus = jnp.maximum(surplus, tied - need)
    any_surplus = jnp.max(surplus) > 0
    n_grp = tq // LANES

    @pl.when(jnp.logical_not(any_surplus))
    def _write_no_ties():
        def write_body(j, carry):
            start = pl.multiple_of(j * tq, tq)
            thr = thr_ref[...]
            for g in range(n_grp):
                cols = pl.ds(start + g * LANES, LANES)
                bias_ref[:, cols] = jnp.where(keys_ref[:, cols] >= thr, 0.0, NEG).astype(bias_ref.dtype)
            return carry

        lax.fori_loop(0, i + 1, write_body, 0)

    @pl.when(any_surplus)
    def _write_with_ties():
        ca = lax.broadcasted_iota(jnp.int32, (tq, tq), 0)
        cb = lax.broadcasted_iota(jnp.int32, (tq, tq), 1)
        prefix_ones = (ca <= cb).astype(BF16)
        all_ones = jnp.ones((tq, LANES), BF16)

        def write_body(j, seen):
            start = pl.multiple_of(j * tq, tq)
            kk = keys_ref[:, pl.ds(start, tq)]
            thr = jnp.concatenate([thr_ref[...]] * n_grp, axis=1)
            eq = kk == thr
            eq_b = jnp.where(eq, 1.0, 0.0).astype(BF16)
            rank = jnp.dot(eq_b, prefix_ones, preferred_element_type=F32) + jnp.concatenate([seen] * n_grp, axis=1)
            take = (kk > thr) | (eq & (rank <= jnp.concatenate([need_ref[...]] * n_grp, axis=1)))
            bias_ref[:, pl.ds(start, tq)] = jnp.where(take, 0.0, NEG).astype(bias_ref.dtype)
            return seen + jnp.dot(eq_b, all_ones, preferred_element_type=F32)

        lax.fori_loop(0, i + 1, write_body, jnp.zeros((tq, LANES), F32))

    def fill_body(j, carry):
        bias_ref[:, pl.ds(pl.multiple_of(j * tq, tq), tq)] = jnp.full((tq, tq), NEG, bias_ref.dtype)
        return carry

    lax.fori_loop(i + 1, n_chunks, fill_body, 0)


def _dsa_select_bias(iq, ik2, iw, n_sel, tq=TQ_SELECT):
    s = iq.shape[0]
    tq = min(tq, s)
    assert s % tq == 0 and tq % BISECT_ROWS == 0
    body = functools.partial(_indexer_body, tq=tq, n_sel=n_sel)
    return pl.pallas_call(
        body,
        grid=(s // tq,),
        in_specs=[pl.BlockSpec((tq, IDX_HEADS * IDX_DIM), lambda i: (i, 0)),
                  pl.BlockSpec((s, LANES), lambda i: (0, 0)),
                  pl.BlockSpec((tq, LANES), lambda i: (i, 0))],
        out_specs=pl.BlockSpec((tq, s), lambda i: (i, 0)),
        out_shape=jax.ShapeDtypeStruct((s, s), BF16),
        scratch_shapes=[pltpu.VMEM((tq, s + tq), jnp.int32),
                        pltpu.VMEM((IDX_HEADS, tq, LANES), BF16),
                        pltpu.VMEM((IDX_HEADS, tq, LANES), F32),
                        pltpu.VMEM((tq, LANES), jnp.int32),
                        pltpu.VMEM((tq, LANES), F32)],
        compiler_params=_cparams("arbitrary"),
        name="dsa_select",
    )(iq, ik2, iw)


def _fox_cum_body(f_ref, b_ref, o_ref, carry_ref, *, tc):
    @pl.when(pl.program_id(0) == 0)
    def _():
        carry_ref[...] = jnp.zeros(carry_ref.shape, F32)

    z = f_ref[...] + b_ref[...]
    lf = jnp.minimum(z, 0.0) - jnp.log(1.0 + jnp.exp(-jnp.abs(z)))
    row = lax.broadcasted_iota(jnp.int32, (tc, tc), 0)
    col = lax.broadcasted_iota(jnp.int32, (tc, tc), 1)
    tri = (row >= col).astype(BF16)
    hi = lf.astype(BF16)
    r1 = lf - hi.astype(F32)
    mid = r1.astype(BF16)
    lo = (r1 - mid.astype(F32)).astype(BF16)
    cs = (jnp.dot(tri, hi, preferred_element_type=F32) + jnp.dot(tri, mid, preferred_element_type=F32)
          + jnp.dot(tri, lo, preferred_element_type=F32)) + carry_ref[...]
    o_ref[...] = cs
    carry_ref[...] = cs[tc - 1:tc, :]


def _fox_cumsum(f_raw, b_pad, tc=T_CUMSUM):
    s = f_raw.shape[0]
    tc = min(tc, s)
    return pl.pallas_call(
        functools.partial(_fox_cum_body, tc=tc),
        grid=(s // tc,),
        in_specs=[pl.BlockSpec((tc, LANES), lambda m: (m, 0)), pl.BlockSpec((1, LANES), lambda m: (0, 0))],
        out_specs=pl.BlockSpec((tc, LANES), lambda m: (m, 0)),
        out_shape=jax.ShapeDtypeStruct((s, LANES), F32),
        scratch_shapes=[pltpu.VMEM((1, LANES), F32)],
        compiler_params=_cparams("arbitrary"),
        name="fox_cumsum",
    )(f_raw, b_pad)


def _rope_tables(positions, rot_dim, period):
    half = rot_dim // 2
    inv_freq = ROPE_THETA ** (-jnp.arange(0, rot_dim, 2, dtype=F32) / rot_dim)
    ang = positions.astype(F32)[:, None] * inv_freq
    cos, sin = jnp.cos(ang), jnp.sin(ang)
    lp = jnp.arange(LANES) % period
    in_x1 = lp < half
    in_x2 = (lp >= half) & (lp < 2 * half)
    idx = jnp.where(in_x1, lp, jnp.clip(lp - half, 0, half - 1))
    cos_l, sin_l = cos[:, idx], sin[:, idx]
    c = jnp.where(in_x1 | in_x2, cos_l, 1.0)
    s1 = jnp.where(in_x2, sin_l, 0.0)
    s2 = jnp.where(in_x1, -sin_l, 0.0)
    return c, s1, s2


def _mla_mixer(h, tabs, w_in, q_a_g, kv_a_g, w_q_b, w_kv_b, q_g, k_g):
    s = h.shape[0]
    nope = HEAD_DIM
    scale = MLA_HEAD ** -0.5 * LOG2E
    gains = jnp.concatenate([q_a_g, kv_a_g])
    c_norm = _proj(h, w_in, 0, 0, 2 * MLA_LORA, tn=MLA_LORA, gain=gains, gw=MLA_LORA, real=MLA_LORA)
    kpe = _proj(h, w_in, 0, 2 * MLA_LORA, LANES, tn=LANES, out_dtype=F32,
                epi=functools.partial(_epi_keep_lanes, n_keep=MLA_ROPE))
    wq = w_q_b[0].reshape(MLA_LORA, N_HEADS, MLA_HEAD)
    wq = jnp.concatenate([wq[:, :, MLA_ROPE:], wq[:, :, :MLA_ROPE],
                          jnp.zeros((MLA_LORA, N_HEADS, 2 * LANES - MLA_HEAD), F32)], axis=2)
    wq = wq.reshape(1, MLA_LORA, N_HEADS * 2 * LANES)
    qg = jnp.concatenate([q_g[MLA_ROPE:], q_g[:MLA_ROPE], jnp.zeros((2 * LANES - MLA_HEAD,), F32)])
    q = _proj(c_norm, wq, 0, 0, N_HEADS * 2 * LANES, tn=TN_WIDE, gain=jnp.tile(qg, N_HEADS), tabs=tabs,
              gw=2 * LANES, real=MLA_HEAD, half=MLA_ROPE // 2, rope_groups=lambda g: g % 2 == 1, scale=scale)
    tm, tn = min(TM, s), TN
    aux = [(kpe, (tm, LANES), lambda n, m: (m, 0)),
           (k_g[MLA_ROPE:].reshape(1, nope), (1, nope), lambda n, m: (0, 0)),
           (jnp.pad(k_g[:MLA_ROPE], (0, LANES - MLA_ROPE)).reshape(1, LANES), (1, LANES), lambda n, m: (0, 0))]
    aux += _row_tab_aux(tabs, tm)
    outs = [((s, N_HEADS * 2 * LANES), BF16, (tm, tn), lambda n, m: (m, n)),
            ((s, N_HEADS * HEAD_DIM), BF16, (tm, tn // 2), lambda n, m: (m, n))]
    k, v = _mm(c_norm, [(w_kv_b, 0, 0)], aux, outs, _epi_mla_kv, tm=tm, tn=tn,
               n_tiles=N_HEADS * 2 * LANES // tn, a_blk=1, name="mla_kv")
    return _dense_attention(q, k, v, mode="mla", dq=2 * LANES)


def _dsa_mixer(h, tabs_head, tabs_idx, w_in, q_g, k_g, idx_k_g):
    s = h.shape[0]
    hd = N_HEADS * HEAD_DIM
    ih = IDX_HEADS * IDX_DIM
    half = HEAD_DIM // 8
    q = _proj(h, w_in, 0, 0, hd, tn=TN_WIDE, gain=jnp.tile(q_g, N_HEADS), tabs=tabs_head, half=half,
              scale=HEAD_DIM ** -0.5 * LOG2E)
    k = _proj(h, w_in, 0, hd, hd, tn=TN_WIDE, gain=jnp.tile(k_g, N_HEADS), tabs=tabs_head, half=half)
    v = _proj(h, w_in, 0, 2 * hd, hd, tn=TN_WIDE, epi=_epi_plain)
    iq = _proj(h, w_in, 0, 3 * hd, ih, tn=TN_WIDE, tabs=tabs_idx, half=IDX_DIM // 8)
    assert (3 * hd + ih) % LANES == 0 and w_in.shape[2] - (3 * hd + ih) == IDX_DIM + IDX_HEADS
    tm = min(TM_LONG, s)
    aux = [(jnp.pad(idx_k_g, (0, LANES - IDX_DIM)).reshape(1, LANES), (1, LANES), lambda n, m: (0, 0))]
    aux += _row_tab_aux(tabs_idx, tm)
    outs = [((s, LANES), BF16, (tm, LANES), lambda n, m: (m, 0)), ((s, LANES), F32, (tm, LANES), lambda n, m: (m, 0))]
    ik2, iw = _mm(h, [(w_in, 0, (3 * hd + ih) // LANES)], aux, outs, _epi_index_key_and_weights, tm=tm, tn=LANES,
                  n_tiles=1, name="proj")
    bias = _dsa_select_bias(iq, ik2, iw, min(IDX_TOPK, s // 4))
    return _dsa_attention(q, k, v, bias)


def _diff_mixer(h, tabs, layer_idx, w_in, q_g, k_g, lam_params, subln_g):
    w = N_HEADS * 2 * DIFF_DIM
    half = DIFF_DIM // 8
    q = _proj(h, w_in, 0, 0, w, tn=TN_WIDE, gain=jnp.tile(q_g, 2 * N_HEADS), tabs=tabs, gw=DIFF_DIM, real=DIFF_DIM,
              half=half, scale=DIFF_DIM ** -0.5 * LOG2E)
    k = _proj(h, w_in, 0, w, w, tn=TN_WIDE, gain=jnp.tile(k_g, 2 * N_HEADS), tabs=tabs, gw=DIFF_DIM, real=DIFF_DIM,
              half=half)
    v = _proj(h, w_in, 0, 2 * w, w, tn=TN_WIDE, epi=_epi_plain)
    lam_init = 0.8 - 0.6 * math.exp(-0.3 * layer_idx)
    return _dense_attention(q, k, v, mode="diff", dq=HEAD_DIM, extra=(lam_params, subln_g.reshape(1, HEAD_DIM)),
                            lam_init=lam_init)


def _fox_mixer(h, w_in, b_f, q_g, k_g):
    s = h.shape[0]
    hd = N_HEADS * HEAD_DIM
    q = _proj(h, w_in, 0, 0, hd, tn=TN_WIDE, gain=jnp.tile(q_g, N_HEADS), scale=HEAD_DIM ** -0.5 * LOG2E)
    k = _proj(h, w_in, 0, hd, hd, tn=TN_WIDE, gain=jnp.tile(k_g, N_HEADS))
    v = _proj(h, w_in, 0, 2 * hd, hd, tn=TN_WIDE, epi=_epi_plain)
    f_raw = _proj(h, w_in, 0, 3 * hd, LANES, tn=LANES, out_dtype=F32,
                  epi=functools.partial(_epi_keep_lanes, n_keep=N_HEADS))
    gate_blk = 3 * hd // TN
    tm = min(TM, s)
    gate = _mm(h, [(w_in, 0, gate_blk), (w_in, 0, gate_blk + 1)], [],
               [((s, hd), F32, (tm, TN), lambda n, m: (m, n))], _epi_sigmoid, tm=tm, tn=TN, n_tiles=hd // TN,
               name="proj", lane_shift=N_HEADS)[0]
    cum = _fox_cumsum(f_raw, jnp.pad(b_f, (0, LANES - N_HEADS)).reshape(1, LANES))
    cumt = cum[:, :N_HEADS].T.reshape(N_HEADS, 1, s)
    return _dense_attention(q, k, v, mode="fox", dq=HEAD_DIM, extra=(cum, cumt, gate))


def _swiglu_ffn(h, x, gate, w_gate_up, w_down, layer):
    s = h.shape[0]
    tm, tn = min(TM_LONG, s), TN
    outs = [((s, FFN_HIDDEN), BF16, (tm, tn), lambda n, m: (m, n))]
    act = _mm(h, [(w_gate_up, layer, 0), (w_gate_up, layer, FFN_HIDDEN // tn)], [], outs, _epi_swiglu,
              tm=tm, tn=tn, n_tiles=FFN_HIDDEN // tn, name="ffn_gate_up")[0]
    return _residual_mm(act, w_down, layer, x, gate, tm=TM_DEEP)


def kernel(x, c, positions, ln_mix_g, ln_ffn_g, ada_w, ada_b, ffn_w_gate_up, ffn_w_down, mla_w_in, mla_q_a_g, mla_kv_a_g, mla_w_q_b, mla_w_kv_b, mla_q_g, mla_k_g, mla_w_out, dsa_w_in, dsa_q_g, dsa_k_g, dsa_idx_k_g, dsa_w_out, diff_w_in, diff_q_g, diff_k_g, diff_lambda_q1, diff_lambda_k1, diff_lambda_q2, diff_lambda_k2, diff_subln_g, diff_w_out, fox_w_in, fox_b_f, fox_q_g, fox_k_g, fox_w_out):
    batch, s, d = x.shape
    assert batch == 1 and d == D_MODEL
    depth = ada_w.shape[0]
    pos = positions[0]
    tabs_head = _rope_tables(pos, HEAD_DIM // 4, LANES)
    tabs_small = _rope_tables(pos, IDX_DIM // 4, IDX_DIM)
    tabs_mla = _rope_tables(pos, MLA_ROPE, LANES)
    mod = _ada_mod(c, ada_w, ada_b)
    xs = x[0]
    for i in range(depth):
        sh1, sc1, g1, sh2, sc2, g2 = [mod[i, t * d:(t + 1) * d] for t in range(6)]
        h = _normmod(xs, ln_mix_g[i], sc1, sh1)
        kind, j = i % 4, i // 4
        if kind == 0:
            o = _mla_mixer(h, tabs_mla, mla_w_in[j:j + 1], mla_q_a_g[j], mla_kv_a_g[j], mla_w_q_b[j:j + 1],
                           mla_w_kv_b[j:j + 1], mla_q_g[j], mla_k_g[j])
            w_out = mla_w_out
        elif kind == 1:
            o = _dsa_mixer(h, tabs_head, tabs_small, dsa_w_in[j:j + 1], dsa_q_g[j], dsa_k_g[j], dsa_idx_k_g[j])
            w_out = dsa_w_out
        elif kind == 2:
            lam_params = jnp.stack([diff_lambda_q1[j], diff_lambda_k1[j], diff_lambda_q2[j], diff_lambda_k2[j]])
            o = _diff_mixer(h, tabs_small, i, diff_w_in[j:j + 1], diff_q_g[j], diff_k_g[j], lam_params,
                            diff_subln_g[j])
            w_out = diff_w_out
        else:
            o = _fox_mixer(h, fox_w_in[j:j + 1], fox_b_f[j], fox_q_g[j], fox_k_g[j])
            w_out = fox_w_out
        xs = _residual_mm(o, w_out, j, xs, g1, tn=TN_WIDE)
        h = _normmod(xs, ln_ffn_g[i], sc2, sh2)
        xs = _swiglu_ffn(h, xs, g2, ffn_w_gate_up, ffn_w_down, i)
    return xs[None]
```

```python
import functools
import math

import jax
import jax.numpy as jnp
from jax import lax
from jax.experimental import pallas as pl
from jax.experimental.pallas import tpu as pltpu

F32 = jnp.float32
BF16 = jnp.bfloat16

D_MODEL = 2048
N_HEADS = 16
HEAD_DIM = 128
ROPE_THETA = 500000.0
RMS_EPS = 1e-6
FFN_HIDDEN = 5632
MLA_LORA = 512
MLA_ROPE = 64
MLA_HEAD = 192
IDX_HEADS = 16
IDX_DIM = 64
IDX_TOPK = 256
DIFF_DIM = 64

LANES = 128
LOG2E = 1.4426950408889634
NEG = -1e30
VMEM_LIMIT_BYTES = 56 * 1024 * 1024
TM = 1024
TM_LONG = 2048
TM_DEEP = 512
SUB_ROWS = 256
TN = 512
TN_WIDE = 1024
TN_ADA = 1536
TQ_ATTN = 512
TQ_SELECT = 256
T_CUMSUM = 512
HEADS_PER_STEP = 2
BISECT_ROWS = 128
BISECT_UNCHECKED = 8
INT_MIN = -2 ** 31
KEY_NEG_INF = (0xFF800000 ^ 0x7FFFFFFF) - 2 ** 32


def _cparams(*sem):
    return pltpu.CompilerParams(dimension_semantics=sem, vmem_limit_bytes=VMEM_LIMIT_BYTES)


def _dot_nt(a, b):
    return lax.dot_general(a, b, (((1,), (1,)), ((), ())), preferred_element_type=F32)


def _sigmoid(z):
    return 1.0 / (1.0 + jnp.exp(-z))


def _mm_body(*refs, n_w, n_aux, n_out, epi, sub, lane_shift, w_t):
    a_ref = refs[0]
    w_refs = refs[1:1 + n_w]
    aux_refs = refs[1 + n_w:1 + n_w + n_aux]
    out_refs = refs[1 + n_w + n_aux:1 + n_w + n_aux + n_out]
    wb_refs = refs[1 + n_w + n_aux + n_out:]
    tm = a_ref.shape[0]

    @pl.when(pl.program_id(1) == 0)
    def _cast_weights():
        if lane_shift:
            w0, w1 = w_refs
            if w_t:
                shifted = jnp.concatenate([w0[lane_shift:, :], w1[:lane_shift, :]], axis=0)
            else:
                shifted = jnp.concatenate([w0[:, lane_shift:], w1[:, :lane_shift]], axis=1)
            wb_refs[0][...] = shifted.astype(BF16)
            return
        for w_ref, wb_ref in zip(w_refs, wb_refs):
            wb_ref[...] = w_ref[...].astype(BF16)

    def product(a, wb):
        return _dot_nt(a, wb) if w_t else jnp.dot(a, wb, preferred_element_type=F32)

    for r in range(tm // sub):
        rows = pl.ds(r * sub, sub)
        accs = [product(a_ref[rows, :], wb_ref[...]) for wb_ref in wb_refs]
        epi(accs, [ref.at[rows] if ref.shape[0] == tm else ref for ref in aux_refs],
            [ref.at[rows] for ref in out_refs])


def _mm(a, ws, aux, outs, epi, *, tm, tn, n_tiles, a_blk=0, name="mm", sub=SUB_ROWS, lane_shift=0, w_t=False):
    s = a.shape[0]
    k = ws[0][0].shape[2 if w_t else 1]
    tm = min(tm, s)
    in_specs = [pl.BlockSpec((tm, k), lambda n, m: (m, a_blk))]
    operands = [a]
    for w, layer, off in ws:
        if w_t:
            in_specs.append(pl.BlockSpec((None, tn, k), lambda n, m, layer=layer, off=off: (layer, n + off, 0)))
        else:
            in_specs.append(pl.BlockSpec((None, k, tn), lambda n, m, layer=layer, off=off: (layer, 0, n + off)))
        operands.append(w)
    for arr, block, imap in aux:
        in_specs.append(pl.BlockSpec(block, imap))
        operands.append(arr)
    out_shape = [jax.ShapeDtypeStruct(shape, dtype) for shape, dtype, _, _ in outs]
    out_specs = [pl.BlockSpec(block, imap) for _, _, block, imap in outs]
    body = functools.partial(_mm_body, n_w=len(ws), n_aux=len(aux), n_out=len(outs), epi=epi, sub=min(sub, tm),
                             lane_shift=lane_shift, w_t=w_t)
    res = pl.pallas_call(
        body,
        grid=(n_tiles, s // tm),
        in_specs=in_specs,
        out_specs=out_specs,
        out_shape=out_shape,
        scratch_shapes=[pltpu.VMEM((tn, k) if w_t else (k, tn), BF16) for _ in (ws[:1] if lane_shift else ws)],
        compiler_params=_cparams("arbitrary", "arbitrary"),
        name=name,
    )(*operands)
    return res


def _rope_group(y, tabs, half):
    c, s1, s2 = tabs
    return y * c + pltpu.roll(y, half, 1) * s1 + pltpu.roll(y, LANES - half, 1) * s2


def _group_inv_rms(y, gw, real):
    tn = y.shape[1]
    sq = y * y
    n_groups = tn // LANES
    if gw == DIFF_DIM:
        lane = lax.broadcasted_iota(jnp.int32, (1, LANES), 1)
        lo = lane < DIFF_DIM
        out = []
        for g in range(n_groups):
            sg = sq[:, g * LANES:(g + 1) * LANES]
            s_lo = jnp.sum(jnp.where(lo, sg, 0.0), axis=1, keepdims=True)
            s_hi = jnp.sum(jnp.where(lo, 0.0, sg), axis=1, keepdims=True)
            out.append(lax.rsqrt(jnp.where(lo, s_lo, s_hi) * (1.0 / real) + RMS_EPS))
        return out
    per = gw // LANES
    out = []
    for h in range(n_groups // per):
        folded = sq[:, h * gw:h * gw + LANES]
        for t in range(1, per):
            folded = folded + sq[:, h * gw + t * LANES:h * gw + (t + 1) * LANES]
        r = lax.rsqrt(jnp.sum(folded, axis=1, keepdims=True) * (1.0 / real) + RMS_EPS)
        out.extend([r] * per)
    return out


def _epi_heads(accs, aux_refs, out_refs, *, gw, real, half, rope_groups, scale, use_gain, use_rope):
    y = accs[0]
    tn = y.shape[1]
    idx = 0
    gain = None
    if use_gain:
        gain = aux_refs[idx][...]
        idx += 1
        inv = _group_inv_rms(y, gw, real)
    if use_rope:
        tabs = tuple(aux_refs[idx + t][...] for t in range(3))
    for g in range(tn // LANES):
        yg = y[:, g * LANES:(g + 1) * LANES]
        if use_gain:
            yg = yg * inv[g] * gain[:, g * LANES:(g + 1) * LANES]
        if use_rope and rope_groups(g):
            yg = _rope_group(yg, tabs, half)
        if scale != 1.0:
            yg = yg * scale
        out_refs[0][:, g * LANES:(g + 1) * LANES] = yg.astype(out_refs[0].dtype)


def _epi_plain(accs, aux_refs, out_refs):
    out_refs[0][...] = accs[0].astype(out_refs[0].dtype)


def _epi_sigmoid(accs, aux_refs, out_refs):
    out_refs[0][...] = _sigmoid(accs[0]).astype(out_refs[0].dtype)


def _epi_residual(accs, aux_refs, out_refs):
    x_ref, g_ref = aux_refs
    out_refs[0][...] = x_ref[...] + g_ref[...] * accs[0]


def _epi_swiglu(accs, aux_refs, out_refs):
    gate, up = accs
    out_refs[0][...] = (gate * _sigmoid(gate) * up).astype(out_refs[0].dtype)


def _epi_mla_kv(accs, aux_refs, out_refs):
    kpe_ref, g_nope_ref, g_pe_ref, c_ref, s1_ref, s2_ref = aux_refs
    k_ref, v_ref = out_refs
    y = accs[0]
    kpe = kpe_ref[...]
    ss_pe = jnp.sum(kpe * kpe, axis=1, keepdims=True)
    tabs = (c_ref[...], s1_ref[...], s2_ref[...])
    g_nope = g_nope_ref[...]
    pe_roped = _rope_group(kpe * g_pe_ref[...], tabs, MLA_ROPE // 2)
    for h in range(y.shape[1] // (2 * LANES)):
        kn = y[:, 2 * h * LANES:(2 * h + 1) * LANES]
        v = y[:, (2 * h + 1) * LANES:(2 * h + 2) * LANES]
        r = lax.rsqrt((jnp.sum(kn * kn, axis=1, keepdims=True) + ss_pe) * (1.0 / MLA_HEAD) + RMS_EPS)
        k_ref[:, 2 * h * LANES:(2 * h + 1) * LANES] = (kn * r * g_nope).astype(k_ref.dtype)
        k_ref[:, (2 * h + 1) * LANES:(2 * h + 2) * LANES] = (pe_roped * r).astype(k_ref.dtype)
        v_ref[:, h * LANES:(h + 1) * LANES] = v.astype(v_ref.dtype)


def _epi_index_key_and_weights(accs, aux_refs, out_refs):
    g_ref, c_ref, s1_ref, s2_ref = aux_refs
    ik2_ref, iw_ref = out_refs
    y = accs[0]
    lane = lax.broadcasted_iota(jnp.int32, (1, LANES), 1)
    yk = jnp.where(lane < IDX_DIM, y, 0.0)
    inv = lax.rsqrt(jnp.sum(yk * yk, axis=1, keepdims=True) * (1.0 / IDX_DIM) + RMS_EPS)
    key = _rope_group(yk * inv * g_ref[...], (c_ref[...], s1_ref[...], s2_ref[...]), IDX_DIM // 8)
    ik2_ref[...] = (key + pltpu.roll(key, IDX_DIM, 1)).astype(ik2_ref.dtype)
    weights = pltpu.roll(y, LANES - IDX_DIM, 1) * (IDX_HEADS ** -0.5 * IDX_DIM ** -0.5)
    iw_ref[...] = jnp.where(lane < IDX_HEADS, weights, 0.0)


def _epi_keep_lanes(accs, aux_refs, out_refs, *, n_keep):
    lane = lax.broadcasted_iota(jnp.int32, (1, LANES), 1)
    out_refs[0][...] = jnp.where(lane < n_keep, accs[0], 0.0).astype(out_refs[0].dtype)


def _row_tab_aux(tabs, tm):
    return [(t, (tm, LANES), lambda n, m: (m, 0)) for t in tabs]


def _proj(a, w, layer, col_off, width, *, tn, out_dtype=BF16, gain=None, tabs=None, gw=LANES, real=LANES,
          half=0, rope_groups=lambda g: True, scale=1.0, epi=None, tm=TM_LONG, a_blk=0, w_t=False):
    s = a.shape[0]
    tm = min(tm, s)
    assert col_off % tn == 0 and width % tn == 0
    aux = []
    if gain is not None:
        aux.append((gain.reshape(1, width).astype(F32), (1, tn), lambda n, m: (0, n)))
    if tabs is not None:
        aux.extend(_row_tab_aux(tabs, tm))
    if epi is None:
        epi = functools.partial(_epi_heads, gw=gw, real=real, half=half, rope_groups=rope_groups, scale=scale,
                                use_gain=gain is not None, use_rope=tabs is not None)
    outs = [((s, width), out_dtype, (tm, tn), lambda n, m: (m, n))]
    return _mm(a, [(w, layer, col_off // tn)], aux, outs, epi, tm=tm, tn=tn, n_tiles=width // tn, a_blk=a_blk,
               name="proj", w_t=w_t)[0]


def _residual_mm(a, w, layer, x, gate, *, tn=TN, tm=TM):
    s, d = x.shape
    tm = min(tm, s)
    aux = [(x, (tm, tn), lambda n, m: (m, n)), (gate.reshape(1, d), (1, tn), lambda n, m: (0, n))]
    outs = [((s, d), F32, (tm, tn), lambda n, m: (m, n))]
    return _mm(a, [(w, layer, 0)], aux, outs, _epi_residual, tm=tm, tn=tn, n_tiles=d // tn, name="residual_mm")[0]


def _ada_body(c_ref, w_ref, b_ref, o_ref):
    c = c_ref[...]
    cond = c * _sigmoid(c)
    o_ref[...] = jnp.dot(cond.astype(BF16), w_ref[...].astype(BF16), preferred_element_type=F32) + b_ref[...]


def _ada_mod(c, ada_w, ada_b):
    depth, d, n = ada_w.shape
    tn = TN_ADA
    c8 = jnp.broadcast_to(c, (8, d))
    out = pl.pallas_call(
        _ada_body,
        grid=(depth, n // tn),
        in_specs=[pl.BlockSpec((8, d), lambda l, j: (0, 0)),
                  pl.BlockSpec((None, d, tn), lambda l, j: (l, 0, j)),
                  pl.BlockSpec((None, 1, tn), lambda l, j: (l, 0, j))],
        out_specs=pl.BlockSpec((None, 8, tn), lambda l, j: (l, 0, j)),
        out_shape=jax.ShapeDtypeStruct((depth, 8, n), F32),
        compiler_params=_cparams("arbitrary", "arbitrary"),
        name="ada_mod",
    )(c8, ada_w, ada_b.reshape(depth, 1, n))
    return out[:, 0, :]


def _normmod_body(x_ref, g_ref, sc_ref, sh_ref, o_ref):
    x = x_ref[...]
    y = x * lax.rsqrt(jnp.mean(x * x, axis=1, keepdims=True) + RMS_EPS) * g_ref[...]
    o_ref[...] = (y * (1.0 + sc_ref[...]) + sh_ref[...]).astype(o_ref.dtype)


def _normmod(x, g, sc, sh, tm=TM):
    s, d = x.shape
    tm = min(tm, s)
    vec = pl.BlockSpec((1, d), lambda m: (0, 0))
    return pl.pallas_call(
        _normmod_body,
        grid=(s // tm,),
        in_specs=[pl.BlockSpec((tm, d), lambda m: (m, 0)), vec, vec, vec],
        out_specs=pl.BlockSpec((tm, d), lambda m: (m, 0)),
        out_shape=jax.ShapeDtypeStruct((s, d), BF16),
        compiler_params=_cparams("arbitrary"),
        name="normmod",
    )(x, g.reshape(1, d), sc.reshape(1, d), sh.reshape(1, d))


def _softmax_scratch(n_maps, tq, tk):
    return [pltpu.VMEM((n_maps, tq, 2 * HEAD_DIM), F32), pltpu.VMEM((n_maps, tq, LANES), F32),
            pltpu.VMEM((n_maps, 2, tq, tk), F32), pltpu.VMEM((n_maps, 2, 2, tq, LANES), F32)]


def _softmax_pipeline(qs, kv_cols, k_ref, v_ref, scratch, *, tq, tk, first_chunk, first_keep, n_rest, bias_fn):
    acc_ref, m_ref, s_ref, st_ref = scratch
    ones = jnp.ones((tk, LANES), BF16)
    n_grp = tk // LANES
    maps = range(len(qs))

    def chunk_start(c):
        return pl.multiple_of(c * tk, tk)

    def stage1(c, slot, keep):
        start = chunk_start(c)
        for t in maps:
            s = _dot_nt(qs[t], k_ref[pl.ds(start, tk), kv_cols[t][0]])
            if bias_fn is not None:
                s = s + bias_fn(t, start)
            if keep is not None:
                s = jnp.where(keep, s, NEG)
            m_run = m_ref[t]
            m_new = jnp.maximum(m_run, jnp.max(s, axis=1, keepdims=True))
            s_ref[t, slot] = s
            st_ref[t, slot, 0] = m_new
            st_ref[t, slot, 1] = jnp.exp2(m_run - m_new)
            m_ref[t] = m_new

    def stage2(c, slot):
        start = chunk_start(c)
        for t in maps:
            v1 = jnp.concatenate([v_ref[pl.ds(start, tk), kv_cols[t][1]], ones], axis=1)
            m_new = st_ref[t, slot, 0]
            alpha = st_ref[t, slot, 1]
            p = jnp.concatenate([jnp.exp2(s_ref[t, slot, :, g * LANES:(g + 1) * LANES] - m_new)
                                 for g in range(n_grp)], axis=1)
            pv = jnp.dot(p.astype(BF16), v1, preferred_element_type=F32)
            acc_ref[t] = jnp.concatenate([alpha, alpha], axis=1) * acc_ref[t] + pv

    acc_ref[...] = jnp.zeros(acc_ref.shape, F32)
    m_ref[...] = jnp.full(m_ref.shape, NEG, F32)
    stage1(first_chunk, 0, first_keep)

    def two_chunks(c0):
        stage1(c0, 1, None)
        stage2(jnp.where(c0 == 0, first_chunk, c0 - 1), 0)
        stage1(c0 + 1, 0, None)
        stage2(c0, 1)

    def body(t4, carry):
        two_chunks(4 * t4)
        two_chunks(4 * t4 + 2)
        return carry

    n_quads = n_rest // 4
    lax.fori_loop(0, n_quads, body, 0)
    n_pairs = n_rest // 2

    @pl.when(n_pairs > 2 * n_quads)
    def _pair_tail():
        two_chunks(4 * n_quads)

    pending = jnp.where(n_pairs == 0, first_chunk, 2 * n_pairs - 1)

    @pl.when(n_rest % 2 == 1)
    def _odd_tail():
        stage1(n_rest - 1, 1, None)
        stage2(pending, 0)
        stage2(n_rest - 1, 1)

    @pl.when(n_rest % 2 == 0)
    def _even_tail():
        stage2(pending, 0)


def _causal_keep(tq, tk, col_shift):
    row = lax.broadcasted_iota(jnp.int32, (tq, 1), 0)
    col = lax.broadcasted_iota(jnp.int32, (1, tk), 1)
    return row >= col + col_shift


def _dense_attn_body(*refs, mode, tq, tk, lam_init):
    if mode == "mla":
        q_ref, k_ref, v_ref, o_ref = refs[:4]
        scratch = refs[4:]
    elif mode == "fox":
        q_ref, k_ref, v_ref, cum_ref, cumt_ref, gate_ref, o_ref = refs[:7]
        scratch = refs[7:]
    else:
        q_ref, k_ref, v_ref, lam_ref, subg_ref, o_ref = refs[:6]
        scratch = refs[6:]
    hg = pl.program_id(0)
    i = pl.program_id(1)
    acc_ref = scratch[0]
    dq = q_ref.shape[1] // HEADS_PER_STEP
    heads = range(HEADS_PER_STEP)
    lane = lax.broadcasted_iota(jnp.int32, (1, LANES), 1)

    def head_cols(hh, width=HEAD_DIM):
        return slice(hh * width, (hh + 1) * width)

    qs, kv_cols = [], []
    for hh in heads:
        q = q_ref[:, head_cols(hh, dq)]
        if mode == "diff":
            zero = jnp.zeros_like(q)
            qs += [jnp.where(lane < DIFF_DIM, q, zero), jnp.where(lane >= DIFF_DIM, q, zero)]
            kv_cols += [(head_cols(hh, dq), head_cols(hh))] * 2
        else:
            qs.append(q)
            kv_cols.append((head_cols(hh, dq), head_cols(hh)))
    if mode == "fox":
        cum = cum_ref[...]
        cqs = [jnp.sum(jnp.where(lane == hg * HEADS_PER_STEP + hh, cum, 0.0), axis=1, keepdims=True) * LOG2E
               for hh in heads]

        def bias_fn(t, start):
            return cqs[t] - cumt_ref[t, :, pl.ds(start, tk)] * LOG2E
    else:
        bias_fn = None

    _softmax_pipeline(qs, kv_cols, k_ref, v_ref, scratch, tq=tq, tk=tk, first_chunk=i,
                      first_keep=_causal_keep(tq, tk, 0), n_rest=i, bias_fn=bias_fn)

    def normalised(t):
        acc = acc_ref[t]
        return acc[:, :HEAD_DIM] / acc[:, HEAD_DIM:]

    for hh in heads:
        if mode == "diff":
            lp = lam_ref[...]
            lam = (jnp.exp(jnp.sum(lp[0:1] * lp[1:2], axis=1, keepdims=True))
                   - jnp.exp(jnp.sum(lp[2:3] * lp[3:4], axis=1, keepdims=True)) + lam_init)
            o = normalised(2 * hh) - lam * normalised(2 * hh + 1)
            o = o * lax.rsqrt(jnp.mean(o * o, axis=1, keepdims=True) + RMS_EPS) * subg_ref[...]
            o = o * (1.0 - lam_init)
        else:
            o = normalised(hh)
            if mode == "fox":
                o = o * gate_ref[:, head_cols(hh)]
        o_ref[:, head_cols(hh)] = o.astype(o_ref.dtype)


def _dense_attention(q, k, v, *, mode, dq, extra=(), lam_init=0.0, tq=TQ_ATTN):
    s = q.shape[0]
    tq = min(tq, s)
    tk = tq
    assert s % tq == 0 and N_HEADS % HEADS_PER_STEP == 0
    g = HEADS_PER_STEP
    in_specs = [pl.BlockSpec((tq, g * dq), lambda h, i: (i, h)),
                pl.BlockSpec((s, g * dq), lambda h, i: (0, h)),
                pl.BlockSpec((s, g * HEAD_DIM), lambda h, i: (0, h))]
    operands = [q, k, v]
    if mode == "fox":
        cum, cumt, gate = extra
        in_specs += [pl.BlockSpec((tq, LANES), lambda h, i: (i, 0)),
                     pl.BlockSpec((g, 1, s), lambda h, i: (h, 0, 0)),
                     pl.BlockSpec((tq, g * HEAD_DIM), lambda h, i: (i, h))]
        operands += [cum, cumt, gate]
    elif mode == "diff":
        lam_params, subln_g = extra
        in_specs += [pl.BlockSpec((4, DIFF_DIM), lambda h, i: (0, 0)),
                     pl.BlockSpec((1, HEAD_DIM), lambda h, i: (0, 0))]
        operands += [lam_params, subln_g]
    n_maps = g * (2 if mode == "diff" else 1)
    body = functools.partial(_dense_attn_body, mode=mode, tq=tq, tk=tk, lam_init=lam_init)
    return pl.pallas_call(
        body,
        grid=(N_HEADS // g, s // tq),
        in_specs=in_specs,
        out_specs=pl.BlockSpec((tq, g * HEAD_DIM), lambda h, i: (i, h)),
        out_shape=jax.ShapeDtypeStruct((s, N_HEADS * HEAD_DIM), BF16),
        scratch_shapes=_softmax_scratch(n_maps, tq, tk),
        compiler_params=_cparams("arbitrary", "arbitrary"),
        name="attn_" + mode,
    )(*operands)


def _dsa_attn_body(q_ref, k_ref, v_ref, bias_ref, o_ref, *scratch, tq, tk):
    i = pl.program_id(0)

    def bias_fn(t, start):
        return bias_ref[:, pl.ds(start, tk)].astype(F32)

    cols = [slice(hh * HEAD_DIM, (hh + 1) * HEAD_DIM) for hh in range(HEADS_PER_STEP)]
    _softmax_pipeline([q_ref[:, c] for c in cols], [(c, c) for c in cols], k_ref, v_ref, scratch, tq=tq, tk=tk,
                      first_chunk=i, first_keep=None, n_rest=i, bias_fn=bias_fn)
    for t, c in enumerate(cols):
        acc = scratch[0][t]
        o_ref[:, c] = (acc[:, :HEAD_DIM] / acc[:, HEAD_DIM:]).astype(o_ref.dtype)


def _dsa_attention(q, k, v, bias, *, tq=TQ_ATTN):
    s = q.shape[0]
    tq = min(tq, s)
    tk = tq
    assert s % tq == 0 and N_HEADS % HEADS_PER_STEP == 0
    body = functools.partial(_dsa_attn_body, tq=tq, tk=tk)
    return pl.pallas_call(
        body,
        grid=(s // tq, N_HEADS // HEADS_PER_STEP),
        in_specs=[pl.BlockSpec((tq, HEADS_PER_STEP * HEAD_DIM), lambda i, h: (i, h)),
                  pl.BlockSpec((s, HEADS_PER_STEP * HEAD_DIM), lambda i, h: (0, h)),
                  pl.BlockSpec((s, HEADS_PER_STEP * HEAD_DIM), lambda i, h: (0, h)),
                  pl.BlockSpec((tq, s), lambda i, h: (i, 0))],
        out_specs=pl.BlockSpec((tq, HEADS_PER_STEP * HEAD_DIM), lambda i, h: (i, h)),
        out_shape=jax.ShapeDtypeStruct((s, N_HEADS * HEAD_DIM), BF16),
        scratch_shapes=_softmax_scratch(HEADS_PER_STEP, tq, tk),
        compiler_params=_cparams("arbitrary", "arbitrary"),
        name="attn_dsa",
    )(q, k, v, bias)


def _sortable_key(score):
    bits = lax.bitcast_convert_type(score + 0.0, jnp.int32)
    return bits ^ ((bits >> 31) & 0x7FFFFFFF)


def _indexer_body(iq_ref, ik_ref, iw_ref, bias_ref, keys_ref, qz_ref, wb_ref, thr_ref, need_ref, *, tq, n_sel):
    i = pl.program_id(0)
    s_len = ik_ref.shape[0]
    n_chunks = s_len // tq
    lane = lax.broadcasted_iota(jnp.int32, (1, LANES), 1)
    iw = iw_ref[...]
    for p in range(IDX_HEADS // 2):
        pair = iq_ref[:, p * LANES:(p + 1) * LANES]
        zero = jnp.zeros_like(pair)
        qz_ref[2 * p] = jnp.where(lane < IDX_DIM, pair, zero)
        qz_ref[2 * p + 1] = jnp.where(lane >= IDX_DIM, pair, zero)
    for hh in range(IDX_HEADS):
        wb_ref[hh] = jnp.broadcast_to(iw[:, hh:hh + 1], (tq, LANES))

    def score_chunk(start):
        ik = ik_ref[pl.ds(start, tq), :]
        sc = jnp.zeros((tq, tq), F32)
        for hh in range(IDX_HEADS):
            rel = jnp.maximum(_dot_nt(qz_ref[hh], ik), 0.0)
            sc = sc + jnp.tile(wb_ref[hh], (1, tq // LANES)) * rel
        return sc

    def full_body(j, carry):
        start = pl.multiple_of(j * tq, tq)
        keys_ref[:, pl.ds(start, tq)] = _sortable_key(score_chunk(start))
        return carry

    lax.fori_loop(0, i, full_body, 0)
    diag = pl.multiple_of(i * tq, tq)
    sc = jnp.where(_causal_keep(tq, tq, 0), score_chunk(diag), -jnp.inf)
    keys_ref[:, pl.ds(diag, tq)] = _sortable_key(sc)

    keys_ref[:, pl.ds(pl.multiple_of((i + 1) * tq, tq), tq)] = jnp.full((tq, tq), KEY_NEG_INF, jnp.int32)
    strips = [pl.ds(r0, BISECT_ROWS) for r0 in range(0, tq, BISECT_ROWS)]
    n_trips = (i + 2) // 2
    lanes_per_trip = 2 * tq // LANES

    def count_keys(rows, trial, strict):
        def count_body(j, cnt):
            start = pl.multiple_of(j * (2 * tq), 2 * tq)
            for g in range(lanes_per_trip):
                kk = keys_ref[rows, pl.ds(start + g * LANES, LANES)]
                cnt = cnt + ((kk > trial) if strict else (kk >= trial)).astype(jnp.int32)
            return cnt

        return lax.fori_loop(0, n_trips, count_body, jnp.zeros((BISECT_ROWS, LANES), jnp.int32))

    def key_bounds(rows):
        def max_body(j, gm):
            start = pl.multiple_of(j * (2 * tq), 2 * tq)
            gm = list(gm)
            for g in range(lanes_per_trip):
                gm[g % 2] = jnp.maximum(gm[g % 2], keys_ref[rows, pl.ds(start + g * LANES, LANES)])
            return tuple(gm)

        init = jnp.full((BISECT_ROWS, LANES), KEY_NEG_INF, jnp.int32)
        g0, g1 = lax.fori_loop(0, n_trips, max_body, (init, init))
        lo = jnp.min(jnp.minimum(g0, g1), axis=1, keepdims=True)
        hi = jnp.max(jnp.maximum(g0, g1), axis=1, keepdims=True)
        lo = jnp.maximum(lo, KEY_NEG_INF + 1) + jnp.zeros((BISECT_ROWS, LANES), jnp.int32)
        return lo, jnp.maximum(hi, lo) + 1

    def open_rows(lo, hi):
        width = hi - lo
        return ((width != 0) & (width != 1)).astype(jnp.int32)

    def halve(bounds):
        mids = [lo + lax.shift_right_logical(hi - lo, 1) for lo, hi in bounds]
        counts = [count_keys(rows, mid, False) for rows, mid in zip(strips, mids)]
        out = []
        for cnt, mid, (lo, hi) in zip(counts, mids, bounds):
            total = jnp.sum(cnt, axis=1, keepdims=True)
            enough = total >= n_sel
            out.append((jnp.where(enough, mid, lo), jnp.where(total == n_sel, mid + 1, jnp.where(enough, hi, mid))))
        return out

    def flat(bounds):
        return tuple(b for pair in bounds for b in pair)

    def pairs(flat_bounds):
        return [(flat_bounds[2 * t], flat_bounds[2 * t + 1]) for t in range(len(strips))]

    bounds = lax.fori_loop(0, BISECT_UNCHECKED, lambda _, fb: flat(halve(pairs(fb))),
                           flat([key_bounds(rows) for rows in strips]))

    def still_open(state):
        it, fb = state[0], state[1:]
        n_open = functools.reduce(jnp.maximum, [open_rows(lo, hi) for lo, hi in pairs(fb)])
        return (it < 32) & (jnp.max(n_open) > 0)

    bounds = lax.while_loop(still_open, lambda state: (state[0] + 1,) + flat(halve(pairs(state[1:]))),
                            (jnp.int32(0),) + bounds)[1:]

    surplus = jnp.zeros((BISECT_ROWS, LANES), jnp.int32)
    for rows, (thr, _) in zip(strips, pairs(bounds)):
        above = jnp.sum(count_keys(rows, thr, True), axis=1, keepdims=True)
        tied = jnp.sum(count_keys(rows, thr, False), axis=1, keepdims=True) - above
        need = n_sel - above
        thr_ref[rows, :] = thr
        need_ref[rows, :] = need.astype(F32) + jnp.zeros((BISECT_ROWS, LANES), F32)
        surplus = jnp.maximum(surplus, tied - need)
    any_surplus = jnp.max(surplus) > 0
    n_grp = tq // LANES

    @pl.when(jnp.logical_not(any_surplus))
    def _write_no_ties():
        def write_body(j, carry):
            start = pl.multiple_of(j * tq, tq)
            thr = thr_ref[...]
            for g in range(n_grp):
                cols = pl.ds(start + g * LANES, LANES)
                bias_ref[:, cols] = jnp.where(keys_ref[:, cols] >= thr, 0.0, NEG).astype(bias_ref.dtype)
            return carry

        lax.fori_loop(0, i + 1, write_body, 0)

    @pl.when(any_surplus)
    def _write_with_ties():
        ca = lax.broadcasted_iota(jnp.int32, (tq, tq), 0)
        cb = lax.broadcasted_iota(jnp.int32, (tq, tq), 1)
        prefix_ones = (ca <= cb).astype(BF16)
        all_ones = jnp.ones((tq, LANES), BF16)

        def write_body(j, seen):
            start = pl.multiple_of(j * tq, tq)
            kk = keys_ref[:, pl.ds(start, tq)]
            thr = jnp.concatenate([thr_ref[...]] * n_grp, axis=1)
            eq = kk == thr
            eq_b = jnp.where(eq, 1.0, 0.0).astype(BF16)
            rank = jnp.dot(eq_b, prefix_ones, preferred_element_type=F32) + jnp.concatenate([seen] * n_grp, axis=1)
            take = (kk > thr) | (eq & (rank <= jnp.concatenate([need_ref[...]] * n_grp, axis=1)))
            bias_ref[:, pl.ds(start, tq)] = jnp.where(take, 0.0, NEG).astype(bias_ref.dtype)
            return seen + jnp.dot(eq_b, all_ones, preferred_element_type=F32)

        lax.fori_loop(0, i + 1, write_body, jnp.zeros((tq, LANES), F32))

    def fill_body(j, carry):
        bias_ref[:, pl.ds(pl.multiple_of(j * tq, tq), tq)] = jnp.full((tq, tq), NEG, bias_ref.dtype)
        return carry

    lax.fori_loop(i + 1, n_chunks, fill_body, 0)


def _dsa_select_bias(iq, ik2, iw, n_sel, tq=TQ_SELECT):
    s = iq.shape[0]
    tq = min(tq, s)
    assert s % tq == 0 and tq % BISECT_ROWS == 0
    body = functools.partial(_indexer_body, tq=tq, n_sel=n_sel)
    return pl.pallas_call(
        body,
        grid=(s // tq,),
        in_specs=[pl.BlockSpec((tq, IDX_HEADS * IDX_DIM), lambda i: (i, 0)),
                  pl.BlockSpec((s, LANES), lambda i: (0, 0)),
                  pl.BlockSpec((tq, LANES), lambda i: (i, 0))],
        out_specs=pl.BlockSpec((tq, s), lambda i: (i, 0)),
        out_shape=jax.ShapeDtypeStruct((s, s), BF16),
        scratch_shapes=[pltpu.VMEM((tq, s + tq), jnp.int32),
                        pltpu.VMEM((IDX_HEADS, tq, LANES), BF16),
                        pltpu.VMEM((IDX_HEADS, tq, LANES), F32),
                        pltpu.VMEM((tq, LANES), jnp.int32),
                        pltpu.VMEM((tq, LANES), F32)],
        compiler_params=_cparams("arbitrary"),
        name="dsa_select",
    )(iq, ik2, iw)


def _fox_cum_body(f_ref, b_ref, o_ref, carry_ref, *, tc):
    @pl.when(pl.program_id(0) == 0)
    def _():
        carry_ref[...] = jnp.zeros(carry_ref.shape, F32)

    z = f_ref[...] + b_ref[...]
    lf = jnp.minimum(z, 0.0) - jnp.log(1.0 + jnp.exp(-jnp.abs(z)))
    row = lax.broadcasted_iota(jnp.int32, (tc, tc), 0)
    col = lax.broadcasted_iota(jnp.int32, (tc, tc), 1)
    tri = (row >= col).astype(BF16)
    hi = lf.astype(BF16)
    r1 = lf - hi.astype(F32)
    mid = r1.astype(BF16)
    lo = (r1 - mid.astype(F32)).astype(BF16)
    cs = (jnp.dot(tri, hi, preferred_element_type=F32) + jnp.dot(tri, mid, preferred_element_type=F32)
          + jnp.dot(tri, lo, preferred_element_type=F32)) + carry_ref[...]
    o_ref[...] = cs
    carry_ref[...] = cs[tc - 1:tc, :]


def _fox_cumsum(f_raw, b_pad, tc=T_CUMSUM):
    s = f_raw.shape[0]
    tc = min(tc, s)
    return pl.pallas_call(
        functools.partial(_fox_cum_body, tc=tc),
        grid=(s // tc,),
        in_specs=[pl.BlockSpec((tc, LANES), lambda m: (m, 0)), pl.BlockSpec((1, LANES), lambda m: (0, 0))],
        out_specs=pl.BlockSpec((tc, LANES), lambda m: (m, 0)),
        out_shape=jax.ShapeDtypeStruct((s, LANES), F32),
        scratch_shapes=[pltpu.VMEM((1, LANES), F32)],
        compiler_params=_cparams("arbitrary"),
        name="fox_cumsum",
    )(f_raw, b_pad)


def _rope_tables(positions, rot_dim, period):
    half = rot_dim // 2
    inv_freq = ROPE_THETA ** (-jnp.arange(0, rot_dim, 2, dtype=F32) / rot_dim)
    ang = positions.astype(F32)[:, None] * inv_freq
    cos, sin = jnp.cos(ang), jnp.sin(ang)
    lp = jnp.arange(LANES) % period
    in_x1 = lp < half
    in_x2 = (lp >= half) & (lp < 2 * half)
    idx = jnp.where(in_x1, lp, jnp.clip(lp - half, 0, half - 1))
    cos_l, sin_l = cos[:, idx], sin[:, idx]
    c = jnp.where(in_x1 | in_x2, cos_l, 1.0)
    s1 = jnp.where(in_x2, sin_l, 0.0)
    s2 = jnp.where(in_x1, -sin_l, 0.0)
    return c, s1, s2


def _mla_mixer(h, tabs, w_in, q_a_g, kv_a_g, w_q_b, w_kv_b, q_g, k_g):
    s = h.shape[0]
    nope = HEAD_DIM
    scale = MLA_HEAD ** -0.5 * LOG2E
    gains = jnp.concatenate([q_a_g, kv_a_g])
    c_norm = _proj(h, w_in, 0, 0, 2 * MLA_LORA, tn=MLA_LORA, gain=gains, gw=MLA_LORA, real=MLA_LORA)
    kpe = _proj(h, w_in, 0, 2 * MLA_LORA, LANES, tn=LANES, out_dtype=F32,
                epi=functools.partial(_epi_keep_lanes, n_keep=MLA_ROPE))
    wq = w_q_b[0].reshape(MLA_LORA, N_HEADS, MLA_HEAD)
    wq = jnp.concatenate([wq[:, :, MLA_ROPE:], wq[:, :, :MLA_ROPE],
                          jnp.zeros((MLA_LORA, N_HEADS, 2 * LANES - MLA_HEAD), F32)], axis=2)
    wq = wq.reshape(1, MLA_LORA, N_HEADS * 2 * LANES)
    qg = jnp.concatenate([q_g[MLA_ROPE:], q_g[:MLA_ROPE], jnp.zeros((2 * LANES - MLA_HEAD,), F32)])
    q = _proj(c_norm, wq, 0, 0, N_HEADS * 2 * LANES, tn=TN_WIDE, gain=jnp.tile(qg, N_HEADS), tabs=tabs,
              gw=2 * LANES, real=MLA_HEAD, half=MLA_ROPE // 2, rope_groups=lambda g: g % 2 == 1, scale=scale)
    tm, tn = min(TM, s), TN
    aux = [(kpe, (tm, LANES), lambda n, m: (m, 0)),
           (k_g[MLA_ROPE:].reshape(1, nope), (1, nope), lambda n, m: (0, 0)),
           (jnp.pad(k_g[:MLA_ROPE], (0, LANES - MLA_ROPE)).reshape(1, LANES), (1, LANES), lambda n, m: (0, 0))]
    aux += _row_tab_aux(tabs, tm)
    outs = [((s, N_HEADS * 2 * LANES), BF16, (tm, tn), lambda n, m: (m, n)),
            ((s, N_HEADS * HEAD_DIM), BF16, (tm, tn // 2), lambda n, m: (m, n))]
    k, v = _mm(c_norm, [(w_kv_b, 0, 0)], aux, outs, _epi_mla_kv, tm=tm, tn=tn,
               n_tiles=N_HEADS * 2 * LANES // tn, a_blk=1, name="mla_kv")
    return _dense_attention(q, k, v, mode="mla", dq=2 * LANES)


def _dsa_mixer(h, tabs_head, tabs_idx, w_in, q_g, k_g, idx_k_g):
    s = h.shape[0]
    hd = N_HEADS * HEAD_DIM
    ih = IDX_HEADS * IDX_DIM
    half = HEAD_DIM // 8
    wt = jnp.swapaxes(w_in, 1, 2)
    q = _proj(h, wt, 0, 0, hd, tn=TN_WIDE, gain=jnp.tile(q_g, N_HEADS), tabs=tabs_head, half=half,
              scale=HEAD_DIM ** -0.5 * LOG2E, w_t=True)
    k = _proj(h, wt, 0, hd, hd, tn=TN_WIDE, gain=jnp.tile(k_g, N_HEADS), tabs=tabs_head, half=half, w_t=True)
    v = _proj(h, wt, 0, 2 * hd, hd, tn=TN_WIDE, epi=_epi_plain, w_t=True)
    iq = _proj(h, wt, 0, 3 * hd, ih, tn=TN_WIDE, tabs=tabs_idx, half=IDX_DIM // 8, w_t=True)
    assert (3 * hd + ih) % LANES == 0 and w_in.shape[2] - (3 * hd + ih) == IDX_DIM + IDX_HEADS
    tm = min(TM_LONG, s)
    aux = [(jnp.pad(idx_k_g, (0, LANES - IDX_DIM)).reshape(1, LANES), (1, LANES), lambda n, m: (0, 0))]
    aux += _row_tab_aux(tabs_idx, tm)
    outs = [((s, LANES), BF16, (tm, LANES), lambda n, m: (m, 0)), ((s, LANES), F32, (tm, LANES), lambda n, m: (m, 0))]
    ik2, iw = _mm(h, [(wt, 0, (3 * hd + ih) // LANES)], aux, outs, _epi_index_key_and_weights, tm=tm, tn=LANES,
                  n_tiles=1, name="proj", w_t=True)
    bias = _dsa_select_bias(iq, ik2, iw, min(IDX_TOPK, s // 4))
    return _dsa_attention(q, k, v, bias)


def _diff_mixer(h, tabs, layer_idx, w_in, q_g, k_g, lam_params, subln_g):
    w = N_HEADS * 2 * DIFF_DIM
    half = DIFF_DIM // 8
    q = _proj(h, w_in, 0, 0, w, tn=TN_WIDE, gain=jnp.tile(q_g, 2 * N_HEADS), tabs=tabs, gw=DIFF_DIM, real=DIFF_DIM,
              half=half, scale=DIFF_DIM ** -0.5 * LOG2E)
    k = _proj(h, w_in, 0, w, w, tn=TN_WIDE, gain=jnp.tile(k_g, 2 * N_HEADS), tabs=tabs, gw=DIFF_DIM, real=DIFF_DIM,
              half=half)
    v = _proj(h, w_in, 0, 2 * w, w, tn=TN_WIDE, epi=_epi_plain)
    lam_init = 0.8 - 0.6 * math.exp(-0.3 * layer_idx)
    return _dense_attention(q, k, v, mode="diff", dq=HEAD_DIM, extra=(lam_params, subln_g.reshape(1, HEAD_DIM)),
                            lam_init=lam_init)


def _fox_mixer(h, w_in, b_f, q_g, k_g):
    s = h.shape[0]
    hd = N_HEADS * HEAD_DIM
    wt = jnp.swapaxes(w_in, 1, 2)
    q = _proj(h, wt, 0, 0, hd, tn=TN_WIDE, gain=jnp.tile(q_g, N_HEADS), scale=HEAD_DIM ** -0.5 * LOG2E, w_t=True)
    k = _proj(h, wt, 0, hd, hd, tn=TN_WIDE, gain=jnp.tile(k_g, N_HEADS), w_t=True)
    v = _proj(h, wt, 0, 2 * hd, hd, tn=TN_WIDE, epi=_epi_plain, w_t=True)
    f_raw = _proj(h, wt, 0, 3 * hd, LANES, tn=LANES, out_dtype=F32,
                  epi=functools.partial(_epi_keep_lanes, n_keep=N_HEADS), w_t=True)
    gate_blk = 3 * hd // TN
    tm = min(TM, s)
    gate = _mm(h, [(wt, 0, gate_blk), (wt, 0, gate_blk + 1)], [],
               [((s, hd), F32, (tm, TN), lambda n, m: (m, n))], _epi_sigmoid, tm=tm, tn=TN, n_tiles=hd // TN,
               name="proj", lane_shift=N_HEADS, w_t=True)[0]
    cum = _fox_cumsum(f_raw, jnp.pad(b_f, (0, LANES - N_HEADS)).reshape(1, LANES))
    cumt = cum[:, :N_HEADS].T.reshape(N_HEADS, 1, s)
    return _dense_attention(q, k, v, mode="fox", dq=HEAD_DIM, extra=(cum, cumt, gate))


def _swiglu_ffn(h, x, gate, w_gate_up, w_down, layer):
    s = h.shape[0]
    tm, tn = min(TM_LONG, s), TN
    outs = [((s, FFN_HIDDEN), BF16, (tm, tn), lambda n, m: (m, n))]
    act = _mm(h, [(w_gate_up, layer, 0), (w_gate_up, layer, FFN_HIDDEN // tn)], [], outs, _epi_swiglu,
              tm=tm, tn=tn, n_tiles=FFN_HIDDEN // tn, name="ffn_gate_up")[0]
    return _residual_mm(act, w_down, layer, x, gate, tm=TM_DEEP)


def kernel(x, c, positions, ln_mix_g, ln_ffn_g, ada_w, ada_b, ffn_w_gate_up, ffn_w_down, mla_w_in, mla_q_a_g, mla_kv_a_g, mla_w_q_b, mla_w_kv_b, mla_q_g, mla_k_g, mla_w_out, dsa_w_in, dsa_q_g, dsa_k_g, dsa_idx_k_g, dsa_w_out, diff_w_in, diff_q_g, diff_k_g, diff_lambda_q1, diff_lambda_k1, diff_lambda_q2, diff_lambda_k2, diff_subln_g, diff_w_out, fox_w_in, fox_b_f, fox_q_g, fox_k_g, fox_w_out):
    batch, s, d = x.shape
    assert batch == 1 and d == D_MODEL
    depth = ada_w.shape[0]
    pos = positions[0]
    tabs_head = _rope_tables(pos, HEAD_DIM // 4, LANES)
    tabs_small = _rope_tables(pos, IDX_DIM // 4, IDX_DIM)
    tabs_mla = _rope_tables(pos, MLA_ROPE, LANES)
    mod = _ada_mod(c, ada_w, ada_b)
    xs = x[0]
    for i in range(depth):
        sh1, sc1, g1, sh2, sc2, g2 = [mod[i, t * d:(t + 1) * d] for t in range(6)]
        h = _normmod(xs, ln_mix_g[i], sc1, sh1)
        kind, j = i % 4, i // 4
        if kind == 0:
            o = _mla_mixer(h, tabs_mla, mla_w_in[j:j + 1], mla_q_a_g[j], mla_kv_a_g[j], mla_w_q_b[j:j + 1],
                           mla_w_kv_b[j:j + 1], mla_q_g[j], mla_k_g[j])
            w_out = mla_w_out
        elif kind == 1:
            o = _dsa_mixer(h, tabs_head, tabs_small, dsa_w_in[j:j + 1], dsa_q_g[j], dsa_k_g[j], dsa_idx_k_g[j])
            w_out = dsa_w_out
        elif kind == 2:
            lam_params = jnp.stack([diff_lambda_q1[j], diff_lambda_k1[j], diff_lambda_q2[j], diff_lambda_k2[j]])
            o = _diff_mixer(h, tabs_small, i, diff_w_in[j:j + 1], diff_q_g[j], diff_k_g[j], lam_params,
                            diff_subln_g[j])
            w_out = diff_w_out
        else:
            o = _fox_mixer(h, fox_w_in[j:j + 1], fox_b_f[j], fox_q_g[j], fox_k_g[j])
            w_out = fox_w_out
        xs = _residual_mm(o, w_out, j, xs, g1, tn=TN_WIDE)
        h = _normmod(xs, ln_ffn_g[i], sc2, sh2)
        xs = _swiglu_ffn(h, xs, g2, ffn_w_gate_up, ffn_w_down, i)
    return xs[None]
```

```python
import functools
import math

import jax
import jax.numpy as jnp
from jax import lax
from jax.experimental import pallas as pl
from jax.experimental.pallas import tpu as pltpu

F32 = jnp.float32
BF16 = jnp.bfloat16

D_MODEL = 2048
N_HEADS = 16
HEAD_DIM = 128
ROPE_THETA = 500000.0
RMS_EPS = 1e-6
FFN_HIDDEN = 5632
MLA_LORA = 512
MLA_ROPE = 64
MLA_HEAD = 192
IDX_HEADS = 16
IDX_DIM = 64
IDX_TOPK = 256
DIFF_DIM = 64

LANES = 128
LOG2E = 1.4426950408889634
NEG = -1e30
VMEM_LIMIT_BYTES = 56 * 1024 * 1024
TM = 1024
TM_LONG = 2048
TM_DEEP = 512
SUB_ROWS = 256
TN = 512
TN_WIDE = 1024
TN_ADA = 1536
TQ_ATTN = 512
TQ_SELECT = 256
T_CUMSUM = 512
HEADS_PER_STEP = 2
BISECT_ROWS = 128
BISECT_UNCHECKED = 16
INT_MIN = -2 ** 31
KEY_NEG_INF = (0xFF800000 ^ 0x7FFFFFFF) - 2 ** 32


def _cparams(*sem):
    return pltpu.CompilerParams(dimension_semantics=sem, vmem_limit_bytes=VMEM_LIMIT_BYTES)


def _dot_nt(a, b):
    return lax.dot_general(a, b, (((1,), (1,)), ((), ())), preferred_element_type=F32)


def _sigmoid(z):
    return 1.0 / (1.0 + jnp.exp(-z))


def _mm_body(*refs, n_w, n_aux, n_out, epi, sub, lane_shift, w_t):
    a_ref = refs[0]
    w_refs = refs[1:1 + n_w]
    aux_refs = refs[1 + n_w:1 + n_w + n_aux]
    out_refs = refs[1 + n_w + n_aux:1 + n_w + n_aux + n_out]
    wb_refs = refs[1 + n_w + n_aux + n_out:]
    tm = a_ref.shape[0]

    @pl.when(pl.program_id(1) == 0)
    def _cast_weights():
        if lane_shift:
            w0, w1 = w_refs
            if w_t:
                shifted = jnp.concatenate([w0[lane_shift:, :], w1[:lane_shift, :]], axis=0)
            else:
                shifted = jnp.concatenate([w0[:, lane_shift:], w1[:, :lane_shift]], axis=1)
            wb_refs[0][...] = shifted.astype(BF16)
            return
        for w_ref, wb_ref in zip(w_refs, wb_refs):
            wb_ref[...] = w_ref[...].astype(BF16)

    def product(a, wb):
        return _dot_nt(a, wb) if w_t else jnp.dot(a, wb, preferred_element_type=F32)

    for r in range(tm // sub):
        rows = pl.ds(r * sub, sub)
        accs = [product(a_ref[rows, :], wb_ref[...]) for wb_ref in wb_refs]
        epi(accs, [ref.at[rows] if ref.shape[0] == tm else ref for ref in aux_refs],
            [ref.at[rows] for ref in out_refs])


def _mm(a, ws, aux, outs, epi, *, tm, tn, n_tiles, a_blk=0, name="mm", sub=SUB_ROWS, lane_shift=0, w_t=False):
    s = a.shape[0]
    k = ws[0][0].shape[2 if w_t else 1]
    tm = min(tm, s)
    in_specs = [pl.BlockSpec((tm, k), lambda n, m: (m, a_blk))]
    operands = [a]
    for w, layer, off in ws:
        if w_t:
            in_specs.append(pl.BlockSpec((None, tn, k), lambda n, m, layer=layer, off=off: (layer, n + off, 0)))
        else:
            in_specs.append(pl.BlockSpec((None, k, tn), lambda n, m, layer=layer, off=off: (layer, 0, n + off)))
        operands.append(w)
    for arr, block, imap in aux:
        in_specs.append(pl.BlockSpec(block, imap))
        operands.append(arr)
    out_shape = [jax.ShapeDtypeStruct(shape, dtype) for shape, dtype, _, _ in outs]
    out_specs = [pl.BlockSpec(block, imap) for _, _, block, imap in outs]
    body = functools.partial(_mm_body, n_w=len(ws), n_aux=len(aux), n_out=len(outs), epi=epi, sub=min(sub, tm),
                             lane_shift=lane_shift, w_t=w_t)
    res = pl.pallas_call(
        body,
        grid=(n_tiles, s // tm),
        in_specs=in_specs,
        out_specs=out_specs,
        out_shape=out_shape,
        scratch_shapes=[pltpu.VMEM((tn, k) if w_t else (k, tn), BF16) for _ in (ws[:1] if lane_shift else ws)],
        compiler_params=_cparams("arbitrary", "arbitrary"),
        name=name,
    )(*operands)
    return res


def _rope_group(y, tabs, half):
    c, s1, s2 = tabs
    return y * c + pltpu.roll(y, half, 1) * s1 + pltpu.roll(y, LANES - half, 1) * s2


def _group_inv_rms(y, gw, real):
    tn = y.shape[1]
    sq = y * y
    n_groups = tn // LANES
    if gw == DIFF_DIM:
        lane = lax.broadcasted_iota(jnp.int32, (1, LANES), 1)
        lo = lane < DIFF_DIM
        out = []
        for g in range(n_groups):
            sg = sq[:, g * LANES:(g + 1) * LANES]
            s_lo = jnp.sum(jnp.where(lo, sg, 0.0), axis=1, keepdims=True)
            s_hi = jnp.sum(jnp.where(lo, 0.0, sg), axis=1, keepdims=True)
            out.append(lax.rsqrt(jnp.where(lo, s_lo, s_hi) * (1.0 / real) + RMS_EPS))
        return out
    per = gw // LANES
    out = []
    for h in range(n_groups // per):
        folded = sq[:, h * gw:h * gw + LANES]
        for t in range(1, per):
            folded = folded + sq[:, h * gw + t * LANES:h * gw + (t + 1) * LANES]
        r = lax.rsqrt(jnp.sum(folded, axis=1, keepdims=True) * (1.0 / real) + RMS_EPS)
        out.extend([r] * per)
    return out


def _epi_heads(accs, aux_refs, out_refs, *, gw, real, half, rope_groups, scale, use_gain, use_rope):
    y = accs[0]
    tn = y.shape[1]
    idx = 0
    gain = None
    if use_gain:
        gain = aux_refs[idx][...]
        idx += 1
        inv = _group_inv_rms(y, gw, real)
    if use_rope:
        tabs = tuple(aux_refs[idx + t][...] for t in range(3))
    for g in range(tn // LANES):
        yg = y[:, g * LANES:(g + 1) * LANES]
        if use_gain:
            yg = yg * inv[g] * gain[:, g * LANES:(g + 1) * LANES]
        if use_rope and rope_groups(g):
            yg = _rope_group(yg, tabs, half)
        if scale != 1.0:
            yg = yg * scale
        out_refs[0][:, g * LANES:(g + 1) * LANES] = yg.astype(out_refs[0].dtype)


def _epi_plain(accs, aux_refs, out_refs):
    out_refs[0][...] = accs[0].astype(out_refs[0].dtype)


def _epi_sigmoid(accs, aux_refs, out_refs):
    out_refs[0][...] = _sigmoid(accs[0]).astype(out_refs[0].dtype)


def _epi_residual(accs, aux_refs, out_refs):
    x_ref, g_ref = aux_refs
    out_refs[0][...] = x_ref[...] + g_ref[...] * accs[0]


def _epi_swiglu(accs, aux_refs, out_refs):
    gate, up = accs
    out_refs[0][...] = (gate * _sigmoid(gate) * up).astype(out_refs[0].dtype)


def _epi_mla_kv(accs, aux_refs, out_refs):
    kpe_ref, g_nope_ref, g_pe_ref, c_ref, s1_ref, s2_ref = aux_refs
    k_ref, v_ref = out_refs
    y = accs[0]
    kpe = kpe_ref[...]
    ss_pe = jnp.sum(kpe * kpe, axis=1, keepdims=True)
    tabs = (c_ref[...], s1_ref[...], s2_ref[...])
    g_nope = g_nope_ref[...]
    pe_roped = _rope_group(kpe * g_pe_ref[...], tabs, MLA_ROPE // 2)
    for h in range(y.shape[1] // (2 * LANES)):
        kn = y[:, 2 * h * LANES:(2 * h + 1) * LANES]
        v = y[:, (2 * h + 1) * LANES:(2 * h + 2) * LANES]
        r = lax.rsqrt((jnp.sum(kn * kn, axis=1, keepdims=True) + ss_pe) * (1.0 / MLA_HEAD) + RMS_EPS)
        k_ref[:, 2 * h * LANES:(2 * h + 1) * LANES] = (kn * r * g_nope).astype(k_ref.dtype)
        k_ref[:, (2 * h + 1) * LANES:(2 * h + 2) * LANES] = (pe_roped * r).astype(k_ref.dtype)
        v_ref[:, h * LANES:(h + 1) * LANES] = v.astype(v_ref.dtype)


def _epi_index_key_and_weights(accs, aux_refs, out_refs):
    g_ref, c_ref, s1_ref, s2_ref = aux_refs
    ik2_ref, iw_ref = out_refs
    y = accs[0]
    lane = lax.broadcasted_iota(jnp.int32, (1, LANES), 1)
    yk = jnp.where(lane < IDX_DIM, y, 0.0)
    inv = lax.rsqrt(jnp.sum(yk * yk, axis=1, keepdims=True) * (1.0 / IDX_DIM) + RMS_EPS)
    key = _rope_group(yk * inv * g_ref[...], (c_ref[...], s1_ref[...], s2_ref[...]), IDX_DIM // 8)
    ik2_ref[...] = (key + pltpu.roll(key, IDX_DIM, 1)).astype(ik2_ref.dtype)
    weights = pltpu.roll(y, LANES - IDX_DIM, 1) * (IDX_HEADS ** -0.5 * IDX_DIM ** -0.5)
    iw_ref[...] = jnp.where(lane < IDX_HEADS, weights, 0.0)


def _epi_keep_lanes(accs, aux_refs, out_refs, *, n_keep):
    lane = lax.broadcasted_iota(jnp.int32, (1, LANES), 1)
    out_refs[0][...] = jnp.where(lane < n_keep, accs[0], 0.0).astype(out_refs[0].dtype)


def _row_tab_aux(tabs, tm):
    return [(t, (tm, LANES), lambda n, m: (m, 0)) for t in tabs]


def _proj(a, w, layer, col_off, width, *, tn, out_dtype=BF16, gain=None, tabs=None, gw=LANES, real=LANES,
          half=0, rope_groups=lambda g: True, scale=1.0, epi=None, tm=TM_LONG, a_blk=0, w_t=False):
    s = a.shape[0]
    tm = min(tm, s)
    assert col_off % tn == 0 and width % tn == 0
    aux = []
    if gain is not None:
        aux.append((gain.reshape(1, width).astype(F32), (1, tn), lambda n, m: (0, n)))
    if tabs is not None:
        aux.extend(_row_tab_aux(tabs, tm))
    if epi is None:
        epi = functools.partial(_epi_heads, gw=gw, real=real, half=half, rope_groups=rope_groups, scale=scale,
                                use_gain=gain is not None, use_rope=tabs is not None)
    outs = [((s, width), out_dtype, (tm, tn), lambda n, m: (m, n))]
    return _mm(a, [(w, layer, col_off // tn)], aux, outs, epi, tm=tm, tn=tn, n_tiles=width // tn, a_blk=a_blk,
               name="proj", w_t=w_t)[0]


def _residual_mm(a, w, layer, x, gate, *, tn=TN, tm=TM):
    s, d = x.shape
    tm = min(tm, s)
    aux = [(x, (tm, tn), lambda n, m: (m, n)), (gate.reshape(1, d), (1, tn), lambda n, m: (0, n))]
    outs = [((s, d), F32, (tm, tn), lambda n, m: (m, n))]
    return _mm(a, [(w, layer, 0)], aux, outs, _epi_residual, tm=tm, tn=tn, n_tiles=d // tn, name="residual_mm")[0]


def _ada_body(c_ref, w_ref, b_ref, o_ref):
    c = c_ref[...]
    cond = c * _sigmoid(c)
    o_ref[...] = jnp.dot(cond.astype(BF16), w_ref[...].astype(BF16), preferred_element_type=F32) + b_ref[...]


def _ada_mod(c, ada_w, ada_b):
    depth, d, n = ada_w.shape
    tn = TN_ADA
    c8 = jnp.broadcast_to(c, (8, d))
    out = pl.pallas_call(
        _ada_body,
        grid=(depth, n // tn),
        in_specs=[pl.BlockSpec((8, d), lambda l, j: (0, 0)),
                  pl.BlockSpec((None, d, tn), lambda l, j: (l, 0, j)),
                  pl.BlockSpec((None, 1, tn), lambda l, j: (l, 0, j))],
        out_specs=pl.BlockSpec((None, 8, tn), lambda l, j: (l, 0, j)),
        out_shape=jax.ShapeDtypeStruct((depth, 8, n), F32),
        compiler_params=_cparams("arbitrary", "arbitrary"),
        name="ada_mod",
    )(c8, ada_w, ada_b.reshape(depth, 1, n))
    return out[:, 0, :]


def _normmod_body(x_ref, g_ref, sc_ref, sh_ref, o_ref):
    x = x_ref[...]
    y = x * lax.rsqrt(jnp.mean(x * x, axis=1, keepdims=True) + RMS_EPS) * g_ref[...]
    o_ref[...] = (y * (1.0 + sc_ref[...]) + sh_ref[...]).astype(o_ref.dtype)


def _normmod(x, g, sc, sh, tm=TM):
    s, d = x.shape
    tm = min(tm, s)
    vec = pl.BlockSpec((1, d), lambda m: (0, 0))
    return pl.pallas_call(
        _normmod_body,
        grid=(s // tm,),
        in_specs=[pl.BlockSpec((tm, d), lambda m: (m, 0)), vec, vec, vec],
        out_specs=pl.BlockSpec((tm, d), lambda m: (m, 0)),
        out_shape=jax.ShapeDtypeStruct((s, d), BF16),
        compiler_params=_cparams("arbitrary"),
        name="normmod",
    )(x, g.reshape(1, d), sc.reshape(1, d), sh.reshape(1, d))


def _softmax_scratch(n_maps, tq, tk):
    return [pltpu.VMEM((n_maps, tq, 2 * HEAD_DIM), F32), pltpu.VMEM((n_maps, tq, LANES), F32),
            pltpu.VMEM((n_maps, 2, tq, tk), F32), pltpu.VMEM((n_maps, 2, 2, tq, LANES), F32)]


def _softmax_pipeline(qs, kv_cols, k_ref, v_ref, scratch, *, tq, tk, first_chunk, first_keep, n_rest, bias_fn):
    acc_ref, m_ref, s_ref, st_ref = scratch
    ones = jnp.ones((tk, LANES), BF16)
    n_grp = tk // LANES
    maps = range(len(qs))

    def chunk_start(c):
        return pl.multiple_of(c * tk, tk)

    def stage1(c, slot, keep):
        start = chunk_start(c)
        for t in maps:
            s = _dot_nt(qs[t], k_ref[pl.ds(start, tk), kv_cols[t][0]])
            if bias_fn is not None:
                s = s + bias_fn(t, start)
            if keep is not None:
                s = jnp.where(keep, s, NEG)
            m_run = m_ref[t]
            m_new = jnp.maximum(m_run, jnp.max(s, axis=1, keepdims=True))
            s_ref[t, slot] = s
            st_ref[t, slot, 0] = m_new
            st_ref[t, slot, 1] = jnp.exp2(m_run - m_new)
            m_ref[t] = m_new

    def stage2(c, slot):
        start = chunk_start(c)
        for t in maps:
            v1 = jnp.concatenate([v_ref[pl.ds(start, tk), kv_cols[t][1]], ones], axis=1)
            m_new = st_ref[t, slot, 0]
            alpha = st_ref[t, slot, 1]
            p = jnp.concatenate([jnp.exp2(s_ref[t, slot, :, g * LANES:(g + 1) * LANES] - m_new)
                                 for g in range(n_grp)], axis=1)
            pv = jnp.dot(p.astype(BF16), v1, preferred_element_type=F32)
            acc_ref[t] = jnp.concatenate([alpha, alpha], axis=1) * acc_ref[t] + pv

    acc_ref[...] = jnp.zeros(acc_ref.shape, F32)
    m_ref[...] = jnp.full(m_ref.shape, NEG, F32)
    stage1(first_chunk, 0, first_keep)

    def two_chunks(c0):
        stage1(c0, 1, None)
        stage2(jnp.where(c0 == 0, first_chunk, c0 - 1), 0)
        stage1(c0 + 1, 0, None)
        stage2(c0, 1)

    def body(t4, carry):
        two_chunks(4 * t4)
        two_chunks(4 * t4 + 2)
        return carry

    n_quads = n_rest // 4
    lax.fori_loop(0, n_quads, body, 0)
    n_pairs = n_rest // 2

    @pl.when(n_pairs > 2 * n_quads)
    def _pair_tail():
        two_chunks(4 * n_quads)

    pending = jnp.where(n_pairs == 0, first_chunk, 2 * n_pairs - 1)

    @pl.when(n_rest % 2 == 1)
    def _odd_tail():
        stage1(n_rest - 1, 1, None)
        stage2(pending, 0)
        stage2(n_rest - 1, 1)

    @pl.when(n_rest % 2 == 0)
    def _even_tail():
        stage2(pending, 0)


def _causal_keep(tq, tk, col_shift):
    row = lax.broadcasted_iota(jnp.int32, (tq, 1), 0)
    col = lax.broadcasted_iota(jnp.int32, (1, tk), 1)
    return row >= col + col_shift


def _dense_attn_body(*refs, mode, tq, tk, lam_init):
    if mode == "mla":
        q_ref, k_ref, v_ref, o_ref = refs[:4]
        scratch = refs[4:]
    elif mode == "fox":
        q_ref, k_ref, v_ref, cum_ref, cumt_ref, gate_ref, o_ref = refs[:7]
        scratch = refs[7:]
    else:
        q_ref, k_ref, v_ref, lam_ref, subg_ref, o_ref = refs[:6]
        scratch = refs[6:]
    hg = pl.program_id(0)
    i = pl.program_id(1)
    acc_ref = scratch[0]
    dq = q_ref.shape[1] // HEADS_PER_STEP
    heads = range(HEADS_PER_STEP)
    lane = lax.broadcasted_iota(jnp.int32, (1, LANES), 1)

    def head_cols(hh, width=HEAD_DIM):
        return slice(hh * width, (hh + 1) * width)

    qs, kv_cols = [], []
    for hh in heads:
        q = q_ref[:, head_cols(hh, dq)]
        if mode == "diff":
            zero = jnp.zeros_like(q)
            qs += [jnp.where(lane < DIFF_DIM, q, zero), jnp.where(lane >= DIFF_DIM, q, zero)]
            kv_cols += [(head_cols(hh, dq), head_cols(hh))] * 2
        else:
            qs.append(q)
            kv_cols.append((head_cols(hh, dq), head_cols(hh)))
    if mode == "fox":
        cum = cum_ref[...]
        cqs = [jnp.sum(jnp.where(lane == hg * HEADS_PER_STEP + hh, cum, 0.0), axis=1, keepdims=True) * LOG2E
               for hh in heads]

        def bias_fn(t, start):
            return cqs[t] - cumt_ref[t, :, pl.ds(start, tk)] * LOG2E
    else:
        bias_fn = None

    _softmax_pipeline(qs, kv_cols, k_ref, v_ref, scratch, tq=tq, tk=tk, first_chunk=i,
                      first_keep=_causal_keep(tq, tk, 0), n_rest=i, bias_fn=bias_fn)

    def normalised(t):
        acc = acc_ref[t]
        return acc[:, :HEAD_DIM] / acc[:, HEAD_DIM:]

    for hh in heads:
        if mode == "diff":
            lp = lam_ref[...]
            lam = (jnp.exp(jnp.sum(lp[0:1] * lp[1:2], axis=1, keepdims=True))
                   - jnp.exp(jnp.sum(lp[2:3] * lp[3:4], axis=1, keepdims=True)) + lam_init)
            o = normalised(2 * hh) - lam * normalised(2 * hh + 1)
            o = o * lax.rsqrt(jnp.mean(o * o, axis=1, keepdims=True) + RMS_EPS) * subg_ref[...]
            o = o * (1.0 - lam_init)
        else:
            o = normalised(hh)
            if mode == "fox":
                o = o * gate_ref[:, head_cols(hh)]
        o_ref[:, head_cols(hh)] = o.astype(o_ref.dtype)


def _dense_attention(q, k, v, *, mode, dq, extra=(), lam_init=0.0, tq=TQ_ATTN):
    s = q.shape[0]
    tq = min(tq, s)
    tk = tq
    assert s % tq == 0 and N_HEADS % HEADS_PER_STEP == 0
    g = HEADS_PER_STEP
    in_specs = [pl.BlockSpec((tq, g * dq), lambda h, i: (i, h)),
                pl.BlockSpec((s, g * dq), lambda h, i: (0, h)),
                pl.BlockSpec((s, g * HEAD_DIM), lambda h, i: (0, h))]
    operands = [q, k, v]
    if mode == "fox":
        cum, cumt, gate = extra
        in_specs += [pl.BlockSpec((tq, LANES), lambda h, i: (i, 0)),
                     pl.BlockSpec((g, 1, s), lambda h, i: (h, 0, 0)),
                     pl.BlockSpec((tq, g * HEAD_DIM), lambda h, i: (i, h))]
        operands += [cum, cumt, gate]
    elif mode == "diff":
        lam_params, subln_g = extra
        in_specs += [pl.BlockSpec((4, DIFF_DIM), lambda h, i: (0, 0)),
                     pl.BlockSpec((1, HEAD_DIM), lambda h, i: (0, 0))]
        operands += [lam_params, subln_g]
    n_maps = g * (2 if mode == "diff" else 1)
    body = functools.partial(_dense_attn_body, mode=mode, tq=tq, tk=tk, lam_init=lam_init)
    return pl.pallas_call(
        body,
        grid=(N_HEADS // g, s // tq),
        in_specs=in_specs,
        out_specs=pl.BlockSpec((tq, g * HEAD_DIM), lambda h, i: (i, h)),
        out_shape=jax.ShapeDtypeStruct((s, N_HEADS * HEAD_DIM), BF16),
        scratch_shapes=_softmax_scratch(n_maps, tq, tk),
        compiler_params=_cparams("arbitrary", "arbitrary"),
        name="attn_" + mode,
    )(*operands)


def _dsa_attn_body(q_ref, k_ref, v_ref, bias_ref, o_ref, *scratch, tq, tk):
    i = pl.program_id(0)

    def bias_fn(t, start):
        return bias_ref[:, pl.ds(start, tk)].astype(F32)

    cols = [slice(hh * HEAD_DIM, (hh + 1) * HEAD_DIM) for hh in range(HEADS_PER_STEP)]
    _softmax_pipeline([q_ref[:, c] for c in cols], [(c, c) for c in cols], k_ref, v_ref, scratch, tq=tq, tk=tk,
                      first_chunk=i, first_keep=None, n_rest=i, bias_fn=bias_fn)
    for t, c in enumerate(cols):
        acc = scratch[0][t]
        o_ref[:, c] = (acc[:, :HEAD_DIM] / acc[:, HEAD_DIM:]).astype(o_ref.dtype)


def _dsa_attention(q, k, v, bias, *, tq=TQ_ATTN):
    s = q.shape[0]
    tq = min(tq, s)
    tk = tq
    assert s % tq == 0 and N_HEADS % HEADS_PER_STEP == 0
    body = functools.partial(_dsa_attn_body, tq=tq, tk=tk)
    return pl.pallas_call(
        body,
        grid=(s // tq, N_HEADS // HEADS_PER_STEP),
        in_specs=[pl.BlockSpec((tq, HEADS_PER_STEP * HEAD_DIM), lambda i, h: (i, h)),
                  pl.BlockSpec((s, HEADS_PER_STEP * HEAD_DIM), lambda i, h: (0, h)),
                  pl.BlockSpec((s, HEADS_PER_STEP * HEAD_DIM), lambda i, h: (0, h)),
                  pl.BlockSpec((tq, s), lambda i, h: (i, 0))],
        out_specs=pl.BlockSpec((tq, HEADS_PER_STEP * HEAD_DIM), lambda i, h: (i, h)),
        out_shape=jax.ShapeDtypeStruct((s, N_HEADS * HEAD_DIM), BF16),
        scratch_shapes=_softmax_scratch(HEADS_PER_STEP, tq, tk),
        compiler_params=_cparams("arbitrary", "arbitrary"),
        name="attn_dsa",
    )(q, k, v, bias)


def _sortable_key(score):
    bits = lax.bitcast_convert_type(score + 0.0, jnp.int32)
    return bits ^ ((bits >> 31) & 0x7FFFFFFF)


def _indexer_body(iq_ref, ik_ref, iw_ref, bias_ref, keys_ref, qz_ref, wb_ref, thr_ref, need_ref, *, tq, n_sel):
    i = pl.program_id(0)
    s_len = ik_ref.shape[0]
    n_chunks = s_len // tq
    lane = lax.broadcasted_iota(jnp.int32, (1, LANES), 1)
    iw = iw_ref[...]
    for p in range(IDX_HEADS // 2):
        pair = iq_ref[:, p * LANES:(p + 1) * LANES]
        zero = jnp.zeros_like(pair)
        qz_ref[2 * p] = jnp.where(lane < IDX_DIM, pair, zero)
        qz_ref[2 * p + 1] = jnp.where(lane >= IDX_DIM, pair, zero)
    for hh in range(IDX_HEADS):
        wb_ref[hh] = jnp.broadcast_to(iw[:, hh:hh + 1], (tq, LANES))

    def score_chunk(start):
        ik = ik_ref[pl.ds(start, tq), :]
        sc = jnp.zeros((tq, tq), F32)
        for hh in range(IDX_HEADS):
            rel = jnp.maximum(_dot_nt(qz_ref[hh], ik), 0.0)
            sc = sc + jnp.tile(wb_ref[hh], (1, tq // LANES)) * rel
        return sc

    def full_body(j, carry):
        start = pl.multiple_of(j * tq, tq)
        keys_ref[:, pl.ds(start, tq)] = _sortable_key(score_chunk(start))
        return carry

    lax.fori_loop(0, i, full_body, 0)
    diag = pl.multiple_of(i * tq, tq)
    sc = jnp.where(_causal_keep(tq, tq, 0), score_chunk(diag), -jnp.inf)
    keys_ref[:, pl.ds(diag, tq)] = _sortable_key(sc)

    keys_ref[:, pl.ds(pl.multiple_of((i + 1) * tq, tq), tq)] = jnp.full((tq, tq), KEY_NEG_INF, jnp.int32)
    strips = [pl.ds(r0, BISECT_ROWS) for r0 in range(0, tq, BISECT_ROWS)]
    n_trips = (i + 2) // 2
    lanes_per_trip = 2 * tq // LANES

    def count_keys(rows, trial, strict):
        def count_body(j, cnt):
            start = pl.multiple_of(j * (2 * tq), 2 * tq)
            for g in range(lanes_per_trip):
                kk = keys_ref[rows, pl.ds(start + g * LANES, LANES)]
                cnt = cnt + ((kk > trial) if strict else (kk >= trial)).astype(jnp.int32)
            return cnt

        return lax.fori_loop(0, n_trips, count_body, jnp.zeros((BISECT_ROWS, LANES), jnp.int32))

    def key_bounds(rows):
        def max_body(j, gm):
            start = pl.multiple_of(j * (2 * tq), 2 * tq)
            gm = list(gm)
            for g in range(lanes_per_trip):
                gm[g % 2] = jnp.maximum(gm[g % 2], keys_ref[rows, pl.ds(start + g * LANES, LANES)])
            return tuple(gm)

        init = jnp.full((BISECT_ROWS, LANES), KEY_NEG_INF, jnp.int32)
        g0, g1 = lax.fori_loop(0, n_trips, max_body, (init, init))
        lo = jnp.min(jnp.minimum(g0, g1), axis=1, keepdims=True)
        hi = jnp.max(jnp.maximum(g0, g1), axis=1, keepdims=True)
        lo = jnp.maximum(lo, KEY_NEG_INF + 1) + jnp.zeros((BISECT_ROWS, LANES), jnp.int32)
        return lo, jnp.maximum(hi, lo) + 1

    def open_rows(lo, hi):
        width = hi - lo
        return ((width != 0) & (width != 1)).astype(jnp.int32)

    def halve(bounds):
        mids = [lo + lax.shift_right_logical(hi - lo, 1) for lo, hi in bounds]
        counts = [count_keys(rows, mid, False) for rows, mid in zip(strips, mids)]
        out = []
        for cnt, mid, (lo, hi) in zip(counts, mids, bounds):
            total = jnp.dot(cnt.astype(F32).astype(BF16), jnp.ones((LANES, LANES), BF16), preferred_element_type=F32)
            enough = total >= n_sel
            out.append((jnp.where(enough, mid, lo), jnp.where(total == n_sel, mid + 1, jnp.where(enough, hi, mid))))
        return out

    def flat(bounds):
        return tuple(b for pair in bounds for b in pair)

    def pairs(flat_bounds):
        return [(flat_bounds[2 * t], flat_bounds[2 * t + 1]) for t in range(len(strips))]

    bounds = lax.fori_loop(0, BISECT_UNCHECKED, lambda _, fb: flat(halve(pairs(fb))),
                           flat([key_bounds(rows) for rows in strips]))

    def still_open(state):
        it, fb = state[0], state[1:]
        n_open = functools.reduce(jnp.maximum, [open_rows(lo, hi) for lo, hi in pairs(fb)])
        return (it < 32) & (jnp.max(n_open) > 0)

    bounds = lax.while_loop(still_open, lambda state: (state[0] + 1,) + flat(halve(pairs(state[1:]))),
                            (jnp.int32(0),) + bounds)[1:]

    surplus = jnp.zeros((BISECT_ROWS, LANES), jnp.int32)
    for rows, (thr, _) in zip(strips, pairs(bounds)):
        above = jnp.sum(count_keys(rows, thr, True), axis=1, keepdims=True)
        tied = jnp.sum(count_keys(rows, thr, False), axis=1, keepdims=True) - above
        need = n_sel - above
        thr_ref[rows, :] = thr
        need_ref[rows, :] = need.astype(F32) + jnp.zeros((BISECT_ROWS, LANES), F32)
        surplus = jnp.maximum(surplus, tied - need)
    any_surplus = jnp.max(surplus) > 0
    n_grp = tq // LANES

    @pl.when(jnp.logical_not(any_surplus))
    def _write_no_ties():
        def write_body(j, carry):
            start = pl.multiple_of(j * tq, tq)
            thr = thr_ref[...]
            for g in range(n_grp):
                cols = pl.ds(start + g * LANES, LANES)
                bias_ref[:, cols] = jnp.where(keys_ref[:, cols] >= thr, 0.0, NEG).astype(bias_ref.dtype)
            return carry

        lax.fori_loop(0, i + 1, write_body, 0)

    @pl.when(any_surplus)
    def _write_with_ties():
        ca = lax.broadcasted_iota(jnp.int32, (tq, tq), 0)
        cb = lax.broadcasted_iota(jnp.int32, (tq, tq), 1)
        prefix_ones = (ca <= cb).astype(BF16)
        all_ones = jnp.ones((tq, LANES), BF16)

        def write_body(j, seen):
            start = pl.multiple_of(j * tq, tq)
            kk = keys_ref[:, pl.ds(start, tq)]
            thr = jnp.concatenate([thr_ref[...]] * n_grp, axis=1)
            eq = kk == thr
            eq_b = jnp.where(eq, 1.0, 0.0).astype(BF16)
            rank = jnp.dot(eq_b, prefix_ones, preferred_element_type=F32) + jnp.concatenate([seen] * n_grp, axis=1)
            take = (kk > thr) | (eq & (rank <= jnp.concatenate([need_ref[...]] * n_grp, axis=1)))
            bias_ref[:, pl.ds(start, tq)] = jnp.where(take, 0.0, NEG).astype(bias_ref.dtype)
            return seen + jnp.dot(eq_b, all_ones, preferred_element_type=F32)

        lax.fori_loop(0, i + 1, write_body, jnp.zeros((tq, LANES), F32))

    def fill_body(j, carry):
        bias_ref[:, pl.ds(pl.multiple_of(j * tq, tq), tq)] = jnp.full((tq, tq), NEG, bias_ref.dtype)
        return carry

    lax.fori_loop(i + 1, n_chunks, fill_body, 0)


def _dsa_select_bias(iq, ik2, iw, n_sel, tq=TQ_SELECT):
    s = iq.shape[0]
    tq = min(tq, s)
    assert s % tq == 0 and tq % BISECT_ROWS == 0 and (s + 2 * tq) // LANES <= 256
    body = functools.partial(_indexer_body, tq=tq, n_sel=n_sel)
    return pl.pallas_call(
        body,
        grid=(s // tq,),
        in_specs=[pl.BlockSpec((tq, IDX_HEADS * IDX_DIM), lambda i: (i, 0)),
                  pl.BlockSpec((s, LANES), lambda i: (0, 0)),
                  pl.BlockSpec((tq, LANES), lambda i: (i, 0))],
        out_specs=pl.BlockSpec((tq, s), lambda i: (i, 0)),
        out_shape=jax.ShapeDtypeStruct((s, s), BF16),
        scratch_shapes=[pltpu.VMEM((tq, s + tq), jnp.int32),
                        pltpu.VMEM((IDX_HEADS, tq, LANES), BF16),
                        pltpu.VMEM((IDX_HEADS, tq, LANES), F32),
                        pltpu.VMEM((tq, LANES), jnp.int32),
                        pltpu.VMEM((tq, LANES), F32)],
        compiler_params=_cparams("arbitrary"),
        name="dsa_select",
    )(iq, ik2, iw)


def _fox_cum_body(f_ref, b_ref, o_ref, carry_ref, *, tc):
    @pl.when(pl.program_id(0) == 0)
    def _():
        carry_ref[...] = jnp.zeros(carry_ref.shape, F32)

    z = f_ref[...] + b_ref[...]
    lf = jnp.minimum(z, 0.0) - jnp.log(1.0 + jnp.exp(-jnp.abs(z)))
    row = lax.broadcasted_iota(jnp.int32, (tc, tc), 0)
    col = lax.broadcasted_iota(jnp.int32, (tc, tc), 1)
    tri = (row >= col).astype(BF16)
    hi = lf.astype(BF16)
    r1 = lf - hi.astype(F32)
    mid = r1.astype(BF16)
    lo = (r1 - mid.astype(F32)).astype(BF16)
    cs = (jnp.dot(tri, hi, preferred_element_type=F32) + jnp.dot(tri, mid, preferred_element_type=F32)
          + jnp.dot(tri, lo, preferred_element_type=F32)) + carry_ref[...]
    o_ref[...] = cs
    carry_ref[...] = cs[tc - 1:tc, :]


def _fox_cumsum(f_raw, b_pad, tc=T_CUMSUM):
    s = f_raw.shape[0]
    tc = min(tc, s)
    return pl.pallas_call(
        functools.partial(_fox_cum_body, tc=tc),
        grid=(s // tc,),
        in_specs=[pl.BlockSpec((tc, LANES), lambda m: (m, 0)), pl.BlockSpec((1, LANES), lambda m: (0, 0))],
        out_specs=pl.BlockSpec((tc, LANES), lambda m: (m, 0)),
        out_shape=jax.ShapeDtypeStruct((s, LANES), F32),
        scratch_shapes=[pltpu.VMEM((1, LANES), F32)],
        compiler_params=_cparams("arbitrary"),
        name="fox_cumsum",
    )(f_raw, b_pad)


def _rope_tables(positions, rot_dim, period):
    half = rot_dim // 2
    inv_freq = ROPE_THETA ** (-jnp.arange(0, rot_dim, 2, dtype=F32) / rot_dim)
    ang = positions.astype(F32)[:, None] * inv_freq
    cos, sin = jnp.cos(ang), jnp.sin(ang)
    lp = jnp.arange(LANES) % period
    in_x1 = lp < half
    in_x2 = (lp >= half) & (lp < 2 * half)
    idx = jnp.where(in_x1, lp, jnp.clip(lp - half, 0, half - 1))
    cos_l, sin_l = cos[:, idx], sin[:, idx]
    c = jnp.where(in_x1 | in_x2, cos_l, 1.0)
    s1 = jnp.where(in_x2, sin_l, 0.0)
    s2 = jnp.where(in_x1, -sin_l, 0.0)
    return c, s1, s2


def _mla_mixer(h, tabs, w_in, q_a_g, kv_a_g, w_q_b, w_kv_b, q_g, k_g):
    s = h.shape[0]
    nope = HEAD_DIM
    scale = MLA_HEAD ** -0.5 * LOG2E
    gains = jnp.concatenate([q_a_g, kv_a_g])
    wt = jnp.swapaxes(w_in, 1, 2)
    c_norm = _proj(h, wt, 0, 0, 2 * MLA_LORA, tn=MLA_LORA, gain=gains, gw=MLA_LORA, real=MLA_LORA, w_t=True)
    kpe = _proj(h, wt, 0, 2 * MLA_LORA, LANES, tn=LANES, out_dtype=F32,
                epi=functools.partial(_epi_keep_lanes, n_keep=MLA_ROPE), w_t=True)
    wq = w_q_b[0].reshape(MLA_LORA, N_HEADS, MLA_HEAD)
    wq = jnp.concatenate([wq[:, :, MLA_ROPE:], wq[:, :, :MLA_ROPE],
                          jnp.zeros((MLA_LORA, N_HEADS, 2 * LANES - MLA_HEAD), F32)], axis=2)
    wq = wq.reshape(1, MLA_LORA, N_HEADS * 2 * LANES)
    qg = jnp.concatenate([q_g[MLA_ROPE:], q_g[:MLA_ROPE], jnp.zeros((2 * LANES - MLA_HEAD,), F32)])
    q = _proj(c_norm, wq, 0, 0, N_HEADS * 2 * LANES, tn=TN_WIDE, gain=jnp.tile(qg, N_HEADS), tabs=tabs,
              gw=2 * LANES, real=MLA_HEAD, half=MLA_ROPE // 2, rope_groups=lambda g: g % 2 == 1, scale=scale)
    tm, tn = min(TM, s), TN
    aux = [(kpe, (tm, LANES), lambda n, m: (m, 0)),
           (k_g[MLA_ROPE:].reshape(1, nope), (1, nope), lambda n, m: (0, 0)),
           (jnp.pad(k_g[:MLA_ROPE], (0, LANES - MLA_ROPE)).reshape(1, LANES), (1, LANES), lambda n, m: (0, 0))]
    aux += _row_tab_aux(tabs, tm)
    outs = [((s, N_HEADS * 2 * LANES), BF16, (tm, tn), lambda n, m: (m, n)),
            ((s, N_HEADS * HEAD_DIM), BF16, (tm, tn // 2), lambda n, m: (m, n))]
    k, v = _mm(c_norm, [(w_kv_b, 0, 0)], aux, outs, _epi_mla_kv, tm=tm, tn=tn,
               n_tiles=N_HEADS * 2 * LANES // tn, a_blk=1, name="mla_kv")
    return _dense_attention(q, k, v, mode="mla", dq=2 * LANES)


def _dsa_mixer(h, tabs_head, tabs_idx, w_in, q_g, k_g, idx_k_g):
    s = h.shape[0]
    hd = N_HEADS * HEAD_DIM
    ih = IDX_HEADS * IDX_DIM
    half = HEAD_DIM // 8
    wt = jnp.swapaxes(w_in, 1, 2)
    q = _proj(h, wt, 0, 0, hd, tn=TN_WIDE, gain=jnp.tile(q_g, N_HEADS), tabs=tabs_head, half=half,
              scale=HEAD_DIM ** -0.5 * LOG2E, w_t=True)
    k = _proj(h, wt, 0, hd, hd, tn=TN_WIDE, gain=jnp.tile(k_g, N_HEADS), tabs=tabs_head, half=half, w_t=True)
    v = _proj(h, wt, 0, 2 * hd, hd, tn=TN_WIDE, epi=_epi_plain, w_t=True)
    iq = _proj(h, wt, 0, 3 * hd, ih, tn=TN_WIDE, tabs=tabs_idx, half=IDX_DIM // 8, w_t=True)
    assert (3 * hd + ih) % LANES == 0 and w_in.shape[2] - (3 * hd + ih) == IDX_DIM + IDX_HEADS
    tm = min(TM_LONG, s)
    aux = [(jnp.pad(idx_k_g, (0, LANES - IDX_DIM)).reshape(1, LANES), (1, LANES), lambda n, m: (0, 0))]
    aux += _row_tab_aux(tabs_idx, tm)
    outs = [((s, LANES), BF16, (tm, LANES), lambda n, m: (m, 0)), ((s, LANES), F32, (tm, LANES), lambda n, m: (m, 0))]
    ik2, iw = _mm(h, [(wt, 0, (3 * hd + ih) // LANES)], aux, outs, _epi_index_key_and_weights, tm=tm, tn=LANES,
                  n_tiles=1, name="proj", w_t=True)
    bias = _dsa_select_bias(iq, ik2, iw, min(IDX_TOPK, s // 4))
    return _dsa_attention(q, k, v, bias)


def _diff_mixer(h, tabs, layer_idx, w_in, q_g, k_g, lam_params, subln_g):
    w = N_HEADS * 2 * DIFF_DIM
    half = DIFF_DIM // 8
    q = _proj(h, w_in, 0, 0, w, tn=TN_WIDE, gain=jnp.tile(q_g, 2 * N_HEADS), tabs=tabs, gw=DIFF_DIM, real=DIFF_DIM,
              half=half, scale=DIFF_DIM ** -0.5 * LOG2E)
    k = _proj(h, w_in, 0, w, w, tn=TN_WIDE, gain=jnp.tile(k_g, 2 * N_HEADS), tabs=tabs, gw=DIFF_DIM, real=DIFF_DIM,
              half=half)
    v = _proj(h, w_in, 0, 2 * w, w, tn=TN_WIDE, epi=_epi_plain)
    lam_init = 0.8 - 0.6 * math.exp(-0.3 * layer_idx)
    return _dense_attention(q, k, v, mode="diff", dq=HEAD_DIM, extra=(lam_params, subln_g.reshape(1, HEAD_DIM)),
                            lam_init=lam_init)


def _fox_mixer(h, w_in, b_f, q_g, k_g):
    s = h.shape[0]
    hd = N_HEADS * HEAD_DIM
    wt = jnp.swapaxes(w_in, 1, 2)
    q = _proj(h, wt, 0, 0, hd, tn=TN_WIDE, gain=jnp.tile(q_g, N_HEADS), scale=HEAD_DIM ** -0.5 * LOG2E, w_t=True)
    k = _proj(h, wt, 0, hd, hd, tn=TN_WIDE, gain=jnp.tile(k_g, N_HEADS), w_t=True)
    v = _proj(h, wt, 0, 2 * hd, hd, tn=TN_WIDE, epi=_epi_plain, w_t=True)
    f_raw = _proj(h, wt, 0, 3 * hd, LANES, tn=LANES, out_dtype=F32,
                  epi=functools.partial(_epi_keep_lanes, n_keep=N_HEADS), w_t=True)
    gate_blk = 3 * hd // TN
    tm = min(TM, s)
    gate = _mm(h, [(wt, 0, gate_blk), (wt, 0, gate_blk + 1)], [],
               [((s, hd), F32, (tm, TN), lambda n, m: (m, n))], _epi_sigmoid, tm=tm, tn=TN, n_tiles=hd // TN,
               name="proj", lane_shift=N_HEADS, w_t=True)[0]
    cum = _fox_cumsum(f_raw, jnp.pad(b_f, (0, LANES - N_HEADS)).reshape(1, LANES))
    cumt = cum[:, :N_HEADS].T.reshape(N_HEADS, 1, s)
    return _dense_attention(q, k, v, mode="fox", dq=HEAD_DIM, extra=(cum, cumt, gate))


def _swiglu_ffn(h, x, gate, w_gate_up, w_down, layer):
    s = h.shape[0]
    tm, tn = min(TM_LONG, s), TN
    outs = [((s, FFN_HIDDEN), BF16, (tm, tn), lambda n, m: (m, n))]
    act = _mm(h, [(w_gate_up, layer, 0), (w_gate_up, layer, FFN_HIDDEN // tn)], [], outs, _epi_swiglu,
              tm=tm, tn=tn, n_tiles=FFN_HIDDEN // tn, name="ffn_gate_up")[0]
    return _residual_mm(act, w_down, layer, x, gate, tm=TM_DEEP)


def kernel(x, c, positions, ln_mix_g, ln_ffn_g, ada_w, ada_b, ffn_w_gate_up, ffn_w_down, mla_w_in, mla_q_a_g, mla_kv_a_g, mla_w_q_b, mla_w_kv_b, mla_q_g, mla_k_g, mla_w_out, dsa_w_in, dsa_q_g, dsa_k_g, dsa_idx_k_g, dsa_w_out, diff_w_in, diff_q_g, diff_k_g, diff_lambda_q1, diff_lambda_k1, diff_lambda_q2, diff_lambda_k2, diff_subln_g, diff_w_out, fox_w_in, fox_b_f, fox_q_g, fox_k_g, fox_w_out):
    batch, s, d = x.shape
    assert batch == 1 and d == D_MODEL
    depth = ada_w.shape[0]
    pos = positions[0]
    tabs_head = _rope_tables(pos, HEAD_DIM // 4, LANES)
    tabs_small = _rope_tables(pos, IDX_DIM // 4, IDX_DIM)
    tabs_mla = _rope_tables(pos, MLA_ROPE, LANES)
    mod = _ada_mod(c, ada_w, ada_b)
    xs = x[0]
    for i in range(depth):
        sh1, sc1, g1, sh2, sc2, g2 = [mod[i, t * d:(t + 1) * d] for t in range(6)]
        h = _normmod(xs, ln_mix_g[i], sc1, sh1)
        kind, j = i % 4, i // 4
        if kind == 0:
            o = _mla_mixer(h, tabs_mla, mla_w_in[j:j + 1], mla_q_a_g[j], mla_kv_a_g[j], mla_w_q_b[j:j + 1],
                           mla_w_kv_b[j:j + 1], mla_q_g[j], mla_k_g[j])
            w_out = mla_w_out
        elif kind == 1:
            o = _dsa_mixer(h, tabs_head, tabs_small, dsa_w_in[j:j + 1], dsa_q_g[j], dsa_k_g[j], dsa_idx_k_g[j])
            w_out = dsa_w_out
        elif kind == 2:
            lam_params = jnp.stack([diff_lambda_q1[j], diff_lambda_k1[j], diff_lambda_q2[j], diff_lambda_k2[j]])
            o = _diff_mixer(h, tabs_small, i, diff_w_in[j:j + 1], diff_q_g[j], diff_k_g[j], lam_params,
                            diff_subln_g[j])
            w_out = diff_w_out
        else:
            o = _fox_mixer(h, fox_w_in[j:j + 1], fox_b_f[j], fox_q_g[j], fox_k_g[j])
            w_out = fox_w_out
        xs = _residual_mm(o, w_out, j, xs, g1, tn=TN_WIDE)
        h = _normmod(xs, ln_ffn_g[i], sc2, sh2)
        xs = _swiglu_ffn(h, xs, g2, ffn_w_gate_up, ffn_w_down, i)
    return xs[None]
```
